```python
import jax, jax.numpy as jnp
from jax import lax
import numpy as np

D_MODEL = 2048
BATCH = 4
SEQ = 4096
DEPTH = 1

N_HEADS = 16
HEAD_DIM = 128
D_ATTN = N_HEADS * HEAD_DIM
MOBA_BLOCK = 256
MOBA_TOPK = 3
Q_CHUNK = 16
ROPE_THETA = 500000.0
ROT_DIM = HEAD_DIM // 4
D_RNN = 2048
N_RNN_BLOCKS = 16
RNN_BLOCK = D_RNN // N_RNN_BLOCKS
CONV_WIDTH = 4
LRU_C = 8.0
MEM_LEN = 256
MEM_HEADS = 4
MEM_HEAD_DIM = 128
D_MEM = MEM_HEADS * MEM_HEAD_DIM
D_FF = -(-8 * D_MODEL // (3 * 256)) * 256
IN_WIDTH = 2 * D_RNN + 3 * D_ATTN + 2 * D_MODEL
RMS_EPS = 1e-6
NEG_INF = -1e30

kernel_name = "hawk_moba_gated_hybrid"


def rmsnorm(x, g):
    xf = x.astype(jnp.float32)
    y = xf * lax.rsqrt(jnp.mean(xf * xf, axis=-1, keepdims=True) + RMS_EPS)
    return (y * g.astype(jnp.float32)).astype(x.dtype)


def partial_rope(x, positions):
    half = ROT_DIM // 2
    inv_freq = jnp.power(ROPE_THETA, -jnp.arange(half, dtype=jnp.float32) / half)
    ang = positions.astype(jnp.float32)[..., None] * inv_freq
    cos = jnp.cos(ang)[:, :, None, :]
    sin = jnp.sin(ang)[:, :, None, :]
    xf = x.astype(jnp.float32)
    x1 = xf[..., :half]
    x2 = xf[..., half:ROT_DIM]
    out = jnp.concatenate([x1 * cos - x2 * sin, x2 * cos + x1 * sin, xf[..., ROT_DIM:]], axis=-1)
    return out.astype(x.dtype)


def causal_depthwise_conv(x, w, b):
    y = lax.conv_general_dilated(
        x, w[:, None, :].astype(x.dtype), window_strides=(1,),
        padding=[(CONV_WIDTH - 1, 0)], dimension_numbers=("NWC", "WIO", "NWC"),
        feature_group_count=x.shape[-1])
    return y + b.astype(x.dtype)


def rg_lru(x, w_a, b_a, w_i, b_i, lam):
    B, S, _ = x.shape
    xb = x.reshape(B, S, N_RNN_BLOCKS, RNN_BLOCK)
    r = jax.nn.sigmoid(jnp.einsum('bsnk,nkj->bsnj', xb, w_a, preferred_element_type=jnp.float32)
                       .reshape(B, S, D_RNN) + b_a.astype(jnp.float32))
    i = jax.nn.sigmoid(jnp.einsum('bsnk,nkj->bsnj', xb, w_i, preferred_element_type=jnp.float32)
                       .reshape(B, S, D_RNN) + b_i.astype(jnp.float32))
    log_a = -LRU_C * r * jax.nn.softplus(-lam.astype(jnp.float32))
    a = jnp.exp(log_a)
    u = jnp.sqrt(-jnp.expm1(2.0 * log_a)) * (i * x.astype(jnp.float32))

    def combine(left, right):
        a1, b1 = left
        a2, b2 = right
        return a1 * a2, a2 * b1 + b2

    _, h = lax.associative_scan(combine, (a, u), axis=1)
    return h.astype(x.dtype)


def moba_attention(q, k, v):
    B, H, S, dh = q.shape
    nb = -(-S // MOBA_BLOCK)
    s_pad = nb * MOBA_BLOCK
    k_sel = min(MOBA_TOPK, nb)
    scale = dh ** -0.5
    pad = ((0, 0), (0, 0), (0, s_pad - S), (0, 0))
    kp = jnp.pad(k, pad)
    vp = jnp.pad(v, pad)
    k_blocks = kp.reshape(B, H, nb, MOBA_BLOCK, dh)
    v_blocks = vp.reshape(B, H, nb, MOBA_BLOCK, dh)
    k_mean = jnp.mean(k_blocks.astype(jnp.float32), axis=3)

    q_blk = jnp.arange(S) // MOBA_BLOCK
    gate = jnp.einsum('bhsd,bhnd->bhsn', q.astype(jnp.float32), k_mean)
    past = jnp.arange(nb)[None, :] < q_blk[:, None]
    gate = jnp.where(past, gate, -jnp.inf)
    _, sel_idx = lax.top_k(gate, k_sel)

    bi = jnp.arange(B)[:, None, None, None]
    hi = jnp.arange(H)[None, :, None, None]
    sel_width = k_sel * MOBA_BLOCK

    def chunk(c):
        start = c * Q_CHUNK
        qc = lax.dynamic_slice_in_dim(q, start, Q_CHUNK, axis=2)
        idx = lax.dynamic_slice_in_dim(sel_idx, start, Q_CHUNK, axis=2)
        pos = start + jnp.arange(Q_CHUNK)
        n_past = pos // MOBA_BLOCK
        valid = jnp.arange(k_sel)[None, :] < n_past[:, None]
        kg = k_blocks[bi, hi, idx]
        vg = v_blocks[bi, hi, idx]
        s_sel = jnp.einsum('bhqd,bhqnkd->bhqnk', qc, kg, preferred_element_type=jnp.float32) * scale
        s_sel = jnp.where(valid[None, None, :, :, None], s_sel, NEG_INF)
        own_start = (start // MOBA_BLOCK) * MOBA_BLOCK
        ko = lax.dynamic_slice_in_dim(kp, own_start, MOBA_BLOCK, axis=2)
        vo = lax.dynamic_slice_in_dim(vp, own_start, MOBA_BLOCK, axis=2)
        s_own = jnp.einsum('bhqd,bhkd->bhqk', qc, ko, preferred_element_type=jnp.float32) * scale
        causal = (own_start + jnp.arange(MOBA_BLOCK))[None, :] <= pos[:, None]
        s_own = jnp.where(causal[None, None], s_own, NEG_INF)
        s_all = jnp.concatenate([s_sel.reshape(B, H, Q_CHUNK, sel_width), s_own], axis=-1)
        p = jax.nn.softmax(s_all, axis=-1)
        p_sel = p[..., :sel_width].reshape(B, H, Q_CHUNK, k_sel, MOBA_BLOCK).astype(v.dtype)
        p_own = p[..., sel_width:].astype(v.dtype)
        o = (jnp.einsum('bhqnk,bhqnkd->bhqd', p_sel, vg, preferred_element_type=jnp.float32)
             + jnp.einsum('bhqk,bhkd->bhqd', p_own, vo, preferred_element_type=jnp.float32))
        return o.astype(q.dtype)

    out = lax.map(chunk, jnp.arange(S // Q_CHUNK))
    return out.transpose(1, 2, 0, 3, 4).reshape(B, H, S, dh)


def hybrid_mixer(h, positions, w_in, conv_w, conv_b, w_a, b_a, w_i, b_i, lam,
                 w_rnn_proj, w_attn_proj, w_mix_out):
    B, S, _ = h.shape
    proj = h @ w_in
    o0 = 0
    xr = proj[..., o0:o0 + D_RNN]; o0 += D_RNN
    yr = proj[..., o0:o0 + D_RNN]; o0 += D_RNN
    q = proj[..., o0:o0 + D_ATTN]; o0 += D_ATTN
    k = proj[..., o0:o0 + D_ATTN]; o0 += D_ATTN
    v = proj[..., o0:o0 + D_ATTN]; o0 += D_ATTN
    g_rnn = proj[..., o0:o0 + D_MODEL]; o0 += D_MODEL
    g_attn = proj[..., o0:o0 + D_MODEL]

    hr = rg_lru(causal_depthwise_conv(xr, conv_w, conv_b), w_a, b_a, w_i, b_i, lam)
    rnn_out = (jax.nn.gelu(yr) * hr) @ w_rnn_proj

    q = partial_rope(q.reshape(B, S, N_HEADS, HEAD_DIM), positions).transpose(0, 2, 1, 3)
    k = partial_rope(k.reshape(B, S, N_HEADS, HEAD_DIM), positions).transpose(0, 2, 1, 3)
    v = v.reshape(B, S, N_HEADS, HEAD_DIM).transpose(0, 2, 1, 3)
    o = moba_attention(q, k, v).transpose(0, 2, 1, 3).reshape(B, S, D_ATTN)
    attn_out = o @ w_attn_proj

    merged = jax.nn.sigmoid(g_rnn) * rnn_out + jax.nn.sigmoid(g_attn) * attn_out
    return merged @ w_mix_out


def memory_cross_attention(h, mem_n, w_q, w_kv, w_o):
    B, S, _ = h.shape
    M = mem_n.shape[1]
    q = (h @ w_q).reshape(B, S, MEM_HEADS, MEM_HEAD_DIM)
    kv = mem_n @ w_kv
    k = kv[..., :D_MEM].reshape(B, M, MEM_HEADS, MEM_HEAD_DIM)
    v = kv[..., D_MEM:].reshape(B, M, MEM_HEADS, MEM_HEAD_DIM)
    s = jnp.einsum('bshd,bmhd->bhsm', q, k, preferred_element_type=jnp.float32) * (MEM_HEAD_DIM ** -0.5)
    p = jax.nn.softmax(s, axis=-1).astype(v.dtype)
    o = jnp.einsum('bhsm,bmhd->bshd', p, v)
    return o.reshape(B, S, D_MEM) @ w_o


def swiglu(h, w_gate, w_up, w_down):
    return (jax.nn.silu(h @ w_gate) * (h @ w_up)) @ w_down


def setup_inputs(seed: int = 0) -> dict:
    key = jax.random.key(seed)
    ks = jax.random.split(key, 32)
    f32 = jnp.float32

    def nrm(k, shape, fan_in):
        return jax.random.normal(k, shape, f32) * (fan_in ** -0.5)

    def gain(k, shape):
        return 1.0 + 0.02 * jax.random.normal(k, shape, f32)

    def bias(k, shape):
        return 0.01 * jax.random.normal(k, shape, f32)

    L = DEPTH
    x = jax.random.normal(ks[0], (BATCH, SEQ, D_MODEL), f32)
    mem = jax.random.normal(ks[1], (BATCH, MEM_LEN, D_MODEL), f32)
    offset = jax.random.randint(ks[2], (BATCH, 1), 0, 1024, dtype=jnp.int32)
    positions = (offset + jnp.arange(SEQ, dtype=jnp.int32)[None, :]).astype(jnp.int32)
    a_init = jax.random.uniform(ks[3], (L, D_RNN), f32, 0.9, 0.999)
    lru_lambda = jnp.log(a_init) - jnp.log1p(-a_init)
    return {
        "x": x,
        "mem": mem,
        "positions": positions,
        "norm_mix_g": gain(ks[4], (L, D_MODEL)),
        "w_in": nrm(ks[5], (L, D_MODEL, IN_WIDTH), D_MODEL),
        "conv_w": nrm(ks[6], (L, CONV_WIDTH, D_RNN), CONV_WIDTH),
        "conv_b": bias(ks[7], (L, D_RNN)),
        "lru_w_a": nrm(ks[8], (L, N_RNN_BLOCKS, RNN_BLOCK, RNN_BLOCK), RNN_BLOCK),
        "lru_b_a": bias(ks[9], (L, D_RNN)),
        "lru_w_i": nrm(ks[10], (L, N_RNN_BLOCKS, RNN_BLOCK, RNN_BLOCK), RNN_BLOCK),
        "lru_b_i": bias(ks[11], (L, D_RNN)),
        "lru_lambda": lru_lambda,
        "w_rnn_proj": nrm(ks[12], (L, D_RNN, D_MODEL), D_RNN),
        "w_attn_proj": nrm(ks[13], (L, D_ATTN, D_MODEL), D_ATTN),
        "w_mix_out": nrm(ks[14], (L, D_MODEL, D_MODEL), D_MODEL),
        "norm_xq_g": gain(ks[15], (L, D_MODEL)),
        "norm_mem_g": gain(ks[16], (L, D_MODEL)),
        "w_xq": nrm(ks[17], (L, D_MODEL, D_MEM), D_MODEL),
        "w_xkv": nrm(ks[18], (L, D_MODEL, 2 * D_MEM), D_MODEL),
        "w_xo": nrm(ks[19], (L, D_MEM, D_MODEL), D_MEM),
        "norm_ffn_g": gain(ks[20], (L, D_MODEL)),
        "w_ffn_gate": nrm(ks[21], (L, D_MODEL, D_FF), D_MODEL),
        "w_ffn_up": nrm(ks[22], (L, D_MODEL, D_FF), D_MODEL),
        "w_ffn_down": nrm(ks[23], (L, D_FF, D_MODEL), D_FF),
        "norm_final_g": gain(ks[24], (D_MODEL,)),
    }


def reference(x, mem, positions, norm_mix_g, w_in, conv_w, conv_b, lru_w_a, lru_b_a,
              lru_w_i, lru_b_i, lru_lambda, w_rnn_proj, w_attn_proj, w_mix_out,
              norm_xq_g, norm_mem_g, w_xq, w_xkv, w_xo, norm_ffn_g, w_ffn_gate,
              w_ffn_up, w_ffn_down, norm_final_g):
    for l in range(DEPTH):
        h = rmsnorm(x, norm_mix_g[l])
        x = x + hybrid_mixer(h, positions, w_in[l], conv_w[l], conv_b[l], lru_w_a[l], lru_b_a[l],
                             lru_w_i[l], lru_b_i[l], lru_lambda[l], w_rnn_proj[l],
                             w_attn_proj[l], w_mix_out[l])
        h = rmsnorm(x, norm_xq_g[l])
        mem_n = rmsnorm(mem, norm_mem_g[l])
        x = x + memory_cross_attention(h, mem_n, w_xq[l], w_xkv[l], w_xo[l])
        h = rmsnorm(x, norm_ffn_g[l])
        x = x + swiglu(h, w_ffn_gate[l], w_ffn_up[l], w_ffn_down[l])
    return rmsnorm(x, norm_final_g)
```

```python
import functools

import jax
import jax.numpy as jnp
from jax import lax
from jax.experimental import pallas as pl
from jax.experimental.pallas import tpu as pltpu

D_MODEL = 2048
N_HEADS = 16
HEAD_DIM = 128
D_ATTN = N_HEADS * HEAD_DIM
MOBA_BLOCK = 256
MOBA_TOPK = 3
ROPE_THETA = 500000.0
ROT_DIM = HEAD_DIM // 4
ROT_HALF = ROT_DIM // 2
D_RNN = 2048
N_RNN_BLOCKS = 16
RNN_BLOCK = D_RNN // N_RNN_BLOCKS
CONV_WIDTH = 4
LRU_C = 8.0
MEM_HEADS = 4
MEM_HEAD_DIM = 128
D_MEM = MEM_HEADS * MEM_HEAD_DIM
RMS_EPS = 1e-6
NEG_INF = -1e30

LANES = 128
SUBLANES = 8
MIB = 1024 * 1024

BF16 = jnp.bfloat16
F32 = jnp.float32

_NT = (((1,), (1,)), ((), ()))
_TN = (((0,), (0,)), ((), ()))


def _params(semantics, vmem_mib):
    return pltpu.CompilerParams(dimension_semantics=semantics,
                                vmem_limit_bytes=vmem_mib * MIB)


def _rms(x, g):
    ms = jnp.mean(x * x, axis=-1, keepdims=True)
    return x * lax.rsqrt(ms + RMS_EPS) * g


def _norm_kernel(x_ref, g_ref, o_ref):
    o_ref[...] = _rms(x_ref[...], g_ref[...]).astype(o_ref.dtype)


def _norm_bf16(x, g, tm=512):
    m, d = x.shape
    return pl.pallas_call(
        _norm_kernel,
        grid=(m // tm,),
        in_specs=[pl.BlockSpec((tm, d), lambda i: (i, 0)),
                  pl.BlockSpec((1, d), lambda i: (0, 0))],
        out_specs=pl.BlockSpec((tm, d), lambda i: (i, 0)),
        out_shape=jax.ShapeDtypeStruct((m, d), BF16),
        compiler_params=_params(("parallel",), 32),
    )(x, g.reshape(1, d))


def _rope_table_kernel(pos_ref, invf_ref, cos_ref, sin_ref):
    ang = pos_ref[...] * invf_ref[...]
    lane = lax.broadcasted_iota(jnp.int32, ang.shape, 1)
    c = jnp.cos(ang)
    s = jnp.sin(ang)
    cos_ref[...] = jnp.where(lane < ROT_DIM, c, 1.0)
    sin_ref[...] = jnp.where(lane < ROT_HALF, -s, jnp.where(lane < ROT_DIM, s, 0.0))


def _rope_tables(positions, tr=1024):
    n = positions.size
    pos = positions.astype(F32).reshape(n, 1)
    inv_freq = jnp.power(ROPE_THETA, -jnp.arange(ROT_HALF, dtype=F32) / ROT_HALF)
    invf = jnp.concatenate([inv_freq, inv_freq, jnp.zeros((LANES - ROT_DIM,), F32)]).reshape(1, LANES)
    return pl.pallas_call(
        _rope_table_kernel,
        grid=(n // tr,),
        in_specs=[pl.BlockSpec((tr, 1), lambda i: (i, 0)),
                  pl.BlockSpec((1, LANES), lambda i: (0, 0))],
        out_specs=[pl.BlockSpec((tr, LANES), lambda i: (i, 0)),
                   pl.BlockSpec((tr, LANES), lambda i: (i, 0))],
        out_shape=[jax.ShapeDtypeStruct((n, LANES), F32),
                   jax.ShapeDtypeStruct((n, LANES), F32)],
        compiler_params=_params(("parallel",), 32),
    )(pos, invf)


def _mm_plain_kernel(a_ref, w_ref, o_ref):
    acc = jnp.dot(a_ref[...], w_ref[...], preferred_element_type=F32)
    o_ref[...] = acc.astype(o_ref.dtype)


def _mm_sigmoid_kernel(a_ref, w_ref, o_ref):
    acc = jnp.dot(a_ref[...], w_ref[...], preferred_element_type=F32)
    o_ref[...] = jax.nn.sigmoid(acc).astype(o_ref.dtype)


def _mm_residual_kernel(a_ref, w_ref, r_ref, o_ref):
    acc = jnp.dot(a_ref[...], w_ref[...], preferred_element_type=F32)
    o_ref[...] = r_ref[...] + acc


def _mm_rope_kernel(a_ref, w_ref, cos_ref, sin_ref, o_ref, mean_ref):
    acc = jnp.dot(a_ref[...], w_ref[...], preferred_element_type=F32)
    tm, tn = acc.shape
    cosf = cos_ref[...]
    sinf = sin_ref[...]
    lane = lax.broadcasted_iota(jnp.int32, (tm, HEAD_DIM), 1)
    first_half = lane < ROT_HALF
    for hd in range(tn // HEAD_DIM):
        a = acc[:, hd * HEAD_DIM:(hd + 1) * HEAD_DIM]
        partner = jnp.where(first_half,
                            pltpu.roll(a, HEAD_DIM - ROT_HALF, 1),
                            pltpu.roll(a, ROT_HALF, 1))
        r = a * cosf + partner * sinf
        o_ref[:, hd * HEAD_DIM:(hd + 1) * HEAD_DIM] = r.astype(o_ref.dtype)
        mean_ref[0, :, hd * HEAD_DIM:(hd + 1) * HEAD_DIM] = jnp.mean(
            r.reshape(tm // MOBA_BLOCK, MOBA_BLOCK, HEAD_DIM), axis=1)


def _matmul(kernel, a, w, *, col_off, n_cols, out_dtype, tm=1024, tn=512,
            extra=(), extra_specs=(), extra_out_shape=(), extra_out_specs=(), vmem_mib=48):
    m, k = a.shape
    off = col_off // tn
    out_shape = [jax.ShapeDtypeStruct((m, n_cols), out_dtype)] + list(extra_out_shape)
    out_specs = [pl.BlockSpec((tm, tn), lambda i, j: (i, j))] + list(extra_out_specs)
    res = pl.pallas_call(
        kernel,
        grid=(m // tm, n_cols // tn),
        in_specs=[pl.BlockSpec((tm, k), lambda i, j: (i, 0)),
                  pl.BlockSpec((k, tn), lambda i, j: (0, j + off))] + list(extra_specs),
        out_specs=out_specs,
        out_shape=out_shape,
        compiler_params=_params(("parallel", "arbitrary"), vmem_mib),
    )(a, w, *extra)
    return res if extra_out_shape else res[0]


def _rglru_kernel(xr_ref, yr_ref, cw_ref, cb_ref, wa_ref, wi_ref, ba_ref, bi_ref, lam_ref,
                  o_ref, xbuf, a_scr, u_scr, h_scr, hc_scr):
    ts, tc = xr_ref.shape
    t = pl.program_id(2)

    @pl.when(t == 0)
    def _():
        xbuf[0:SUBLANES, :] = jnp.zeros((SUBLANES, tc), F32)
        hc_scr[...] = jnp.zeros_like(hc_scr)

    xbuf[SUBLANES:SUBLANES + ts, :] = xr_ref[...]
    xc = jnp.zeros((ts, tc), F32) + cb_ref[...]
    for kk in range(CONV_WIDTH):
        start = SUBLANES - (CONV_WIDTH - 1) + kk
        xc = xc + cw_ref[kk:kk + 1, :] * xbuf[start:start + ts, :]
    xbuf[0:SUBLANES, :] = xbuf[ts:ts + SUBLANES, :]

    sp = jax.nn.softplus(-lam_ref[...])
    for nb in range(tc // RNN_BLOCK):
        sl = slice(nb * RNN_BLOCK, (nb + 1) * RNN_BLOCK)
        xb = xc[:, sl]
        xb16 = xb.astype(BF16)
        r = jax.nn.sigmoid(jnp.dot(xb16, wa_ref[nb], preferred_element_type=F32) + ba_ref[:, sl])
        ig = jax.nn.sigmoid(jnp.dot(xb16, wi_ref[nb], preferred_element_type=F32) + bi_ref[:, sl])
        log_a = -LRU_C * r * sp[:, sl]
        a_scr[:, sl] = jnp.exp(log_a)
        th = jnp.tanh(log_a)
        u_scr[:, sl] = jnp.sqrt(-2.0 * th / (1.0 - th)) * (ig * xb)

    def step(row, h):
        h = a_scr[pl.ds(row, 1), :] * h + u_scr[pl.ds(row, 1), :]
        h_scr[pl.ds(row, 1), :] = h
        return h

    hc_scr[...] = lax.fori_loop(0, ts, step, hc_scr[...], unroll=8)
    o_ref[...] = (jax.nn.gelu(yr_ref[...]) * h_scr[...]).astype(o_ref.dtype)


def _rglru(xy, conv_w, conv_b, w_a, b_a, w_i, b_i, lam, batch, seq, ts=512, tc=512):
    n = batch * seq
    nt = seq // ts
    ncb = D_RNN // tc
    row = lambda v: v.reshape(1, D_RNN)
    vec_spec = pl.BlockSpec((1, tc), lambda b, c, t: (0, c))
    gate_spec = pl.BlockSpec((tc // RNN_BLOCK, RNN_BLOCK, RNN_BLOCK), lambda b, c, t: (c, 0, 0))
    return pl.pallas_call(
        _rglru_kernel,
        grid=(batch, ncb, nt),
        in_specs=[pl.BlockSpec((ts, tc), lambda b, c, t: (b * nt + t, c)),
                  pl.BlockSpec((ts, tc), lambda b, c, t: (b * nt + t, ncb + c)),
                  pl.BlockSpec((CONV_WIDTH, tc), lambda b, c, t: (0, c)),
                  vec_spec, gate_spec, gate_spec, vec_spec, vec_spec, vec_spec],
        out_specs=pl.BlockSpec((ts, tc), lambda b, c, t: (b * nt + t, c)),
        out_shape=jax.ShapeDtypeStruct((n, D_RNN), BF16),
        scratch_shapes=[pltpu.VMEM((ts + SUBLANES, tc), F32),
                        pltpu.VMEM((ts, tc), F32),
                        pltpu.VMEM((ts, tc), F32),
                        pltpu.VMEM((ts, tc), F32),
                        pltpu.VMEM((1, tc), F32)],
        compiler_params=_params(("parallel", "parallel", "arbitrary"), 32),
    )(xy, xy, conv_w, row(conv_b), w_a.astype(BF16), w_i.astype(BF16), row(b_a), row(b_i), row(lam))


def _moba_kernel(q_ref, k_ref, v_ref, km_ref, o_ref, bias_scr):
    j = pl.program_id(2)
    nblk = km_ref.shape[0]
    blk = MOBA_BLOCK
    scale = HEAD_DIM ** -0.5
    q = q_ref[...]

    km = km_ref[...]
    km_hi = km.astype(BF16)
    km_lo = (km - km_hi.astype(F32)).astype(BF16)
    gate = (lax.dot_general(km_hi, q, _NT, preferred_element_type=F32)
            + lax.dot_general(km_lo, q, _NT, preferred_element_type=F32))
    bidx = lax.broadcasted_iota(jnp.int32, gate.shape, 0)
    past = bidx < j
    g = jnp.where(past, gate, -jnp.inf)
    rank = jnp.zeros(gate.shape, jnp.int32)
    for other in range(nblk):
        go = g[other:other + 1, :]
        beats = jnp.where(go > g, 1, jnp.where(go == g, jnp.where(bidx > other, 1, 0), 0))
        rank = rank + beats
    bias_scr[...] = jnp.where(past, jnp.where(rank < MOBA_TOPK, 0.0, NEG_INF), NEG_INF)

    own = pl.multiple_of(j * blk, blk)
    k_o = k_ref[pl.ds(own, blk), :]
    v_o = v_ref[pl.ds(own, blk), :]
    s = lax.dot_general(k_o, q, _NT, preferred_element_type=F32) * scale
    kpos = lax.broadcasted_iota(jnp.int32, s.shape, 0)
    qpos = lax.broadcasted_iota(jnp.int32, s.shape, 1)
    s = jnp.where(kpos <= qpos, s, NEG_INF)
    m = jnp.max(s, axis=0, keepdims=True)
    p = jnp.exp(s - m)
    l = jnp.sum(p, axis=0, keepdims=True)
    acc = lax.dot_general(v_o, p.astype(BF16), _TN, preferred_element_type=F32)

    def body(i, carry):
        m, l, acc = carry
        start = pl.multiple_of(i * blk, blk)
        k_i = k_ref[pl.ds(start, blk), :]
        v_i = v_ref[pl.ds(start, blk), :]
        s = (lax.dot_general(k_i, q, _NT, preferred_element_type=F32) * scale
             + bias_scr[pl.ds(i, 1), :])
        m_new = jnp.maximum(m, jnp.max(s, axis=0, keepdims=True))
        alpha = jnp.exp(m - m_new)
        p = jnp.exp(s - m_new)
        l = alpha * l + jnp.sum(p, axis=0, keepdims=True)
        acc = alpha * acc + lax.dot_general(v_i, p.astype(BF16), _TN, preferred_element_type=F32)
        return m_new, l, acc

    m, l, acc = lax.fori_loop(0, j, body, (m, l, acc))
    o_ref[...] = (acc / l).T.astype(o_ref.dtype)


def _moba(qk, v, kmean, batch, seq):
    n = batch * seq
    nblk = seq // MOBA_BLOCK
    return pl.pallas_call(
        _moba_kernel,
        grid=(batch, N_HEADS, nblk),
        in_specs=[pl.BlockSpec((MOBA_BLOCK, HEAD_DIM), lambda b, h, j: (b * nblk + j, h)),
                  pl.BlockSpec((seq, HEAD_DIM), lambda b, h, j: (b, N_HEADS + h)),
                  pl.BlockSpec((seq, HEAD_DIM), lambda b, h, j: (b, h)),
                  pl.BlockSpec((nblk, HEAD_DIM), lambda b, h, j: (b, N_HEADS + h))],
        out_specs=pl.BlockSpec((MOBA_BLOCK, HEAD_DIM), lambda b, h, j: (b * nblk + j, h)),
        out_shape=jax.ShapeDtypeStruct((n, D_ATTN), BF16),
        scratch_shapes=[pltpu.VMEM((nblk, MOBA_BLOCK), F32)],
        compiler_params=_params(("parallel", "parallel", "arbitrary"), 32),
    )(qk, qk, v, kmean)


def _merge_kernel(gh_ref, o_ref, wr_ref, wa_ref, gr_ref, ga_ref, out_ref):
    rnn = jnp.dot(gh_ref[...], wr_ref[...], preferred_element_type=F32)
    att = jnp.dot(o_ref[...], wa_ref[...], preferred_element_type=F32)
    out_ref[...] = (gr_ref[...].astype(F32) * rnn + ga_ref[...].astype(F32) * att).astype(out_ref.dtype)


def _merge(gh, o, w_rnn, w_attn, gates, tm=1024, tn=512):
    m, k = gh.shape
    nj = D_MODEL // tn
    a_spec = pl.BlockSpec((tm, k), lambda i, j: (i, 0))
    w_spec = pl.BlockSpec((k, tn), lambda i, j: (0, j))
    return pl.pallas_call(
        _merge_kernel,
        grid=(m // tm, nj),
        in_specs=[a_spec, a_spec, w_spec, w_spec,
                  pl.BlockSpec((tm, tn), lambda i, j: (i, j)),
                  pl.BlockSpec((tm, tn), lambda i, j: (i, nj + j))],
        out_specs=pl.BlockSpec((tm, tn), lambda i, j: (i, j)),
        out_shape=jax.ShapeDtypeStruct((m, D_MODEL), BF16),
        compiler_params=_params(("parallel", "arbitrary"), 48),
    )(gh, o, w_rnn, w_attn, gates, gates)


def _memkv_kernel(mem_ref, g_ref, w_ref, o_ref):
    hn = _rms(mem_ref[...], g_ref[...]).astype(BF16)
    o_ref[...] = jnp.dot(hn, w_ref[...], preferred_element_type=F32).astype(o_ref.dtype)


def _memkv(mem2d, g, w_kv, mem_len):
    m, d = mem2d.shape
    return pl.pallas_call(
        _memkv_kernel,
        grid=(m // mem_len,),
        in_specs=[pl.BlockSpec((mem_len, d), lambda i: (i, 0)),
                  pl.BlockSpec((1, d), lambda i: (0, 0)),
                  pl.BlockSpec((d, 2 * D_MEM), lambda i: (0, 0))],
        out_specs=pl.BlockSpec((mem_len, 2 * D_MEM), lambda i: (i, 0)),
        out_shape=jax.ShapeDtypeStruct((m, 2 * D_MEM), BF16),
        compiler_params=_params(("parallel",), 32),
    )(mem2d, g.reshape(1, d), w_kv)


def _xattn_kernel(x_ref, g_ref, wq_ref, kv_ref, wo_ref, o_ref):
    x = x_ref[...]
    hn = _rms(x, g_ref[...]).astype(BF16)
    q = jnp.dot(hn, wq_ref[...], preferred_element_type=F32).astype(BF16)
    scale = MEM_HEAD_DIM ** -0.5
    heads = []
    for hd in range(MEM_HEADS):
        sl = slice(hd * MEM_HEAD_DIM, (hd + 1) * MEM_HEAD_DIM)
        kh = kv_ref[:, sl]
        vh = kv_ref[:, D_MEM + hd * MEM_HEAD_DIM:D_MEM + (hd + 1) * MEM_HEAD_DIM]
        s = lax.dot_general(q[:, sl], kh, _NT, preferred_element_type=F32) * scale
        m = jnp.max(s, axis=-1, keepdims=True)
        p = jnp.exp(s - m)
        l = jnp.sum(p, axis=-1, keepdims=True)
        oh = jnp.dot(p.astype(BF16), vh, preferred_element_type=F32) / l
        heads.append(oh.astype(BF16))
    o_all = jnp.concatenate(heads, axis=-1)
    o_ref[...] = x + jnp.dot(o_all, wo_ref[...], preferred_element_type=F32)


def _xattn(x, g, w_q, kv, w_o, seq, mem_len, tm=512):
    m, d = x.shape
    per_batch = seq // tm
    return pl.pallas_call(
        _xattn_kernel,
        grid=(m // tm,),
        in_specs=[pl.BlockSpec((tm, d), lambda i: (i, 0)),
                  pl.BlockSpec((1, d), lambda i: (0, 0)),
                  pl.BlockSpec((d, D_MEM), lambda i: (0, 0)),
                  pl.BlockSpec((mem_len, 2 * D_MEM), lambda i: (i // per_batch, 0)),
                  pl.BlockSpec((D_MEM, d), lambda i: (0, 0))],
        out_specs=pl.BlockSpec((tm, d), lambda i: (i, 0)),
        out_shape=jax.ShapeDtypeStruct((m, d), F32),
        compiler_params=_params(("parallel",), 48),
    )(x, g.reshape(1, d), w_q, kv, w_o)


def _ffn_kernel(x_ref, g_ref, wg_ref, wu_ref, wd_ref, gf_ref, o_ref, h_scr, acc_scr):
    f = pl.program_id(1)

    @pl.when(f == 0)
    def _():
        h_scr[...] = _rms(x_ref[...], g_ref[...]).astype(BF16)
        acc_scr[...] = jnp.zeros_like(acc_scr)

    hn = h_scr[...]
    a = jnp.dot(hn, wg_ref[...], preferred_element_type=F32)
    b = jnp.dot(hn, wu_ref[...], preferred_element_type=F32)
    act = (jax.nn.silu(a) * b).astype(BF16)
    acc_scr[...] += jnp.dot(act, wd_ref[...], preferred_element_type=F32)

    @pl.when(f == pl.num_programs(1) - 1)
    def _():
        o_ref[...] = _rms(x_ref[...] + acc_scr[...], gf_ref[...])


def _ffn(x, g, w_gate, w_up, w_down, g_final, tm=512, tf=512):
    m, d = x.shape
    d_ff = w_gate.shape[1]
    return pl.pallas_call(
        _ffn_kernel,
        grid=(m // tm, d_ff // tf),
        in_specs=[pl.BlockSpec((tm, d), lambda i, f: (i, 0)),
                  pl.BlockSpec((1, d), lambda i, f: (0, 0)),
                  pl.BlockSpec((d, tf), lambda i, f: (0, f)),
                  pl.BlockSpec((d, tf), lambda i, f: (0, f)),
                  pl.BlockSpec((tf, d), lambda i, f: (f, 0)),
                  pl.BlockSpec((1, d), lambda i, f: (0, 0))],
        out_specs=pl.BlockSpec((tm, d), lambda i, f: (i, 0)),
        out_shape=jax.ShapeDtypeStruct((m, d), F32),
        scratch_shapes=[pltpu.VMEM((tm, d), BF16), pltpu.VMEM((tm, d), F32)],
        compiler_params=_params(("parallel", "arbitrary"), 48),
    )(x, g.reshape(1, d), w_gate, w_up, w_down, g_final.reshape(1, d))


def _layer(x2d, mem2d, cosf, sinf, batch, seq, mem_len, p):
    n = x2d.shape[0]
    w_in = p["w_in"].astype(BF16)
    hn = _norm_bf16(x2d, p["norm_mix_g"])

    c0 = 0
    xy = _matmul(_mm_plain_kernel, hn, w_in, col_off=c0, n_cols=2 * D_RNN, out_dtype=F32)
    c0 += 2 * D_RNN
    tm = 1024
    rope_spec = pl.BlockSpec((tm, LANES), lambda i, j: (i, 0))
    qk, means = _matmul(
        _mm_rope_kernel, hn, w_in, col_off=c0, n_cols=2 * D_ATTN, out_dtype=BF16, tm=tm,
        extra=(cosf, sinf), extra_specs=(rope_spec, rope_spec),
        extra_out_shape=(jax.ShapeDtypeStruct((n // tm, tm // MOBA_BLOCK, 2 * D_ATTN), F32),),
        extra_out_specs=(pl.BlockSpec((1, tm // MOBA_BLOCK, 512), lambda i, j: (i, 0, j)),))
    c0 += 2 * D_ATTN
    v = _matmul(_mm_plain_kernel, hn, w_in, col_off=c0, n_cols=D_ATTN, out_dtype=BF16)
    c0 += D_ATTN
    gates = _matmul(_mm_sigmoid_kernel, hn, w_in, col_off=c0, n_cols=2 * D_MODEL, out_dtype=BF16)

    gh = _rglru(xy, p["conv_w"], p["conv_b"], p["lru_w_a"], p["lru_b_a"], p["lru_w_i"],
                p["lru_b_i"], p["lru_lambda"], batch, seq)
    kmean = means.reshape(n // MOBA_BLOCK, 2 * D_ATTN)
    o = _moba(qk, v, kmean, batch, seq)

    merged = _merge(gh, o, p["w_rnn_proj"].astype(BF16), p["w_attn_proj"].astype(BF16), gates)
    x1 = _matmul(_mm_residual_kernel, merged, p["w_mix_out"].astype(BF16), col_off=0,
                 n_cols=D_MODEL, out_dtype=F32,
                 extra=(x2d,), extra_specs=(pl.BlockSpec((1024, 512), lambda i, j: (i, j)),))

    kv = _memkv(mem2d, p["norm_mem_g"], p["w_xkv"].astype(BF16), mem_len)
    x2 = _xattn(x1, p["norm_xq_g"], p["w_xq"].astype(BF16), kv, p["w_xo"].astype(BF16), seq, mem_len)
    return x2


def kernel(x, mem, positions, norm_mix_g, w_in, conv_w, conv_b, lru_w_a, lru_b_a, lru_w_i, lru_b_i,
           lru_lambda, w_rnn_proj, w_attn_proj, w_mix_out, norm_xq_g, norm_mem_g, w_xq, w_xkv, w_xo,
           norm_ffn_g, w_ffn_gate, w_ffn_up, w_ffn_down, norm_final_g):
    batch, seq, d = x.shape
    mem_len = mem.shape[1]
    assert w_in.shape[0] == 1, "only DEPTH == 1 is supported"
    x2d = x.reshape(batch * seq, d)
    mem2d = mem.reshape(batch * mem_len, d)
    cosf, sinf = _rope_tables(positions)
    p = dict(norm_mix_g=norm_mix_g[0], w_in=w_in[0], conv_w=conv_w[0], conv_b=conv_b[0],
             lru_w_a=lru_w_a[0], lru_b_a=lru_b_a[0], lru_w_i=lru_w_i[0], lru_b_i=lru_b_i[0],
             lru_lambda=lru_lambda[0], w_rnn_proj=w_rnn_proj[0], w_attn_proj=w_attn_proj[0],
             w_mix_out=w_mix_out[0], norm_xq_g=norm_xq_g[0], norm_mem_g=norm_mem_g[0],
             w_xq=w_xq[0], w_xkv=w_xkv[0], w_xo=w_xo[0])
    x2 = _layer(x2d, mem2d, cosf, sinf, batch, seq, mem_len, p)
    out = _ffn(x2, norm_ffn_g[0], w_ffn_gate[0].astype(BF16), w_ffn_up[0].astype(BF16),
               w_ffn_down[0].astype(BF16), norm_final_g)
    return out.reshape(batch, seq, d)
```

```python
import functools

import jax
import jax.numpy as jnp
from jax import lax
from jax.experimental import pallas as pl
from jax.experimental.pallas import tpu as pltpu

D_MODEL = 2048
N_HEADS = 16
HEAD_DIM = 128
D_ATTN = N_HEADS * HEAD_DIM
MOBA_BLOCK = 256
MOBA_TOPK = 3
ROPE_THETA = 500000.0
ROT_DIM = HEAD_DIM // 4
ROT_HALF = ROT_DIM // 2
D_RNN = 2048
N_RNN_BLOCKS = 16
RNN_BLOCK = D_RNN // N_RNN_BLOCKS
CONV_WIDTH = 4
LRU_C = 8.0
MEM_HEADS = 4
MEM_HEAD_DIM = 128
D_MEM = MEM_HEADS * MEM_HEAD_DIM
RMS_EPS = 1e-6
NEG_INF = -1e30
LOG2_E = 1.4426950408889634

LANES = 128
SUBLANES = 8
MIB = 1024 * 1024

BF16 = jnp.bfloat16
F32 = jnp.float32

_NT = (((1,), (1,)), ((), ()))
_TN = (((0,), (0,)), ((), ()))


def _params(semantics, vmem_mib):
    return pltpu.CompilerParams(dimension_semantics=semantics,
                                vmem_limit_bytes=vmem_mib * MIB)


def _rms(x, g):
    ms = jnp.mean(x * x, axis=-1, keepdims=True)
    return x * lax.rsqrt(ms + RMS_EPS) * g


def _norm_kernel(x_ref, g_ref, o_ref):
    o_ref[...] = _rms(x_ref[...], g_ref[...]).astype(o_ref.dtype)


def _norm_bf16(x, g, tm=512):
    m, d = x.shape
    return pl.pallas_call(
        _norm_kernel,
        grid=(m // tm,),
        in_specs=[pl.BlockSpec((tm, d), lambda i: (i, 0)),
                  pl.BlockSpec((1, d), lambda i: (0, 0))],
        out_specs=pl.BlockSpec((tm, d), lambda i: (i, 0)),
        out_shape=jax.ShapeDtypeStruct((m, d), BF16),
        name="norm_mix",
        compiler_params=_params(("parallel",), 32),
    )(x, g.reshape(1, d))


def _rope_table_kernel(pos_ref, invf_ref, cos_ref, sin_ref):
    ang = pos_ref[...] * invf_ref[...]
    lane = lax.broadcasted_iota(jnp.int32, ang.shape, 1)
    c = jnp.cos(ang)
    s = jnp.sin(ang)
    cos_ref[...] = jnp.where(lane < ROT_DIM, c, 1.0)
    sin_ref[...] = jnp.where(lane < ROT_HALF, -s, jnp.where(lane < ROT_DIM, s, 0.0))


def _rope_tables(positions, tr=1024):
    n = positions.size
    pos = positions.astype(F32).reshape(n, 1)
    inv_freq = jnp.power(ROPE_THETA, -jnp.arange(ROT_HALF, dtype=F32) / ROT_HALF)
    invf = jnp.concatenate([inv_freq, inv_freq, jnp.zeros((LANES - ROT_DIM,), F32)]).reshape(1, LANES)
    return pl.pallas_call(
        _rope_table_kernel,
        grid=(n // tr,),
        in_specs=[pl.BlockSpec((tr, 1), lambda i: (i, 0)),
                  pl.BlockSpec((1, LANES), lambda i: (0, 0))],
        out_specs=[pl.BlockSpec((tr, LANES), lambda i: (i, 0)),
                   pl.BlockSpec((tr, LANES), lambda i: (i, 0))],
        out_shape=[jax.ShapeDtypeStruct((n, LANES), F32),
                   jax.ShapeDtypeStruct((n, LANES), F32)],
        name="rope_tables",
        compiler_params=_params(("parallel",), 32),
    )(pos, invf)


def _mm_plain_kernel(a_ref, w_ref, o_ref):
    acc = jnp.dot(a_ref[...], w_ref[...], preferred_element_type=F32)
    o_ref[...] = acc.astype(o_ref.dtype)


def _mm_sigmoid_kernel(a_ref, w_ref, o_ref):
    acc = jnp.dot(a_ref[...], w_ref[...], preferred_element_type=F32)
    o_ref[...] = jax.nn.sigmoid(acc).astype(o_ref.dtype)


def _mm_residual_kernel(a_ref, w_ref, r_ref, o_ref):
    acc = jnp.dot(a_ref[...], w_ref[...], preferred_element_type=F32)
    o_ref[...] = r_ref[...] + acc


def _mm_rope_kernel(a_ref, w_ref, cos_ref, sin_ref, o_ref, mean_ref):
    acc = jnp.dot(a_ref[...], w_ref[...], preferred_element_type=F32)
    tm, tn = acc.shape
    cosf = cos_ref[...]
    sinf = sin_ref[...]
    lane = lax.broadcasted_iota(jnp.int32, (tm, HEAD_DIM), 1)
    first_half = lane < ROT_HALF
    for hd in range(tn // HEAD_DIM):
        a = acc[:, hd * HEAD_DIM:(hd + 1) * HEAD_DIM]
        partner = jnp.where(first_half,
                            pltpu.roll(a, HEAD_DIM - ROT_HALF, 1),
                            pltpu.roll(a, ROT_HALF, 1))
        r = a * cosf + partner * sinf
        o_ref[:, hd * HEAD_DIM:(hd + 1) * HEAD_DIM] = r.astype(o_ref.dtype)
        mean_ref[0, :, hd * HEAD_DIM:(hd + 1) * HEAD_DIM] = jnp.mean(
            r.reshape(tm // MOBA_BLOCK, MOBA_BLOCK, HEAD_DIM), axis=1)


def _matmul(kernel, name, a, w, *, col_off, n_cols, out_dtype, tm=1024, tn=512,
            extra=(), extra_specs=(), extra_out_shape=(), extra_out_specs=(), vmem_mib=48):
    m, k = a.shape
    off = col_off // tn
    out_shape = [jax.ShapeDtypeStruct((m, n_cols), out_dtype)] + list(extra_out_shape)
    out_specs = [pl.BlockSpec((tm, tn), lambda i, j: (i, j))] + list(extra_out_specs)
    res = pl.pallas_call(
        kernel,
        grid=(m // tm, n_cols // tn),
        in_specs=[pl.BlockSpec((tm, k), lambda i, j: (i, 0)),
                  pl.BlockSpec((k, tn), lambda i, j: (0, j + off))] + list(extra_specs),
        out_specs=out_specs,
        out_shape=out_shape,
        name=name,
        compiler_params=_params(("parallel", "arbitrary"), vmem_mib),
    )(a, w, *extra)
    return res if extra_out_shape else res[0]


def _rglru_kernel(xr_ref, yr_ref, cw_ref, cb_ref, wa_ref, wi_ref, ba_ref, bi_ref, lam_ref,
                  o_ref, xbuf, a_scr, u_scr, h_scr, hc_scr):
    ts, tc = xr_ref.shape
    t = pl.program_id(2)

    @pl.when(t == 0)
    def _():
        xbuf[0:SUBLANES, :] = jnp.zeros((SUBLANES, tc), F32)
        hc_scr[...] = jnp.zeros_like(hc_scr)

    xbuf[SUBLANES:SUBLANES + ts, :] = xr_ref[...]
    xc = jnp.zeros((ts, tc), F32) + cb_ref[...]
    for kk in range(CONV_WIDTH):
        start = SUBLANES - (CONV_WIDTH - 1) + kk
        xc = xc + cw_ref[kk:kk + 1, :] * xbuf[start:start + ts, :]
    xbuf[0:SUBLANES, :] = xbuf[ts:ts + SUBLANES, :]

    sp = jax.nn.softplus(-lam_ref[...])
    for nb in range(tc // RNN_BLOCK):
        sl = slice(nb * RNN_BLOCK, (nb + 1) * RNN_BLOCK)
        xb = xc[:, sl]
        xb16 = xb.astype(BF16)
        r = jax.nn.sigmoid(jnp.dot(xb16, wa_ref[nb], preferred_element_type=F32) + ba_ref[:, sl])
        ig = jax.nn.sigmoid(jnp.dot(xb16, wi_ref[nb], preferred_element_type=F32) + bi_ref[:, sl])
        log_a = -LRU_C * r * sp[:, sl]
        a_scr[:, sl] = jnp.exp(log_a)
        th = jnp.tanh(log_a)
        u_scr[:, sl] = jnp.sqrt(-2.0 * th / (1.0 - th)) * (ig * xb)

    def step(row, h):
        h = a_scr[pl.ds(row, 1), :] * h + u_scr[pl.ds(row, 1), :]
        h_scr[pl.ds(row, 1), :] = h
        return h

    hc_scr[...] = lax.fori_loop(0, ts, step, hc_scr[...], unroll=8)
    o_ref[...] = (jax.nn.gelu(yr_ref[...]) * h_scr[...]).astype(o_ref.dtype)


def _rglru(xy, conv_w, conv_b, w_a, b_a, w_i, b_i, lam, batch, seq, ts=512, tc=512):
    n = batch * seq
    nt = seq // ts
    ncb = D_RNN // tc
    row = lambda v: v.reshape(1, D_RNN)
    vec_spec = pl.BlockSpec((1, tc), lambda b, c, t: (0, c))
    gate_spec = pl.BlockSpec((tc // RNN_BLOCK, RNN_BLOCK, RNN_BLOCK), lambda b, c, t: (c, 0, 0))
    return pl.pallas_call(
        _rglru_kernel,
        grid=(batch, ncb, nt),
        in_specs=[pl.BlockSpec((ts, tc), lambda b, c, t: (b * nt + t, c)),
                  pl.BlockSpec((ts, tc), lambda b, c, t: (b * nt + t, ncb + c)),
                  pl.BlockSpec((CONV_WIDTH, tc), lambda b, c, t: (0, c)),
                  vec_spec, gate_spec, gate_spec, vec_spec, vec_spec, vec_spec],
        out_specs=pl.BlockSpec((ts, tc), lambda b, c, t: (b * nt + t, c)),
        out_shape=jax.ShapeDtypeStruct((n, D_RNN), BF16),
        scratch_shapes=[pltpu.VMEM((ts + SUBLANES, tc), F32),
                        pltpu.VMEM((ts, tc), F32),
                        pltpu.VMEM((ts, tc), F32),
                        pltpu.VMEM((ts, tc), F32),
                        pltpu.VMEM((1, tc), F32)],
        name="rglru",
        compiler_params=_params(("parallel", "parallel", "arbitrary"), 32),
    )(xy, xy, conv_w, row(conv_b), w_a.astype(BF16), w_i.astype(BF16), row(b_a), row(b_i), row(lam))


def _moba_kernel(q_ref, k_ref, v_ref, km_ref, o_ref, bias_scr, s_scr, *, heads):
    j = pl.program_id(2)
    nblk = km_ref.shape[0]
    blk = MOBA_BLOCK
    c = (HEAD_DIM ** -0.5) * LOG2_E
    head_slices = [slice(h * HEAD_DIM, (h + 1) * HEAD_DIM) for h in range(heads)]

    def score_into(slot, block):
        start = pl.multiple_of(block * blk, blk)
        for h, hs in enumerate(head_slices):
            s_scr[slot, h] = lax.dot_general(k_ref[pl.ds(start, blk), hs], q_ref[:, hs], _NT,
                                             preferred_element_type=F32)

    def attend(masked_scores, block, state):
        start = pl.multiple_of(block * blk, blk)
        soft = []
        for h in range(heads):
            m, l, _ = state[h]
            s = masked_scores(h)
            m_new = jnp.maximum(m, jnp.max(s, axis=0, keepdims=True))
            alpha = jnp.exp2((m - m_new) * c)
            p = jnp.exp2((s - m_new) * c)
            l = alpha * l + jnp.sum(p, axis=0, keepdims=True)
            soft.append((m_new, l, alpha, p.astype(BF16)))
        out = []
        for h, hs in enumerate(head_slices):
            m_new, l, alpha, p = soft[h]
            pv = lax.dot_general(v_ref[pl.ds(start, blk), hs], p, _TN,
                                 preferred_element_type=F32)
            out.append((m_new, l, alpha * state[h][2] + pv))
        return tuple(out)

    def past_scores(slot, block):
        return lambda h: s_scr[slot, h] + bias_scr[h, pl.ds(block, 1), :]

    score_into(0, 0)
    for h, hs in enumerate(head_slices):
        q = q_ref[:, hs]
        km = km_ref[:, hs]
        km_hi = km.astype(BF16)
        km_lo = (km - km_hi.astype(F32)).astype(BF16)
        gate = (lax.dot_general(km_hi, q, _NT, preferred_element_type=F32)
                + lax.dot_general(km_lo, q, _NT, preferred_element_type=F32))
        bidx = lax.broadcasted_iota(jnp.int32, gate.shape, 0)
        past = bidx < j
        g = jnp.where(past, gate, -jnp.inf)
        rank = jnp.zeros(gate.shape, jnp.int32)
        for other in range(nblk):
            go = g[other:other + 1, :]
            beats = jnp.where(go > g, 1, jnp.where(go == g, jnp.where(bidx > other, 1, 0), 0))
            rank = rank + beats
        bias_scr[h] = jnp.where(past, jnp.where(rank < MOBA_TOPK, 0.0, NEG_INF), NEG_INF)

    def pair_body(t, state):
        first = 2 * t
        score_into(1, first + 1)
        state = attend(past_scores(0, first), first, state)
        score_into(0, first + 2)
        return attend(past_scores(1, first + 1), first + 1, state)

    def odd_body(state):
        score_into(1, j)
        return attend(past_scores(0, j - 1), j - 1, state)

    init = tuple((jnp.full((1, blk), NEG_INF, F32), jnp.zeros((1, blk), F32),
                  jnp.zeros((HEAD_DIM, blk), F32)) for _ in range(heads))
    state = lax.fori_loop(0, lax.shift_right_logical(j, 1), pair_body, init)
    odd = lax.bitwise_and(j, 1)
    state = lax.cond(odd == 1, odd_body, lambda st: st, state)

    kpos = lax.broadcasted_iota(jnp.int32, (blk, blk), 0)
    qpos = lax.broadcasted_iota(jnp.int32, (blk, blk), 1)
    causal = kpos <= qpos
    state = attend(lambda h: jnp.where(causal, s_scr[odd, h], NEG_INF), j, state)
    for h, hs in enumerate(head_slices):
        _, l, acc = state[h]
        o_ref[:, hs] = (acc / l).T.astype(o_ref.dtype)


def _moba(qk, v, kmean, batch, seq, heads=4):
    n = batch * seq
    nblk = seq // MOBA_BLOCK
    groups = N_HEADS // heads
    width = heads * HEAD_DIM
    return pl.pallas_call(
        functools.partial(_moba_kernel, heads=heads),
        grid=(batch, groups, nblk),
        in_specs=[pl.BlockSpec((MOBA_BLOCK, width), lambda b, h, j: (b * nblk + j, h)),
                  pl.BlockSpec((seq, width), lambda b, h, j: (b, groups + h)),
                  pl.BlockSpec((seq, width), lambda b, h, j: (b, h)),
                  pl.BlockSpec((nblk, width), lambda b, h, j: (b, groups + h))],
        out_specs=pl.BlockSpec((MOBA_BLOCK, width), lambda b, h, j: (b * nblk + j, h)),
        out_shape=jax.ShapeDtypeStruct((n, D_ATTN), BF16),
        scratch_shapes=[pltpu.VMEM((heads, nblk, MOBA_BLOCK), F32),
                        pltpu.VMEM((2, heads, MOBA_BLOCK, MOBA_BLOCK), F32)],
        name="moba",
        compiler_params=_params(("parallel", "parallel", "arbitrary"), 40),
    )(qk, qk, v, kmean)


def _merge_kernel(gh_ref, o_ref, wr_ref, wa_ref, gr_ref, ga_ref, out_ref):
    rnn = jnp.dot(gh_ref[...], wr_ref[...], preferred_element_type=F32)
    att = jnp.dot(o_ref[...], wa_ref[...], preferred_element_type=F32)
    out_ref[...] = (gr_ref[...].astype(F32) * rnn + ga_ref[...].astype(F32) * att).astype(out_ref.dtype)


def _merge(gh, o, w_rnn, w_attn, gates, tm=1024, tn=512):
    m, k = gh.shape
    nj = D_MODEL // tn
    a_spec = pl.BlockSpec((tm, k), lambda i, j: (i, 0))
    w_spec = pl.BlockSpec((k, tn), lambda i, j: (0, j))
    return pl.pallas_call(
        _merge_kernel,
        grid=(m // tm, nj),
        in_specs=[a_spec, a_spec, w_spec, w_spec,
                  pl.BlockSpec((tm, tn), lambda i, j: (i, j)),
                  pl.BlockSpec((tm, tn), lambda i, j: (i, nj + j))],
        out_specs=pl.BlockSpec((tm, tn), lambda i, j: (i, j)),
        out_shape=jax.ShapeDtypeStruct((m, D_MODEL), BF16),
        name="merge",
        compiler_params=_params(("parallel", "arbitrary"), 48),
    )(gh, o, w_rnn, w_attn, gates, gates)


def _memkv_kernel(mem_ref, g_ref, w_ref, o_ref):
    hn = _rms(mem_ref[...], g_ref[...]).astype(BF16)
    o_ref[...] = jnp.dot(hn, w_ref[...], preferred_element_type=F32).astype(o_ref.dtype)


def _memkv(mem2d, g, w_kv, mem_len):
    m, d = mem2d.shape
    return pl.pallas_call(
        _memkv_kernel,
        grid=(m // mem_len,),
        in_specs=[pl.BlockSpec((mem_len, d), lambda i: (i, 0)),
                  pl.BlockSpec((1, d), lambda i: (0, 0)),
                  pl.BlockSpec((d, 2 * D_MEM), lambda i: (0, 0))],
        out_specs=pl.BlockSpec((mem_len, 2 * D_MEM), lambda i: (i, 0)),
        out_shape=jax.ShapeDtypeStruct((m, 2 * D_MEM), BF16),
        name="mem_kv",
        compiler_params=_params(("parallel",), 32),
    )(mem2d, g.reshape(1, d), w_kv)


def _xattn_kernel(x_ref, g_ref, wq_ref, kv_ref, wo_ref, o_ref):
    x = x_ref[...]
    hn = _rms(x, g_ref[...]).astype(BF16)
    q = jnp.dot(hn, wq_ref[...], preferred_element_type=F32).astype(BF16)
    scale = MEM_HEAD_DIM ** -0.5
    heads = []
    for hd in range(MEM_HEADS):
        sl = slice(hd * MEM_HEAD_DIM, (hd + 1) * MEM_HEAD_DIM)
        kh = kv_ref[:, sl]
        vh = kv_ref[:, D_MEM + hd * MEM_HEAD_DIM:D_MEM + (hd + 1) * MEM_HEAD_DIM]
        s = lax.dot_general(q[:, sl], kh, _NT, preferred_element_type=F32) * scale
        m = jnp.max(s, axis=-1, keepdims=True)
        p = jnp.exp(s - m)
        l = jnp.sum(p, axis=-1, keepdims=True)
        oh = jnp.dot(p.astype(BF16), vh, preferred_element_type=F32) / l
        heads.append(oh.astype(BF16))
    o_all = jnp.concatenate(heads, axis=-1)
    o_ref[...] = x + jnp.dot(o_all, wo_ref[...], preferred_element_type=F32)


def _xattn(x, g, w_q, kv, w_o, seq, mem_len, tm=512):
    m, d = x.shape
    per_batch = seq // tm
    return pl.pallas_call(
        _xattn_kernel,
        grid=(m // tm,),
        in_specs=[pl.BlockSpec((tm, d), lambda i: (i, 0)),
                  pl.BlockSpec((1, d), lambda i: (0, 0)),
                  pl.BlockSpec((d, D_MEM), lambda i: (0, 0)),
                  pl.BlockSpec((mem_len, 2 * D_MEM), lambda i: (i // per_batch, 0)),
                  pl.BlockSpec((D_MEM, d), lambda i: (0, 0))],
        out_specs=pl.BlockSpec((tm, d), lambda i: (i, 0)),
        out_shape=jax.ShapeDtypeStruct((m, d), F32),
        name="xattn",
        compiler_params=_params(("parallel",), 48),
    )(x, g.reshape(1, d), w_q, kv, w_o)


def _ffn_kernel(x_ref, g_ref, wg_ref, wu_ref, wd_ref, gf_ref, o_ref, h_scr, acc_scr):
    f = pl.program_id(1)

    @pl.when(f == 0)
    def _():
        h_scr[...] = _rms(x_ref[...], g_ref[...]).astype(BF16)
        acc_scr[...] = jnp.zeros_like(acc_scr)

    hn = h_scr[...]
    a = jnp.dot(hn, wg_ref[...], preferred_element_type=F32)
    b = jnp.dot(hn, wu_ref[...], preferred_element_type=F32)
    act = (jax.nn.silu(a) * b).astype(BF16)
    acc_scr[...] += jnp.dot(act, wd_ref[...], preferred_element_type=F32)

    @pl.when(f == pl.num_programs(1) - 1)
    def _():
        o_ref[...] = _rms(x_ref[...] + acc_scr[...], gf_ref[...])


def _ffn(x, g, w_gate, w_up, w_down, g_final, tm=512, tf=512):
    m, d = x.shape
    d_ff = w_gate.shape[1]
    return pl.pallas_call(
        _ffn_kernel,
        grid=(m // tm, d_ff // tf),
        in_specs=[pl.BlockSpec((tm, d), lambda i, f: (i, 0)),
                  pl.BlockSpec((1, d), lambda i, f: (0, 0)),
                  pl.BlockSpec((d, tf), lambda i, f: (0, f)),
                  pl.BlockSpec((d, tf), lambda i, f: (0, f)),
                  pl.BlockSpec((tf, d), lambda i, f: (f, 0)),
                  pl.BlockSpec((1, d), lambda i, f: (0, 0))],
        out_specs=pl.BlockSpec((tm, d), lambda i, f: (i, 0)),
        out_shape=jax.ShapeDtypeStruct((m, d), F32),
        scratch_shapes=[pltpu.VMEM((tm, d), BF16), pltpu.VMEM((tm, d), F32)],
        name="ffn",
        compiler_params=_params(("parallel", "arbitrary"), 48),
    )(x, g.reshape(1, d), w_gate, w_up, w_down, g_final.reshape(1, d))


def _layer(x2d, mem2d, cosf, sinf, batch, seq, mem_len, p):
    n = x2d.shape[0]
    w_in = p["w_in"].astype(BF16)
    hn = _norm_bf16(x2d, p["norm_mix_g"])

    c0 = 0
    xy = _matmul(_mm_plain_kernel, "proj_xy", hn, w_in, col_off=c0, n_cols=2 * D_RNN, out_dtype=F32)
    c0 += 2 * D_RNN
    tm, tn = 1024, 512
    rope_spec = pl.BlockSpec((tm, LANES), lambda i, j: (i, 0))
    qk, means = _matmul(
        _mm_rope_kernel, "proj_qk", hn, w_in, col_off=c0, n_cols=2 * D_ATTN, out_dtype=BF16,
        tm=tm, tn=tn, extra=(cosf, sinf), extra_specs=(rope_spec, rope_spec),
        extra_out_shape=(jax.ShapeDtypeStruct((n // tm, tm // MOBA_BLOCK, 2 * D_ATTN), F32),),
        extra_out_specs=(pl.BlockSpec((1, tm // MOBA_BLOCK, tn), lambda i, j: (i, 0, j)),))
    c0 += 2 * D_ATTN
    v = _matmul(_mm_plain_kernel, "proj_v", hn, w_in, col_off=c0, n_cols=D_ATTN, out_dtype=BF16)
    c0 += D_ATTN
    gates = _matmul(_mm_sigmoid_kernel, "proj_gates", hn, w_in, col_off=c0, n_cols=2 * D_MODEL,
                    out_dtype=BF16)

    gh = _rglru(xy, p["conv_w"], p["conv_b"], p["lru_w_a"], p["lru_b_a"], p["lru_w_i"],
                p["lru_b_i"], p["lru_lambda"], batch, seq)
    kmean = means.reshape(n // MOBA_BLOCK, 2 * D_ATTN)
    o = _moba(qk, v, kmean, batch, seq)

    merged = _merge(gh, o, p["w_rnn_proj"].astype(BF16), p["w_attn_proj"].astype(BF16), gates)
    x1 = _matmul(_mm_residual_kernel, "mix_out", merged, p["w_mix_out"].astype(BF16), col_off=0,
                 n_cols=D_MODEL, out_dtype=F32, tm=tm, tn=tn,
                 extra=(x2d,), extra_specs=(pl.BlockSpec((tm, tn), lambda i, j: (i, j)),))

    kv = _memkv(mem2d, p["norm_mem_g"], p["w_xkv"].astype(BF16), mem_len)
    x2 = _xattn(x1, p["norm_xq_g"], p["w_xq"].astype(BF16), kv, p["w_xo"].astype(BF16), seq, mem_len)
    return x2


def kernel(x, mem, positions, norm_mix_g, w_in, conv_w, conv_b, lru_w_a, lru_b_a, lru_w_i, lru_b_i,
           lru_lambda, w_rnn_proj, w_attn_proj, w_mix_out, norm_xq_g, norm_mem_g, w_xq, w_xkv, w_xo,
           norm_ffn_g, w_ffn_gate, w_ffn_up, w_ffn_down, norm_final_g):
    batch, seq, d = x.shape
    mem_len = mem.shape[1]
    assert w_in.shape[0] == 1, "only DEPTH == 1 is supported"
    x2d = x.reshape(batch * seq, d)
    mem2d = mem.reshape(batch * mem_len, d)
    cosf, sinf = _rope_tables(positions)
    p = dict(norm_mix_g=norm_mix_g[0], w_in=w_in[0], conv_w=conv_w[0], conv_b=conv_b[0],
             lru_w_a=lru_w_a[0], lru_b_a=lru_b_a[0], lru_w_i=lru_w_i[0], lru_b_i=lru_b_i[0],
             lru_lambda=lru_lambda[0], w_rnn_proj=w_rnn_proj[0], w_attn_proj=w_attn_proj[0],
             w_mix_out=w_mix_out[0], norm_xq_g=norm_xq_g[0], norm_mem_g=norm_mem_g[0],
             w_xq=w_xq[0], w_xkv=w_xkv[0], w_xo=w_xo[0])
    x2 = _layer(x2d, mem2d, cosf, sinf, batch, seq, mem_len, p)
    out = _ffn(x2, norm_ffn_g[0], w_ffn_gate[0].astype(BF16), w_ffn_up[0].astype(BF16),
               w_ffn_down[0].astype(BF16), norm_final_g)
    return out.reshape(batch, seq, d)
```

```python
import functools

import jax
import jax.numpy as jnp
from jax import lax
from jax.experimental import pallas as pl
from jax.experimental.pallas import tpu as pltpu

D_MODEL = 2048
N_HEADS = 16
HEAD_DIM = 128
D_ATTN = N_HEADS * HEAD_DIM
MOBA_BLOCK = 256
MOBA_TOPK = 3
ROPE_THETA = 500000.0
ROT_DIM = HEAD_DIM // 4
ROT_HALF = ROT_DIM // 2
D_RNN = 2048
N_RNN_BLOCKS = 16
RNN_BLOCK = D_RNN // N_RNN_BLOCKS
CONV_WIDTH = 4
LRU_C = 8.0
MEM_HEADS = 4
MEM_HEAD_DIM = 128
D_MEM = MEM_HEADS * MEM_HEAD_DIM
RMS_EPS = 1e-6
NEG_INF = -1e30
LOG2_E = 1.4426950408889634

LANES = 128
SUBLANES = 8
BF16_ROWS = 16
MIB = 1024 * 1024

BF16 = jnp.bfloat16
F32 = jnp.float32

_NT = (((1,), (1,)), ((), ()))
_TN = (((0,), (0,)), ((), ()))


def _params(semantics, vmem_mib):
    return pltpu.CompilerParams(dimension_semantics=semantics,
                                vmem_limit_bytes=vmem_mib * MIB)


def _rms(x, g):
    ms = jnp.mean(x * x, axis=-1, keepdims=True)
    return x * lax.rsqrt(ms + RMS_EPS) * g


def _norm_kernel(x_ref, g_ref, o_ref):
    o_ref[...] = _rms(x_ref[...], g_ref[...]).astype(o_ref.dtype)


def _norm_bf16(x, g, tm=512):
    m, d = x.shape
    return pl.pallas_call(
        _norm_kernel,
        grid=(m // tm,),
        in_specs=[pl.BlockSpec((tm, d), lambda i: (i, 0)),
                  pl.BlockSpec((1, d), lambda i: (0, 0))],
        out_specs=pl.BlockSpec((tm, d), lambda i: (i, 0)),
        out_shape=jax.ShapeDtypeStruct((m, d), BF16),
        name="norm_mix",
        compiler_params=_params(("parallel",), 32),
    )(x, g.reshape(1, d))


def _rope_table_kernel(pos_ref, invf_ref, cos_ref, sin_ref):
    ang = pos_ref[...] * invf_ref[...]
    lane = lax.broadcasted_iota(jnp.int32, ang.shape, 1)
    c = jnp.cos(ang)
    s = jnp.sin(ang)
    cos_ref[...] = jnp.where(lane < ROT_DIM, c, 1.0)
    sin_ref[...] = jnp.where(lane < ROT_HALF, -s, jnp.where(lane < ROT_DIM, s, 0.0))


def _rope_tables(positions, tr=1024):
    n = positions.size
    pos = positions.astype(F32).reshape(n, 1)
    inv_freq = jnp.power(ROPE_THETA, -jnp.arange(ROT_HALF, dtype=F32) / ROT_HALF)
    invf = jnp.concatenate([inv_freq, inv_freq, jnp.zeros((LANES - ROT_DIM,), F32)]).reshape(1, LANES)
    return pl.pallas_call(
        _rope_table_kernel,
        grid=(n // tr,),
        in_specs=[pl.BlockSpec((tr, 1), lambda i: (i, 0)),
                  pl.BlockSpec((1, LANES), lambda i: (0, 0))],
        out_specs=[pl.BlockSpec((tr, LANES), lambda i: (i, 0)),
                   pl.BlockSpec((tr, LANES), lambda i: (i, 0))],
        out_shape=[jax.ShapeDtypeStruct((n, LANES), F32),
                   jax.ShapeDtypeStruct((n, LANES), F32)],
        name="rope_tables",
        compiler_params=_params(("parallel",), 32),
    )(pos, invf)


def _mm_plain_kernel(a_ref, w_ref, o_ref):
    acc = jnp.dot(a_ref[...], w_ref[...], preferred_element_type=F32)
    o_ref[...] = acc.astype(o_ref.dtype)


def _mm_transposed_kernel(a_ref, w_ref, o_ref):
    acc = jnp.dot(a_ref[...], w_ref[...], preferred_element_type=F32)
    o_ref[...] = acc.T.astype(o_ref.dtype)


def _mm_sigmoid_kernel(a_ref, w_ref, o_ref):
    acc = jnp.dot(a_ref[...], w_ref[...], preferred_element_type=F32)
    o_ref[...] = jax.nn.sigmoid(acc).astype(o_ref.dtype)


def _mm_residual_kernel(a_ref, w_ref, r_ref, o_ref):
    acc = jnp.dot(a_ref[...], w_ref[...], preferred_element_type=F32)
    o_ref[...] = r_ref[...] + acc


def _mm_rope_kernel(a_ref, w_ref, cos_ref, sin_ref, o_ref, mean_ref):
    acc = jnp.dot(a_ref[...], w_ref[...], preferred_element_type=F32)
    tm, tn = acc.shape
    cosf = cos_ref[...]
    sinf = sin_ref[...]
    lane = lax.broadcasted_iota(jnp.int32, (tm, HEAD_DIM), 1)
    first_half = lane < ROT_HALF
    for hd in range(tn // HEAD_DIM):
        a = acc[:, hd * HEAD_DIM:(hd + 1) * HEAD_DIM]
        partner = jnp.where(first_half,
                            pltpu.roll(a, HEAD_DIM - ROT_HALF, 1),
                            pltpu.roll(a, ROT_HALF, 1))
        r = a * cosf + partner * sinf
        o_ref[:, hd * HEAD_DIM:(hd + 1) * HEAD_DIM] = r.astype(o_ref.dtype)
        mean_ref[0, :, hd * HEAD_DIM:(hd + 1) * HEAD_DIM] = jnp.mean(
            r.reshape(tm // MOBA_BLOCK, MOBA_BLOCK, HEAD_DIM), axis=1)


def _matmul(kernel, name, a, w, *, col_off, n_cols, out_dtype, tm=1024, tn=512,
            extra=(), extra_specs=(), extra_out_shape=(), extra_out_specs=(), vmem_mib=48,
            transposed_out=False):
    m, k = a.shape
    off = col_off // tn
    if transposed_out:
        main_shape, main_spec = (n_cols, m), pl.BlockSpec((tn, tm), lambda i, j: (j, i))
    else:
        main_shape, main_spec = (m, n_cols), pl.BlockSpec((tm, tn), lambda i, j: (i, j))
    out_shape = [jax.ShapeDtypeStruct(main_shape, out_dtype)] + list(extra_out_shape)
    out_specs = [main_spec] + list(extra_out_specs)
    res = pl.pallas_call(
        kernel,
        grid=(m // tm, n_cols // tn),
        in_specs=[pl.BlockSpec((tm, k), lambda i, j: (i, 0)),
                  pl.BlockSpec((k, tn), lambda i, j: (0, j + off))] + list(extra_specs),
        out_specs=out_specs,
        out_shape=out_shape,
        name=name,
        compiler_params=_params(("parallel", "arbitrary"), vmem_mib),
    )(a, w, *extra)
    return res if extra_out_shape else res[0]


def _rglru_kernel(xr_ref, yr_ref, cw_ref, cb_ref, wa_ref, wi_ref, ba_ref, bi_ref, lam_ref,
                  o_ref, xbuf, a_scr, u_scr, h_scr, nat_scr, tail_scr, hc_scr):
    ts, tc = xr_ref.shape
    nv = ts // SUBLANES
    halo = (CONV_WIDTH - 1) * SUBLANES
    t = pl.program_id(2)

    def grp(g):
        return slice(halo + g * SUBLANES, halo + (g + 1) * SUBLANES)

    @pl.when(t == 0)
    def _():
        tail_scr[...] = jnp.zeros_like(tail_scr)
        hc_scr[...] = jnp.zeros_like(hc_scr)

    lane_blocks = [slice(cb * LANES, (cb + 1) * LANES) for cb in range(tc // LANES)]
    for cb, ls in enumerate(lane_blocks):
        nat_scr[cb] = xr_ref[:, ls]
    for v in range(nv):
        for cb, ls in enumerate(lane_blocks):
            xbuf[grp(v), ls] = nat_scr[cb, pl.ds(v, SUBLANES, stride=nv), :]
    sub = lax.broadcasted_iota(jnp.int32, (SUBLANES, tc), 0)
    for d in range(1, CONV_WIDTH):
        keep = slice((CONV_WIDTH - 1 - d) * SUBLANES, (CONV_WIDTH - d) * SUBLANES)
        cur = xbuf[grp(nv - d), :]
        xbuf[grp(-d), :] = pltpu.roll(jnp.where(sub == SUBLANES - 1, tail_scr[keep, :], cur), 1, 0)
        tail_scr[keep, :] = cur

    xc = jnp.zeros((ts, tc), F32) + cb_ref[...]
    for kk in range(CONV_WIDTH):
        start = halo - (CONV_WIDTH - 1 - kk) * SUBLANES
        xc = xc + cw_ref[kk:kk + 1, :] * xbuf[start:start + ts, :]

    half_a = (-0.5 * LRU_C) * jax.nn.softplus(-lam_ref[...])
    for nb in range(tc // RNN_BLOCK):
        sl = slice(nb * RNN_BLOCK, (nb + 1) * RNN_BLOCK)
        xb = xc[:, sl]
        xb16 = xb.astype(BF16)
        zr = jnp.dot(xb16, wa_ref[nb], preferred_element_type=F32) + ba_ref[:, sl]
        zi = jnp.dot(xb16, wi_ref[nb], preferred_element_type=F32) + bi_ref[:, sl]
        log_a = half_a[:, sl] * jnp.tanh(0.5 * zr) + half_a[:, sl]
        ig = 0.5 * jnp.tanh(0.5 * zi) + 0.5
        a_scr[:, sl] = jnp.exp(log_a)
        th = jnp.tanh(log_a)
        y = -2.0 * th / (1.0 - th)
        u_scr[:, sl] = jnp.where(y > 0.0, y * lax.rsqrt(y), 0.0) * (ig * xb)

    h_end = jnp.zeros((SUBLANES, tc), F32)
    p_end = jnp.ones((SUBLANES, tc), F32)
    for v in range(nv):
        rows = slice(v * SUBLANES, (v + 1) * SUBLANES)
        a = a_scr[rows, :]
        h_end = a * h_end + u_scr[rows, :]
        p_end = a * p_end
        h_scr[rows, :] = h_end
        a_scr[rows, :] = p_end
    h_in = hc_scr[...]
    entering = []
    for s in range(SUBLANES):
        entering.append(h_in)
        h_in = h_end[s:s + 1, :] + p_end[s:s + 1, :] * h_in
    hc_scr[...] = h_in
    h_enter = jnp.concatenate(entering, axis=0)

    for v in range(nv):
        rows = slice(v * SUBLANES, (v + 1) * SUBLANES)
        h_v = h_scr[rows, :] + a_scr[rows, :] * h_enter
        for cb, ls in enumerate(lane_blocks):
            nat_scr[cb, pl.ds(v, SUBLANES, stride=nv), :] = h_v[:, ls]
    for cb, ls in enumerate(lane_blocks):
        o_ref[:, ls] = (jax.nn.gelu(yr_ref[:, ls]) * nat_scr[cb]).astype(o_ref.dtype)


def _rglru(xy, conv_w, conv_b, w_a, b_a, w_i, b_i, lam, batch, seq, ts=512, tc=512):
    n = batch * seq
    nt = seq // ts
    ncb = D_RNN // tc
    halo = (CONV_WIDTH - 1) * SUBLANES
    row = lambda v: v.reshape(1, D_RNN)
    vec_spec = pl.BlockSpec((1, tc), lambda b, c, t: (0, c))
    gate_spec = pl.BlockSpec((tc // RNN_BLOCK, RNN_BLOCK, RNN_BLOCK), lambda b, c, t: (c, 0, 0))
    return pl.pallas_call(
        _rglru_kernel,
        grid=(batch, ncb, nt),
        in_specs=[pl.BlockSpec((ts, tc), lambda b, c, t: (b * nt + t, c)),
                  pl.BlockSpec((ts, tc), lambda b, c, t: (b * nt + t, ncb + c)),
                  pl.BlockSpec((CONV_WIDTH, tc), lambda b, c, t: (0, c)),
                  vec_spec, gate_spec, gate_spec, vec_spec, vec_spec, vec_spec],
        out_specs=pl.BlockSpec((ts, tc), lambda b, c, t: (b * nt + t, c)),
        out_shape=jax.ShapeDtypeStruct((n, D_RNN), BF16),
        scratch_shapes=[pltpu.VMEM((halo + ts, tc), F32),
                        pltpu.VMEM((ts, tc), F32),
                        pltpu.VMEM((ts, tc), F32),
                        pltpu.VMEM((ts, tc), F32),
                        pltpu.VMEM((tc // LANES, ts, LANES), F32),
                        pltpu.VMEM((halo, tc), F32),
                        pltpu.VMEM((1, tc), F32)],
        name="rglru",
        compiler_params=_params(("parallel", "parallel", "arbitrary"), 32),
    )(xy, xy, conv_w, row(conv_b), w_a.astype(BF16), w_i.astype(BF16), row(b_a), row(b_i), row(lam))


def _moba_kernel(q_ref, k_ref, v_ref, km_ref, o_ref, bias_scr, s_scr, *, heads):
    j = pl.program_id(2)
    nblk = km_ref.shape[0]
    blk = MOBA_BLOCK
    c = (HEAD_DIM ** -0.5) * LOG2_E
    head_slices = [slice(h * HEAD_DIM, (h + 1) * HEAD_DIM) for h in range(heads)]

    def raw_scores(block):
        start = pl.multiple_of(block * blk, blk)
        return [lax.dot_general(k_ref[pl.ds(start, blk), hs], q_ref[:, hs], _NT,
                                preferred_element_type=F32) for hs in head_slices]

    def score_into(slot, block, raw=None):
        raw = raw_scores(block) if raw is None else raw
        for h in range(heads):
            s_scr[slot, h] = raw[h] + bias_scr[h, pl.ds(block, 1), :]

    ones_rows = jnp.ones((BF16_ROWS, blk), BF16)

    def attend(slot, mask, block, state):
        start = pl.multiple_of(block * blk, blk)
        half = blk // 2
        soft = []
        for h in range(heads):
            m = state[h][0]
            s_lo = mask(s_scr[slot, h, 0:half, :], 0)
            s_hi = mask(s_scr[slot, h, half:blk, :], half)
            m_blk = jnp.maximum(jnp.max(s_lo, axis=0, keepdims=True), jnp.max(s_hi, axis=0, keepdims=True))
            m_new = jnp.maximum(m, m_blk)
            alpha = jnp.exp2((m - m_new) * c)
            p = jnp.exp2((mask(s_scr[slot, h], 0) - m_new) * c)
            soft.append((m_new, alpha, p.astype(BF16)))
        out = []
        for h, hs in enumerate(head_slices):
            m_new, alpha, p = soft[h]
            vt = jnp.concatenate([v_ref[hs, pl.ds(start, blk)], ones_rows], axis=0)
            pv = jnp.dot(vt, p, preferred_element_type=F32)
            out.append((m_new, alpha * state[h][1] + pv))
        return tuple(out)

    no_mask = lambda s, row0: s

    raw0 = raw_scores(0)
    for h, hs in enumerate(head_slices):
        q = q_ref[:, hs]
        km = km_ref[:, hs]
        km_hi = km.astype(BF16)
        km_lo = (km - km_hi.astype(F32)).astype(BF16)
        gate = (lax.dot_general(km_hi, q, _NT, preferred_element_type=F32)
                + lax.dot_general(km_lo, q, _NT, preferred_element_type=F32))
        bidx = lax.broadcasted_iota(jnp.int32, gate.shape, 0)
        past = bidx < j
        g = jnp.where(past, gate, -jnp.inf)
        rank = jnp.zeros(gate.shape, jnp.int32)
        for other in range(nblk):
            go = g[other:other + 1, :]
            beats = jnp.where(go > g, 1, jnp.where(go == g, jnp.where(bidx > other, 1, 0), 0))
            rank = rank + beats
        bias_scr[h] = jnp.where(past, jnp.where(rank < MOBA_TOPK, 0.0, NEG_INF),
                                jnp.where(bidx == j, 0.0, NEG_INF))
    score_into(0, 0, raw0)

    def pair_body(t, state):
        first = 2 * t
        score_into(1, first + 1)
        state = attend(0, no_mask, first, state)
        score_into(0, first + 2)
        return attend(1, no_mask, first + 1, state)

    def odd_body(state):
        score_into(1, j)
        return attend(0, no_mask, j - 1, state)

    init = tuple((jnp.full((1, blk), NEG_INF, F32), jnp.zeros((HEAD_DIM + BF16_ROWS, blk), F32))
                 for _ in range(heads))
    state = lax.fori_loop(0, lax.shift_right_logical(j, 1), pair_body, init)
    odd = lax.bitwise_and(j, 1)
    state = lax.cond(odd == 1, odd_body, lambda st: st, state)

    def causal_mask(s, row0):
        kpos = row0 + lax.broadcasted_iota(jnp.int32, s.shape, 0)
        qpos = lax.broadcasted_iota(jnp.int32, s.shape, 1)
        return jnp.where(kpos <= qpos, s, NEG_INF)

    state = attend(odd, causal_mask, j, state)
    for h, hs in enumerate(head_slices):
        acc = state[h][1]
        o_ref[:, hs] = (acc[0:HEAD_DIM] / acc[HEAD_DIM:HEAD_DIM + 1]).T.astype(o_ref.dtype)


def _moba(qk, v_t, kmean, batch, seq, heads=4):
    n = batch * seq
    nblk = seq // MOBA_BLOCK
    groups = N_HEADS // heads
    width = heads * HEAD_DIM
    return pl.pallas_call(
        functools.partial(_moba_kernel, heads=heads),
        grid=(batch, groups, nblk),
        in_specs=[pl.BlockSpec((MOBA_BLOCK, width), lambda b, h, j: (b * nblk + j, h)),
                  pl.BlockSpec((seq, width), lambda b, h, j: (b, groups + h)),
                  pl.BlockSpec((width, seq), lambda b, h, j: (h, b)),
                  pl.BlockSpec((nblk, width), lambda b, h, j: (b, groups + h))],
        out_specs=pl.BlockSpec((MOBA_BLOCK, width), lambda b, h, j: (b * nblk + j, h)),
        out_shape=jax.ShapeDtypeStruct((n, D_ATTN), BF16),
        scratch_shapes=[pltpu.VMEM((heads, nblk, MOBA_BLOCK), F32),
                        pltpu.VMEM((2, heads, MOBA_BLOCK, MOBA_BLOCK), F32)],
        name="moba",
        compiler_params=_params(("parallel", "parallel", "arbitrary"), 56),
    )(qk, qk, v_t, kmean)


def _merge_kernel(gh_ref, o_ref, wr_ref, wa_ref, gr_ref, ga_ref, out_ref):
    rnn = jnp.dot(gh_ref[...], wr_ref[...], preferred_element_type=F32)
    att = jnp.dot(o_ref[...], wa_ref[...], preferred_element_type=F32)
    out_ref[...] = (gr_ref[...].astype(F32) * rnn + ga_ref[...].astype(F32) * att).astype(out_ref.dtype)


def _merge(gh, o, w_rnn, w_attn, gates, tm=1024, tn=512):
    m, k = gh.shape
    nj = D_MODEL // tn
    a_spec = pl.BlockSpec((tm, k), lambda i, j: (i, 0))
    w_spec = pl.BlockSpec((k, tn), lambda i, j: (0, j))
    return pl.pallas_call(
        _merge_kernel,
        grid=(m // tm, nj),
        in_specs=[a_spec, a_spec, w_spec, w_spec,
                  pl.BlockSpec((tm, tn), lambda i, j: (i, j)),
                  pl.BlockSpec((tm, tn), lambda i, j: (i, nj + j))],
        out_specs=pl.BlockSpec((tm, tn), lambda i, j: (i, j)),
        out_shape=jax.ShapeDtypeStruct((m, D_MODEL), BF16),
        name="merge",
        compiler_params=_params(("parallel", "arbitrary"), 48),
    )(gh, o, w_rnn, w_attn, gates, gates)


def _memkv_kernel(mem_ref, g_ref, w_ref, o_ref):
    hn = _rms(mem_ref[...], g_ref[...]).astype(BF16)
    o_ref[...] = jnp.dot(hn, w_ref[...], preferred_element_type=F32).astype(o_ref.dtype)


def _memkv(mem2d, g, w_kv, mem_len):
    m, d = mem2d.shape
    return pl.pallas_call(
        _memkv_kernel,
        grid=(m // mem_len,),
        in_specs=[pl.BlockSpec((mem_len, d), lambda i: (i, 0)),
                  pl.BlockSpec((1, d), lambda i: (0, 0)),
                  pl.BlockSpec((d, 2 * D_MEM), lambda i: (0, 0))],
        out_specs=pl.BlockSpec((mem_len, 2 * D_MEM), lambda i: (i, 0)),
        out_shape=jax.ShapeDtypeStruct((m, 2 * D_MEM), BF16),
        name="mem_kv",
        compiler_params=_params(("parallel",), 32),
    )(mem2d, g.reshape(1, d), w_kv)


def _xattn_kernel(x_ref, g_ref, wq_ref, kv_ref, wo_ref, o_ref):
    x = x_ref[...]
    hn = _rms(x, g_ref[...]).astype(BF16)
    q = jnp.dot(hn, wq_ref[...], preferred_element_type=F32).astype(BF16)
    scale = MEM_HEAD_DIM ** -0.5
    heads = []
    for hd in range(MEM_HEADS):
        sl = slice(hd * MEM_HEAD_DIM, (hd + 1) * MEM_HEAD_DIM)
        kh = kv_ref[:, sl]
        vh = kv_ref[:, D_MEM + hd * MEM_HEAD_DIM:D_MEM + (hd + 1) * MEM_HEAD_DIM]
        s = lax.dot_general(q[:, sl], kh, _NT, preferred_element_type=F32) * scale
        m = jnp.max(s, axis=-1, keepdims=True)
        p = jnp.exp(s - m)
        l = jnp.sum(p, axis=-1, keepdims=True)
        oh = jnp.dot(p.astype(BF16), vh, preferred_element_type=F32) / l
        heads.append(oh.astype(BF16))
    o_all = jnp.concatenate(heads, axis=-1)
    o_ref[...] = x + jnp.dot(o_all, wo_ref[...], preferred_element_type=F32)


def _xattn(x, g, w_q, kv, w_o, seq, mem_len, tm=512):
    m, d = x.shape
    per_batch = seq // tm
    return pl.pallas_call(
        _xattn_kernel,
        grid=(m // tm,),
        in_specs=[pl.BlockSpec((tm, d), lambda i: (i, 0)),
                  pl.BlockSpec((1, d), lambda i: (0, 0)),
                  pl.BlockSpec((d, D_MEM), lambda i: (0, 0)),
                  pl.BlockSpec((mem_len, 2 * D_MEM), lambda i: (i // per_batch, 0)),
                  pl.BlockSpec((D_MEM, d), lambda i: (0, 0))],
        out_specs=pl.BlockSpec((tm, d), lambda i: (i, 0)),
        out_shape=jax.ShapeDtypeStruct((m, d), F32),
        name="xattn",
        compiler_params=_params(("parallel",), 48),
    )(x, g.reshape(1, d), w_q, kv, w_o)


def _ffn_kernel(x_ref, g_ref, wg_ref, wu_ref, wd_ref, gf_ref, o_ref, h_scr, acc_scr):
    f = pl.program_id(1)

    @pl.when(f == 0)
    def _():
        h_scr[...] = _rms(x_ref[...], g_ref[...]).astype(BF16)
        acc_scr[...] = jnp.zeros_like(acc_scr)

    hn = h_scr[...]
    a = jnp.dot(hn, wg_ref[...], preferred_element_type=F32)
    b = jnp.dot(hn, wu_ref[...], preferred_element_type=F32)
    act = (jax.nn.silu(a) * b).astype(BF16)
    acc_scr[...] += jnp.dot(act, wd_ref[...], preferred_element_type=F32)

    @pl.when(f == pl.num_programs(1) - 1)
    def _():
        o_ref[...] = _rms(x_ref[...] + acc_scr[...], gf_ref[...])


def _ffn(x, g, w_gate, w_up, w_down, g_final, tm=512, tf=512):
    m, d = x.shape
    d_ff = w_gate.shape[1]
    return pl.pallas_call(
        _ffn_kernel,
        grid=(m // tm, d_ff // tf),
        in_specs=[pl.BlockSpec((tm, d), lambda i, f: (i, 0)),
                  pl.BlockSpec((1, d), lambda i, f: (0, 0)),
                  pl.BlockSpec((d, tf), lambda i, f: (0, f)),
                  pl.BlockSpec((d, tf), lambda i, f: (0, f)),
                  pl.BlockSpec((tf, d), lambda i, f: (f, 0)),
                  pl.BlockSpec((1, d), lambda i, f: (0, 0))],
        out_specs=pl.BlockSpec((tm, d), lambda i, f: (i, 0)),
        out_shape=jax.ShapeDtypeStruct((m, d), F32),
        scratch_shapes=[pltpu.VMEM((tm, d), BF16), pltpu.VMEM((tm, d), F32)],
        name="ffn",
        compiler_params=_params(("parallel", "arbitrary"), 48),
    )(x, g.reshape(1, d), w_gate, w_up, w_down, g_final.reshape(1, d))


def _layer(x2d, mem2d, cosf, sinf, batch, seq, mem_len, p):
    n = x2d.shape[0]
    w_in = p["w_in"].astype(BF16)
    hn = _norm_bf16(x2d, p["norm_mix_g"])

    c0 = 0
    wide = 1024
    xy = _matmul(_mm_plain_kernel, "proj_xy", hn, w_in, col_off=c0, n_cols=2 * D_RNN, out_dtype=F32,
                 tn=wide)
    c0 += 2 * D_RNN
    tm, tn = 1024, 512
    rope_spec = pl.BlockSpec((tm, LANES), lambda i, j: (i, 0))
    qk, means = _matmul(
        _mm_rope_kernel, "proj_qk", hn, w_in, col_off=c0, n_cols=2 * D_ATTN, out_dtype=BF16,
        tm=tm, tn=tn, extra=(cosf, sinf), extra_specs=(rope_spec, rope_spec),
        extra_out_shape=(jax.ShapeDtypeStruct((n // tm, tm // MOBA_BLOCK, 2 * D_ATTN), F32),),
        extra_out_specs=(pl.BlockSpec((1, tm // MOBA_BLOCK, tn), lambda i, j: (i, 0, j)),))
    c0 += 2 * D_ATTN
    v_t = _matmul(_mm_transposed_kernel, "proj_v", hn, w_in, col_off=c0, n_cols=D_ATTN,
                  out_dtype=BF16, tn=wide, transposed_out=True)
    c0 += D_ATTN
    gates = _matmul(_mm_sigmoid_kernel, "proj_gates", hn, w_in, col_off=c0, n_cols=2 * D_MODEL,
                    out_dtype=BF16, tn=wide)

    gh = _rglru(xy, p["conv_w"], p["conv_b"], p["lru_w_a"], p["lru_b_a"], p["lru_w_i"],
                p["lru_b_i"], p["lru_lambda"], batch, seq)
    kmean = means.reshape(n // MOBA_BLOCK, 2 * D_ATTN)
    o = _moba(qk, v_t, kmean, batch, seq)

    merged = _merge(gh, o, p["w_rnn_proj"].astype(BF16), p["w_attn_proj"].astype(BF16), gates)
    x1 = _matmul(_mm_residual_kernel, "mix_out", merged, p["w_mix_out"].astype(BF16), col_off=0,
                 n_cols=D_MODEL, out_dtype=F32, tm=tm, tn=wide,
                 extra=(x2d,), extra_specs=(pl.BlockSpec((tm, wide), lambda i, j: (i, j)),))

    kv = _memkv(mem2d, p["norm_mem_g"], p["w_xkv"].astype(BF16), mem_len)
    x2 = _xattn(x1, p["norm_xq_g"], p["w_xq"].astype(BF16), kv, p["w_xo"].astype(BF16), seq, mem_len)
    return x2


def kernel(x, mem, positions, norm_mix_g, w_in, conv_w, conv_b, lru_w_a, lru_b_a, lru_w_i, lru_b_i,
           lru_lambda, w_rnn_proj, w_attn_proj, w_mix_out, norm_xq_g, norm_mem_g, w_xq, w_xkv, w_xo,
           norm_ffn_g, w_ffn_gate, w_ffn_up, w_ffn_down, norm_final_g):
    batch, seq, d = x.shape
    mem_len = mem.shape[1]
    assert w_in.shape[0] == 1, "only DEPTH == 1 is supported"
    x2d = x.reshape(batch * seq, d)
    mem2d = mem.reshape(batch * mem_len, d)
    cosf, sinf = _rope_tables(positions)
    p = dict(norm_mix_g=norm_mix_g[0], w_in=w_in[0], conv_w=conv_w[0], conv_b=conv_b[0],
             lru_w_a=lru_w_a[0], lru_b_a=lru_b_a[0], lru_w_i=lru_w_i[0], lru_b_i=lru_b_i[0],
             lru_lambda=lru_lambda[0], w_rnn_proj=w_rnn_proj[0], w_attn_proj=w_attn_proj[0],
             w_mix_out=w_mix_out[0], norm_xq_g=norm_xq_g[0], norm_mem_g=norm_mem_g[0],
             w_xq=w_xq[0], w_xkv=w_xkv[0], w_xo=w_xo[0])
    x2 = _layer(x2d, mem2d, cosf, sinf, batch, seq, mem_len, p)
    out = _ffn(x2, norm_ffn_g[0], w_ffn_gate[0].astype(BF16), w_ffn_up[0].astype(BF16),
               w_ffn_down[0].astype(BF16), norm_final_g)
    return out.reshape(batch, seq, d)
```

```python
import functools

import jax
import jax.numpy as jnp
from jax import lax
from jax.experimental import pallas as pl
from jax.experimental.pallas import tpu as pltpu

D_MODEL = 2048
N_HEADS = 16
HEAD_DIM = 128
D_ATTN = N_HEADS * HEAD_DIM
MOBA_BLOCK = 256
MOBA_TOPK = 3
ROPE_THETA = 500000.0
ROT_DIM = HEAD_DIM // 4
ROT_HALF = ROT_DIM // 2
D_RNN = 2048
N_RNN_BLOCKS = 16
RNN_BLOCK = D_RNN // N_RNN_BLOCKS
CONV_WIDTH = 4
LRU_C = 8.0
MEM_HEADS = 4
MEM_HEAD_DIM = 128
D_MEM = MEM_HEADS * MEM_HEAD_DIM
RMS_EPS = 1e-6
NEG_INF = -1e30
LOG2_E = 1.4426950408889634

LANES = 128
SUBLANES = 8
BF16_ROWS = 16
MXU_COLS = 2 * 256
ROT_PARTNER = LANES // 2
MIB = 1024 * 1024

BF16 = jnp.bfloat16
F32 = jnp.float32

_NT = (((1,), (1,)), ((), ()))
_TN = (((0,), (0,)), ((), ()))


def _params(semantics, vmem_mib):
    return pltpu.CompilerParams(dimension_semantics=semantics,
                                vmem_limit_bytes=vmem_mib * MIB)


def _rms(x, g):
    ms = jnp.mean(x * x, axis=-1, keepdims=True)
    return x * lax.rsqrt(ms + RMS_EPS) * g


def _norm_kernel(x_ref, g_ref, o_ref):
    o_ref[...] = _rms(x_ref[...], g_ref[...]).astype(o_ref.dtype)


def _norm_bf16(x, g, tm=512):
    m, d = x.shape
    return pl.pallas_call(
        _norm_kernel,
        grid=(m // tm,),
        in_specs=[pl.BlockSpec((tm, d), lambda i: (i, 0)),
                  pl.BlockSpec((1, d), lambda i: (0, 0))],
        out_specs=pl.BlockSpec((tm, d), lambda i: (i, 0)),
        out_shape=jax.ShapeDtypeStruct((m, d), BF16),
        name="norm_mix",
        compiler_params=_params(("parallel",), 32),
    )(x, g.reshape(1, d))


def _rope_table_kernel(pos_ref, invf_ref, cos_ref, sin_ref):
    ang = pos_ref[...] * invf_ref[...]
    lane = lax.broadcasted_iota(jnp.int32, ang.shape, 1)
    s = jnp.sin(ang)
    cos_ref[...] = jnp.cos(ang)
    sin_ref[...] = jnp.where(lane < ROT_PARTNER, -s, s)


def _rope_head_layout():
    split = ROT_DIM + ROT_PARTNER - ROT_HALF
    return [(0, ROT_HALF), (ROT_DIM, split), (ROT_HALF, ROT_DIM), (split, HEAD_DIM)]


def _rope_tables(positions, tr=1024):
    n = positions.size
    pos = positions.astype(F32).reshape(n, 1)
    inv_freq = jnp.power(ROPE_THETA, -jnp.arange(ROT_HALF, dtype=F32) / ROT_HALF)
    gap = jnp.zeros((ROT_PARTNER - ROT_HALF,), F32)
    invf = jnp.concatenate([inv_freq, gap, inv_freq, gap]).reshape(1, LANES)
    return pl.pallas_call(
        _rope_table_kernel,
        grid=(n // tr,),
        in_specs=[pl.BlockSpec((tr, 1), lambda i: (i, 0)),
                  pl.BlockSpec((1, LANES), lambda i: (0, 0))],
        out_specs=[pl.BlockSpec((tr, LANES), lambda i: (i, 0)),
                   pl.BlockSpec((tr, LANES), lambda i: (i, 0))],
        out_shape=[jax.ShapeDtypeStruct((n, LANES), F32),
                   jax.ShapeDtypeStruct((n, LANES), F32)],
        name="rope_tables",
        compiler_params=_params(("parallel",), 32),
    )(pos, invf)


def _mm_plain_kernel(a_ref, w_ref, o_ref):
    acc = jnp.dot(a_ref[...], w_ref[...], preferred_element_type=F32)
    o_ref[...] = acc.astype(o_ref.dtype)


def _mm_transposed_kernel(a_ref, w_ref, o_ref):
    for c0 in range(0, w_ref.shape[1], MXU_COLS):
        acc = jnp.dot(a_ref[...], w_ref[:, c0:c0 + MXU_COLS], preferred_element_type=F32)
        o_ref[c0:c0 + MXU_COLS, :] = acc.T.astype(o_ref.dtype)


def _mm_sigmoid_kernel(a_ref, w_ref, o_ref):
    for c0 in range(0, w_ref.shape[1], MXU_COLS):
        acc = jnp.dot(a_ref[...], w_ref[:, c0:c0 + MXU_COLS], preferred_element_type=F32)
        o_ref[:, c0:c0 + MXU_COLS] = (0.5 * jnp.tanh(0.5 * acc) + 0.5).astype(o_ref.dtype)


def _mm_residual_kernel(a_ref, w_ref, r_ref, o_ref):
    acc = jnp.dot(a_ref[...], w_ref[...], preferred_element_type=F32)
    o_ref[...] = r_ref[...] + acc


def _mm_rope_kernel(a_ref, w_ref, cos_ref, sin_ref, o_ref, mean_ref):
    tm, tn = o_ref.shape
    cosf = cos_ref[...]
    sinf = sin_ref[...]
    for c0 in range(0, tn, MXU_COLS):
        acc = jnp.dot(a_ref[...], w_ref[:, c0:c0 + MXU_COLS], preferred_element_type=F32)
        for h0 in range(0, MXU_COLS, HEAD_DIM):
            a = acc[:, h0:h0 + HEAD_DIM]
            r = a * cosf + pltpu.roll(a, ROT_PARTNER, 1) * sinf
            cols = slice(c0 + h0, c0 + h0 + HEAD_DIM)
            o_ref[:, cols] = r.astype(o_ref.dtype)
            mean_ref[0, :, cols] = jnp.mean(r.reshape(tm // MOBA_BLOCK, MOBA_BLOCK, HEAD_DIM), axis=1)


def _matmul(kernel, name, a, w, *, col_off, n_cols, out_dtype, tm=1024, tn=512,
            extra=(), extra_specs=(), extra_out_shape=(), extra_out_specs=(), vmem_mib=48,
            transposed_out=False):
    m, k = a.shape
    off = col_off // tn
    if transposed_out:
        main_shape, main_spec = (n_cols, m), pl.BlockSpec((tn, tm), lambda i, j: (j, i))
    else:
        main_shape, main_spec = (m, n_cols), pl.BlockSpec((tm, tn), lambda i, j: (i, j))
    out_shape = [jax.ShapeDtypeStruct(main_shape, out_dtype)] + list(extra_out_shape)
    out_specs = [main_spec] + list(extra_out_specs)
    res = pl.pallas_call(
        kernel,
        grid=(m // tm, n_cols // tn),
        in_specs=[pl.BlockSpec((tm, k), lambda i, j: (i, 0)),
                  pl.BlockSpec((k, tn), lambda i, j: (0, j + off))] + list(extra_specs),
        out_specs=out_specs,
        out_shape=out_shape,
        name=name,
        compiler_params=_params(("parallel", "arbitrary"), vmem_mib),
    )(a, w, *extra)
    return res if extra_out_shape else res[0]


def _rglru_kernel(xr_ref, yr_ref, cw_ref, cb_ref, wa_ref, wi_ref, ba_ref, bi_ref, lam_ref,
                  o_ref, xbuf, a_scr, u_scr, h_scr, nat_scr, tail_scr, hc_scr):
    ts, tc = xr_ref.shape
    nv = ts // SUBLANES
    halo = (CONV_WIDTH - 1) * SUBLANES
    t = pl.program_id(2)

    def grp(g):
        return slice(halo + g * SUBLANES, halo + (g + 1) * SUBLANES)

    @pl.when(t == 0)
    def _():
        tail_scr[...] = jnp.zeros_like(tail_scr)
        hc_scr[...] = jnp.zeros_like(hc_scr)

    lane_blocks = [slice(cb * LANES, (cb + 1) * LANES) for cb in range(tc // LANES)]
    for cb, ls in enumerate(lane_blocks):
        nat_scr[cb] = xr_ref[:, ls]
    for v in range(nv):
        for cb, ls in enumerate(lane_blocks):
            xbuf[grp(v), ls] = nat_scr[cb, pl.ds(v, SUBLANES, stride=nv), :]
    sub = lax.broadcasted_iota(jnp.int32, (SUBLANES, tc), 0)
    for d in range(1, CONV_WIDTH):
        keep = slice((CONV_WIDTH - 1 - d) * SUBLANES, (CONV_WIDTH - d) * SUBLANES)
        cur = xbuf[grp(nv - d), :]
        xbuf[grp(-d), :] = pltpu.roll(jnp.where(sub == SUBLANES - 1, tail_scr[keep, :], cur), 1, 0)
        tail_scr[keep, :] = cur

    xc = jnp.zeros((ts, tc), F32) + cb_ref[...]
    for kk in range(CONV_WIDTH):
        start = halo - (CONV_WIDTH - 1 - kk) * SUBLANES
        xc = xc + cw_ref[kk:kk + 1, :] * xbuf[start:start + ts, :]

    half_a = (-0.5 * LRU_C) * jax.nn.softplus(-lam_ref[...])
    for nb in range(tc // RNN_BLOCK):
        sl = slice(nb * RNN_BLOCK, (nb + 1) * RNN_BLOCK)
        xb = xc[:, sl]
        xb16 = xb.astype(BF16)
        zr = jnp.dot(xb16, wa_ref[nb], preferred_element_type=F32) + ba_ref[:, sl]
        zi = jnp.dot(xb16, wi_ref[nb], preferred_element_type=F32) + bi_ref[:, sl]
        log_a = half_a[:, sl] * jnp.tanh(0.5 * zr) + half_a[:, sl]
        ig = 0.5 * jnp.tanh(0.5 * zi) + 0.5
        a_scr[:, sl] = jnp.exp(log_a)
        th = jnp.tanh(log_a)
        y = -2.0 * th / (1.0 - th)
        u_scr[:, sl] = jnp.where(y > 0.0, y * lax.rsqrt(y), 0.0) * (ig * xb)

    h_end = jnp.zeros((SUBLANES, tc), F32)
    p_end = jnp.ones((SUBLANES, tc), F32)
    for v in range(nv):
        rows = slice(v * SUBLANES, (v + 1) * SUBLANES)
        a = a_scr[rows, :]
        h_end = a * h_end + u_scr[rows, :]
        p_end = a * p_end
        h_scr[rows, :] = h_end
        a_scr[rows, :] = p_end
    h_in = hc_scr[...]
    entering = []
    for s in range(SUBLANES):
        entering.append(h_in)
        h_in = h_end[s:s + 1, :] + p_end[s:s + 1, :] * h_in
    hc_scr[...] = h_in
    h_enter = jnp.concatenate(entering, axis=0)

    for v in range(nv):
        rows = slice(v * SUBLANES, (v + 1) * SUBLANES)
        h_v = h_scr[rows, :] + a_scr[rows, :] * h_enter
        for cb, ls in enumerate(lane_blocks):
            nat_scr[cb, pl.ds(v, SUBLANES, stride=nv), :] = h_v[:, ls]
    for cb, ls in enumerate(lane_blocks):
        o_ref[:, ls] = (jax.nn.gelu(yr_ref[:, ls]) * nat_scr[cb]).astype(o_ref.dtype)


def _rglru(xy, conv_w, conv_b, w_a, b_a, w_i, b_i, lam, batch, seq, ts=512, tc=512):
    n = batch * seq
    nt = seq // ts
    ncb = D_RNN // tc
    halo = (CONV_WIDTH - 1) * SUBLANES
    row = lambda v: v.reshape(1, D_RNN)
    vec_spec = pl.BlockSpec((1, tc), lambda b, c, t: (0, c))
    gate_spec = pl.BlockSpec((tc // RNN_BLOCK, RNN_BLOCK, RNN_BLOCK), lambda b, c, t: (c, 0, 0))
    return pl.pallas_call(
        _rglru_kernel,
        grid=(batch, ncb, nt),
        in_specs=[pl.BlockSpec((ts, tc), lambda b, c, t: (b * nt + t, c)),
                  pl.BlockSpec((ts, tc), lambda b, c, t: (b * nt + t, ncb + c)),
                  pl.BlockSpec((CONV_WIDTH, tc), lambda b, c, t: (0, c)),
                  vec_spec, gate_spec, gate_spec, vec_spec, vec_spec, vec_spec],
        out_specs=pl.BlockSpec((ts, tc), lambda b, c, t: (b * nt + t, c)),
        out_shape=jax.ShapeDtypeStruct((n, D_RNN), BF16),
        scratch_shapes=[pltpu.VMEM((halo + ts, tc), F32),
                        pltpu.VMEM((ts, tc), F32),
                        pltpu.VMEM((ts, tc), F32),
                        pltpu.VMEM((ts, tc), F32),
                        pltpu.VMEM((tc // LANES, ts, LANES), F32),
                        pltpu.VMEM((halo, tc), F32),
                        pltpu.VMEM((1, tc), F32)],
        name="rglru",
        compiler_params=_params(("parallel", "parallel", "arbitrary"), 32),
    )(xy, xy, conv_w, row(conv_b), w_a.astype(BF16), w_i.astype(BF16), row(b_a), row(b_i), row(lam))


def _moba_kernel(q_ref, k_ref, v_ref, km_ref, o_ref, bias_scr, s_scr, acc_scr, *, heads):
    j = pl.program_id(2)
    nblk = km_ref.shape[0]
    blk = MOBA_BLOCK
    c = (HEAD_DIM ** -0.5) * LOG2_E
    head_slices = [slice(h * HEAD_DIM, (h + 1) * HEAD_DIM) for h in range(heads)]

    def raw_scores(block):
        start = pl.multiple_of(block * blk, blk)
        return [lax.dot_general(k_ref[pl.ds(start, blk), hs], q_ref[:, hs], _NT,
                                preferred_element_type=F32) for hs in head_slices]

    def score_into(slot, block, raw=None):
        raw = raw_scores(block) if raw is None else raw
        for h in range(heads):
            s_scr[slot, h] = raw[h] + bias_scr[h, pl.ds(block, 1), :]

    ones_rows = jnp.ones((BF16_ROWS, blk), BF16)

    def attend(slot, mask, block, state):
        start = pl.multiple_of(block * blk, blk)
        half = blk // 2
        soft = []
        for h in range(heads):
            m = state[h]
            s_lo = mask(s_scr[slot, h, 0:half, :], 0)
            s_hi = mask(s_scr[slot, h, half:blk, :], half)
            m_blk = jnp.maximum(jnp.max(s_lo, axis=0, keepdims=True), jnp.max(s_hi, axis=0, keepdims=True))
            m_new = jnp.maximum(m, m_blk)
            alpha = jnp.exp2((m - m_new) * c)
            p = jnp.exp2((mask(s_scr[slot, h], 0) - m_new) * c)
            soft.append((m_new, alpha, p.astype(BF16)))
        out = []
        for h, hs in enumerate(head_slices):
            m_new, alpha, p = soft[h]
            vt = jnp.concatenate([v_ref[hs, pl.ds(start, blk)], ones_rows], axis=0)
            pv = jnp.dot(vt, p, preferred_element_type=F32)
            acc_scr[h] = alpha * acc_scr[h] + pv
            out.append(m_new)
        return tuple(out)

    no_mask = lambda s, row0: s

    raw0 = raw_scores(0)
    for h, hs in enumerate(head_slices):
        q = q_ref[:, hs]
        km = km_ref[:, hs]
        km_hi = km.astype(BF16)
        km_lo = (km - km_hi.astype(F32)).astype(BF16)
        gate = (lax.dot_general(km_hi, q, _NT, preferred_element_type=F32)
                + lax.dot_general(km_lo, q, _NT, preferred_element_type=F32))
        bidx = lax.broadcasted_iota(jnp.int32, gate.shape, 0)
        past = bidx < j
        g = jnp.where(past, gate, -jnp.inf)
        rank = jnp.zeros(gate.shape, jnp.int32)
        for other in range(nblk):
            go = g[other:other + 1, :]
            beats = jnp.where(go > g, 1, jnp.where(go == g, jnp.where(bidx > other, 1, 0), 0))
            rank = rank + beats
        bias_scr[h] = jnp.where(past, jnp.where(rank < MOBA_TOPK, 0.0, NEG_INF),
                                jnp.where(bidx == j, 0.0, NEG_INF))
    score_into(0, 0, raw0)

    def pair_body(t, state):
        first = 2 * t
        score_into(1, first + 1)
        state = attend(0, no_mask, first, state)
        score_into(0, first + 2)
        return attend(1, no_mask, first + 1, state)

    def odd_body(state):
        score_into(1, j)
        return attend(0, no_mask, j - 1, state)

    acc_scr[...] = jnp.zeros_like(acc_scr)
    init = tuple(jnp.full((1, blk), NEG_INF, F32) for _ in range(heads))
    state = lax.fori_loop(0, lax.shift_right_logical(j, 1), pair_body, init)
    odd = lax.bitwise_and(j, 1)
    state = lax.cond(odd == 1, odd_body, lambda st: st, state)

    def causal_mask(s, row0):
        kpos = row0 + lax.broadcasted_iota(jnp.int32, s.shape, 0)
        qpos = lax.broadcasted_iota(jnp.int32, s.shape, 1)
        return jnp.where(kpos <= qpos, s, NEG_INF)

    attend(odd, causal_mask, j, state)
    for h, hs in enumerate(head_slices):
        o_ref[:, hs] = (acc_scr[h, 0:HEAD_DIM, :] / acc_scr[h, HEAD_DIM:HEAD_DIM + 1, :]
                        ).T.astype(o_ref.dtype)


def _moba(qk, v_t, kmean, batch, seq, heads=4):
    n = batch * seq
    nblk = seq // MOBA_BLOCK
    groups = N_HEADS // heads
    width = heads * HEAD_DIM
    return pl.pallas_call(
        functools.partial(_moba_kernel, heads=heads),
        grid=(batch, groups, nblk),
        in_specs=[pl.BlockSpec((MOBA_BLOCK, width), lambda b, h, j: (b * nblk + j, h)),
                  pl.BlockSpec((seq, width), lambda b, h, j: (b, groups + h)),
                  pl.BlockSpec((width, seq), lambda b, h, j: (h, b)),
                  pl.BlockSpec((nblk, width), lambda b, h, j: (b, groups + h))],
        out_specs=pl.BlockSpec((MOBA_BLOCK, width), lambda b, h, j: (b * nblk + j, h)),
        out_shape=jax.ShapeDtypeStruct((n, D_ATTN), BF16),
        scratch_shapes=[pltpu.VMEM((heads, nblk, MOBA_BLOCK), F32),
                        pltpu.VMEM((2, heads, MOBA_BLOCK, MOBA_BLOCK), F32),
                        pltpu.VMEM((heads, HEAD_DIM + BF16_ROWS, MOBA_BLOCK), F32)],
        name="moba",
        compiler_params=_params(("parallel", "parallel", "arbitrary"), 56),
    )(qk, qk, v_t, kmean)


def _merge_kernel(gh_ref, o_ref, wr_ref, wa_ref, gr_ref, ga_ref, out_ref):
    rnn = jnp.dot(gh_ref[...], wr_ref[...], preferred_element_type=F32)
    att = jnp.dot(o_ref[...], wa_ref[...], preferred_element_type=F32)
    out_ref[...] = (gr_ref[...].astype(F32) * rnn + ga_ref[...].astype(F32) * att).astype(out_ref.dtype)


def _merge(gh, o, w_rnn, w_attn, gates, tm=1024, tn=512):
    m, k = gh.shape
    nj = D_MODEL // tn
    a_spec = pl.BlockSpec((tm, k), lambda i, j: (i, 0))
    w_spec = pl.BlockSpec((k, tn), lambda i, j: (0, j))
    return pl.pallas_call(
        _merge_kernel,
        grid=(m // tm, nj),
        in_specs=[a_spec, a_spec, w_spec, w_spec,
                  pl.BlockSpec((tm, tn), lambda i, j: (i, j)),
                  pl.BlockSpec((tm, tn), lambda i, j: (i, nj + j))],
        out_specs=pl.BlockSpec((tm, tn), lambda i, j: (i, j)),
        out_shape=jax.ShapeDtypeStruct((m, D_MODEL), BF16),
        name="merge",
        compiler_params=_params(("parallel", "arbitrary"), 48),
    )(gh, o, w_rnn, w_attn, gates, gates)


def _memkv_kernel(mem_ref, g_ref, w_ref, o_ref):
    hn = _rms(mem_ref[...], g_ref[...]).astype(BF16)
    o_ref[...] = jnp.dot(hn, w_ref[...], preferred_element_type=F32).astype(o_ref.dtype)


def _memkv(mem2d, g, w_kv, mem_len):
    m, d = mem2d.shape
    return pl.pallas_call(
        _memkv_kernel,
        grid=(m // mem_len,),
        in_specs=[pl.BlockSpec((mem_len, d), lambda i: (i, 0)),
                  pl.BlockSpec((1, d), lambda i: (0, 0)),
                  pl.BlockSpec((d, 2 * D_MEM), lambda i: (0, 0))],
        out_specs=pl.BlockSpec((mem_len, 2 * D_MEM), lambda i: (i, 0)),
        out_shape=jax.ShapeDtypeStruct((m, 2 * D_MEM), BF16),
        name="mem_kv",
        compiler_params=_params(("parallel",), 32),
    )(mem2d, g.reshape(1, d), w_kv)


def _xattn_kernel(x_ref, g_ref, wq_ref, kv_ref, wo_ref, o_ref):
    x = x_ref[...]
    hn = _rms(x, g_ref[...]).astype(BF16)
    q = jnp.dot(hn, wq_ref[...], preferred_element_type=F32).astype(BF16)
    scale = MEM_HEAD_DIM ** -0.5
    heads = []
    for hd in range(MEM_HEADS):
        sl = slice(hd * MEM_HEAD_DIM, (hd + 1) * MEM_HEAD_DIM)
        kh = kv_ref[:, sl]
        vh = kv_ref[:, D_MEM + hd * MEM_HEAD_DIM:D_MEM + (hd + 1) * MEM_HEAD_DIM]
        s = lax.dot_general(q[:, sl], kh, _NT, preferred_element_type=F32) * scale
        m = jnp.max(s, axis=-1, keepdims=True)
        p = jnp.exp(s - m)
        l = jnp.sum(p, axis=-1, keepdims=True)
        oh = jnp.dot(p.astype(BF16), vh, preferred_element_type=F32) / l
        heads.append(oh.astype(BF16))
    o_all = jnp.concatenate(heads, axis=-1)
    o_ref[...] = x + jnp.dot(o_all, wo_ref[...], preferred_element_type=F32)


def _xattn(x, g, w_q, kv, w_o, seq, mem_len, tm=512):
    m, d = x.shape
    per_batch = seq // tm
    return pl.pallas_call(
        _xattn_kernel,
        grid=(m // tm,),
        in_specs=[pl.BlockSpec((tm, d), lambda i: (i, 0)),
                  pl.BlockSpec((1, d), lambda i: (0, 0)),
                  pl.BlockSpec((d, D_MEM), lambda i: (0, 0)),
                  pl.BlockSpec((mem_len, 2 * D_MEM), lambda i: (i // per_batch, 0)),
                  pl.BlockSpec((D_MEM, d), lambda i: (0, 0))],
        out_specs=pl.BlockSpec((tm, d), lambda i: (i, 0)),
        out_shape=jax.ShapeDtypeStruct((m, d), F32),
        name="xattn",
        compiler_params=_params(("parallel",), 48),
    )(x, g.reshape(1, d), w_q, kv, w_o)


def _ffn_kernel(x_ref, g_ref, wg_ref, wu_ref, wd_ref, gf_ref, o_ref, h_scr, acc_scr):
    f = pl.program_id(1)

    @pl.when(f == 0)
    def _():
        h_scr[...] = _rms(x_ref[...], g_ref[...]).astype(BF16)
        acc_scr[...] = jnp.zeros_like(acc_scr)

    hn = h_scr[...]
    a = jnp.dot(hn, wg_ref[...], preferred_element_type=F32)
    b = jnp.dot(hn, wu_ref[...], preferred_element_type=F32)
    act = (jax.nn.silu(a) * b).astype(BF16)
    acc_scr[...] += jnp.dot(act, wd_ref[...], preferred_element_type=F32)

    @pl.when(f == pl.num_programs(1) - 1)
    def _():
        o_ref[...] = _rms(x_ref[...] + acc_scr[...], gf_ref[...])


def _ffn(x, g, w_gate, w_up, w_down, g_final, tm=512, tf=512):
    m, d = x.shape
    d_ff = w_gate.shape[1]
    return pl.pallas_call(
        _ffn_kernel,
        grid=(m // tm, d_ff // tf),
        in_specs=[pl.BlockSpec((tm, d), lambda i, f: (i, 0)),
                  pl.BlockSpec((1, d), lambda i, f: (0, 0)),
                  pl.BlockSpec((d, tf), lambda i, f: (0, f)),
                  pl.BlockSpec((d, tf), lambda i, f: (0, f)),
                  pl.BlockSpec((tf, d), lambda i, f: (f, 0)),
                  pl.BlockSpec((1, d), lambda i, f: (0, 0))],
        out_specs=pl.BlockSpec((tm, d), lambda i, f: (i, 0)),
        out_shape=jax.ShapeDtypeStruct((m, d), F32),
        scratch_shapes=[pltpu.VMEM((tm, d), BF16), pltpu.VMEM((tm, d), F32)],
        name="ffn",
        compiler_params=_params(("parallel", "arbitrary"), 48),
    )(x, g.reshape(1, d), w_gate, w_up, w_down, g_final.reshape(1, d))


def _layer(x2d, mem2d, cosf, sinf, batch, seq, mem_len, p):
    n = x2d.shape[0]
    qk0, qk1 = 2 * D_RNN, 2 * D_RNN + 2 * D_ATTN
    w_f32 = p["w_in"]
    w_qk = w_f32[:, qk0:qk1].reshape(w_f32.shape[0], 2 * N_HEADS, HEAD_DIM)
    w_qk = jnp.concatenate([w_qk[:, :, a:b] for a, b in _rope_head_layout()], axis=2)
    w_in = jnp.concatenate([w_f32[:, :qk0].astype(BF16), w_qk.reshape(w_f32.shape[0], -1).astype(BF16),
                            w_f32[:, qk1:].astype(BF16)], axis=1)
    hn = _norm_bf16(x2d, p["norm_mix_g"])

    c0 = 0
    wide = 1024
    xy = _matmul(_mm_plain_kernel, "proj_xy", hn, w_in, col_off=c0, n_cols=2 * D_RNN, out_dtype=F32,
                 tn=wide)
    c0 += 2 * D_RNN
    tm, tn = 1024, wide
    rope_spec = pl.BlockSpec((tm, LANES), lambda i, j: (i, 0))
    qk, means = _matmul(
        _mm_rope_kernel, "proj_qk", hn, w_in, col_off=c0, n_cols=2 * D_ATTN, out_dtype=BF16,
        tm=tm, tn=tn, extra=(cosf, sinf), extra_specs=(rope_spec, rope_spec),
        extra_out_shape=(jax.ShapeDtypeStruct((n // tm, tm // MOBA_BLOCK, 2 * D_ATTN), F32),),
        extra_out_specs=(pl.BlockSpec((1, tm // MOBA_BLOCK, tn), lambda i, j: (i, 0, j)),))
    c0 += 2 * D_ATTN
    v_t = _matmul(_mm_transposed_kernel, "proj_v", hn, w_in, col_off=c0, n_cols=D_ATTN,
                  out_dtype=BF16, tn=wide, transposed_out=True)
    c0 += D_ATTN
    gates = _matmul(_mm_sigmoid_kernel, "proj_gates", hn, w_in, col_off=c0, n_cols=2 * D_MODEL,
                    out_dtype=BF16, tn=wide)

    gh = _rglru(xy, p["conv_w"], p["conv_b"], p["lru_w_a"], p["lru_b_a"], p["lru_w_i"],
                p["lru_b_i"], p["lru_lambda"], batch, seq)
    kmean = means.reshape(n // MOBA_BLOCK, 2 * D_ATTN)
    o = _moba(qk, v_t, kmean, batch, seq)

    merged = _merge(gh, o, p["w_rnn_proj"].astype(BF16), p["w_attn_proj"].astype(BF16), gates)
    x1 = _matmul(_mm_residual_kernel, "mix_out", merged, p["w_mix_out"].astype(BF16), col_off=0,
                 n_cols=D_MODEL, out_dtype=F32, tm=512, tn=D_MODEL,
                 extra=(x2d,), extra_specs=(pl.BlockSpec((512, D_MODEL), lambda i, j: (i, j)),))

    kv = _memkv(mem2d, p["norm_mem_g"], p["w_xkv"].astype(BF16), mem_len)
    x2 = _xattn(x1, p["norm_xq_g"], p["w_xq"].astype(BF16), kv, p["w_xo"].astype(BF16), seq, mem_len)
    return x2


def kernel(x, mem, positions, norm_mix_g, w_in, conv_w, conv_b, lru_w_a, lru_b_a, lru_w_i, lru_b_i,
           lru_lambda, w_rnn_proj, w_attn_proj, w_mix_out, norm_xq_g, norm_mem_g, w_xq, w_xkv, w_xo,
           norm_ffn_g, w_ffn_gate, w_ffn_up, w_ffn_down, norm_final_g):
    batch, seq, d = x.shape
    mem_len = mem.shape[1]
    assert w_in.shape[0] == 1, "only DEPTH == 1 is supported"
    x2d = x.reshape(batch * seq, d)
    mem2d = mem.reshape(batch * mem_len, d)
    cosf, sinf = _rope_tables(positions)
    p = dict(norm_mix_g=norm_mix_g[0], w_in=w_in[0], conv_w=conv_w[0], conv_b=conv_b[0],
             lru_w_a=lru_w_a[0], lru_b_a=lru_b_a[0], lru_w_i=lru_w_i[0], lru_b_i=lru_b_i[0],
             lru_lambda=lru_lambda[0], w_rnn_proj=w_rnn_proj[0], w_attn_proj=w_attn_proj[0],
             w_mix_out=w_mix_out[0], norm_xq_g=norm_xq_g[0], norm_mem_g=norm_mem_g[0],
             w_xq=w_xq[0], w_xkv=w_xkv[0], w_xo=w_xo[0])
    x2 = _layer(x2d, mem2d, cosf, sinf, batch, seq, mem_len, p)
    out = _ffn(x2, norm_ffn_g[0], w_ffn_gate[0].astype(BF16), w_ffn_up[0].astype(BF16),
               w_ffn_down[0].astype(BF16), norm_final_g)
    return out.reshape(batch, seq, d)
```

```python
import functools

import jax
import jax.numpy as jnp
from jax import lax
from jax.experimental import pallas as pl
from jax.experimental.pallas import tpu as pltpu

D_MODEL = 2048
N_HEADS = 16
HEAD_DIM = 128
D_ATTN = N_HEADS * HEAD_DIM
MOBA_BLOCK = 256
MOBA_TOPK = 3
ROPE_THETA = 500000.0
ROT_DIM = HEAD_DIM // 4
ROT_HALF = ROT_DIM // 2
D_RNN = 2048
N_RNN_BLOCKS = 16
RNN_BLOCK = D_RNN // N_RNN_BLOCKS
CONV_WIDTH = 4
LRU_C = 8.0
MEM_HEADS = 4
MEM_HEAD_DIM = 128
D_MEM = MEM_HEADS * MEM_HEAD_DIM
RMS_EPS = 1e-6
NEG_INF = -1e30
LOG2_E = 1.4426950408889634

LANES = 128
SUBLANES = 8
BF16_ROWS = 16
MXU_COLS = 2 * 256
ROT_PARTNER = LANES // 2
MIB = 1024 * 1024

BF16 = jnp.bfloat16
F32 = jnp.float32

_NT = (((1,), (1,)), ((), ()))
_TN = (((0,), (0,)), ((), ()))


def _params(semantics, vmem_mib):
    return pltpu.CompilerParams(dimension_semantics=semantics,
                                vmem_limit_bytes=vmem_mib * MIB)


def _rms(x, g):
    ms = jnp.mean(x * x, axis=-1, keepdims=True)
    return x * lax.rsqrt(ms + RMS_EPS) * g


def _norm_kernel(x_ref, g_ref, o_ref):
    o_ref[...] = _rms(x_ref[...], g_ref[...]).astype(o_ref.dtype)


def _norm_bf16(x, g, tm=512):
    m, d = x.shape
    return pl.pallas_call(
        _norm_kernel,
        grid=(m // tm,),
        in_specs=[pl.BlockSpec((tm, d), lambda i: (i, 0)),
                  pl.BlockSpec((1, d), lambda i: (0, 0))],
        out_specs=pl.BlockSpec((tm, d), lambda i: (i, 0)),
        out_shape=jax.ShapeDtypeStruct((m, d), BF16),
        name="norm_mix",
        compiler_params=_params(("parallel",), 32),
    )(x, g.reshape(1, d))


def _rope_table_kernel(pos_ref, invf_ref, cos_ref, sin_ref):
    ang = pos_ref[...] * invf_ref[...]
    lane = lax.broadcasted_iota(jnp.int32, ang.shape, 1)
    s = jnp.sin(ang)
    cos_ref[...] = jnp.cos(ang)
    sin_ref[...] = jnp.where(lane < ROT_PARTNER, -s, s)


def _rope_head_layout():
    split = ROT_DIM + ROT_PARTNER - ROT_HALF
    return [(0, ROT_HALF), (ROT_DIM, split), (ROT_HALF, ROT_DIM), (split, HEAD_DIM)]


def _qk_weight_kernel(w_ref, o_ref):
    lane = lax.broadcasted_iota(jnp.int32, (w_ref.shape[0], HEAD_DIM), 1)
    for h0 in range(0, w_ref.shape[1], HEAD_DIM):
        x = w_ref[:, h0:h0 + HEAD_DIM]
        out, at = x, 0
        for start, stop in _rope_head_layout():
            if start != at:
                moved = pltpu.roll(x, (at - start) % HEAD_DIM, 1)
                out = jnp.where((lane >= at) & (lane < at + stop - start), moved, out)
            at += stop - start
        o_ref[:, h0:h0 + HEAD_DIM] = out.astype(o_ref.dtype)


def _qk_weights(w_in, col0, n_cols, tr=256, tc=1024):
    k = w_in.shape[0]
    off = col0 // tc
    return pl.pallas_call(
        _qk_weight_kernel,
        grid=(k // tr, n_cols // tc),
        in_specs=[pl.BlockSpec((tr, tc), lambda i, j: (i, j + off))],
        out_specs=pl.BlockSpec((tr, tc), lambda i, j: (i, j)),
        out_shape=jax.ShapeDtypeStruct((k, n_cols), BF16),
        name="qk_weights",
        compiler_params=_params(("parallel", "parallel"), 32),
    )(w_in)


def _rope_tables(positions, tr=1024):
    n = positions.size
    pos = positions.astype(F32).reshape(n, 1)
    inv_freq = jnp.power(ROPE_THETA, -jnp.arange(ROT_HALF, dtype=F32) / ROT_HALF)
    gap = jnp.zeros((ROT_PARTNER - ROT_HALF,), F32)
    invf = jnp.concatenate([inv_freq, gap, inv_freq, gap]).reshape(1, LANES)
    return pl.pallas_call(
        _rope_table_kernel,
        grid=(n // tr,),
        in_specs=[pl.BlockSpec((tr, 1), lambda i: (i, 0)),
                  pl.BlockSpec((1, LANES), lambda i: (0, 0))],
        out_specs=[pl.BlockSpec((tr, LANES), lambda i: (i, 0)),
                   pl.BlockSpec((tr, LANES), lambda i: (i, 0))],
        out_shape=[jax.ShapeDtypeStruct((n, LANES), F32),
                   jax.ShapeDtypeStruct((n, LANES), F32)],
        name="rope_tables",
        compiler_params=_params(("parallel",), 32),
    )(pos, invf)


def _mm_plain_kernel(a_ref, w_ref, o_ref):
    acc = jnp.dot(a_ref[...], w_ref[...], preferred_element_type=F32)
    o_ref[...] = acc.astype(o_ref.dtype)


def _mm_transposed_kernel(a_ref, w_ref, o_ref):
    for c0 in range(0, w_ref.shape[1], MXU_COLS):
        acc = jnp.dot(a_ref[...], w_ref[:, c0:c0 + MXU_COLS], preferred_element_type=F32)
        o_ref[c0:c0 + MXU_COLS, :] = acc.T.astype(o_ref.dtype)


def _mm_sigmoid_kernel(a_ref, w_ref, o_ref):
    for c0 in range(0, w_ref.shape[1], MXU_COLS):
        acc = jnp.dot(a_ref[...], w_ref[:, c0:c0 + MXU_COLS], preferred_element_type=F32)
        o_ref[:, c0:c0 + MXU_COLS] = (0.5 * jnp.tanh(0.5 * acc) + 0.5).astype(o_ref.dtype)


def _mm_residual_kernel(a_ref, w_ref, r_ref, o_ref):
    acc = jnp.dot(a_ref[...], w_ref[...], preferred_element_type=F32)
    o_ref[...] = r_ref[...] + acc


def _mm_rope_kernel(a_ref, w_ref, cos_ref, sin_ref, o_ref, mean_ref):
    tm, tn = o_ref.shape
    cosf = cos_ref[...]
    sinf = sin_ref[...]
    for c0 in range(0, tn, MXU_COLS):
        acc = jnp.dot(a_ref[...], w_ref[:, c0:c0 + MXU_COLS], preferred_element_type=F32)
        for h0 in range(0, MXU_COLS, HEAD_DIM):
            a = acc[:, h0:h0 + HEAD_DIM]
            r = a * cosf + pltpu.roll(a, ROT_PARTNER, 1) * sinf
            cols = slice(c0 + h0, c0 + h0 + HEAD_DIM)
            o_ref[:, cols] = r.astype(o_ref.dtype)
            mean_ref[0, :, cols] = jnp.mean(r.reshape(tm // MOBA_BLOCK, MOBA_BLOCK, HEAD_DIM), axis=1)


def _matmul(kernel, name, a, w, *, col_off, n_cols, out_dtype, tm=1024, tn=512,
            extra=(), extra_specs=(), extra_out_shape=(), extra_out_specs=(), vmem_mib=48,
            transposed_out=False):
    m, k = a.shape
    off = col_off // tn
    if transposed_out:
        main_shape, main_spec = (n_cols, m), pl.BlockSpec((tn, tm), lambda i, j: (j, i))
    else:
        main_shape, main_spec = (m, n_cols), pl.BlockSpec((tm, tn), lambda i, j: (i, j))
    out_shape = [jax.ShapeDtypeStruct(main_shape, out_dtype)] + list(extra_out_shape)
    out_specs = [main_spec] + list(extra_out_specs)
    res = pl.pallas_call(
        kernel,
        grid=(m // tm, n_cols // tn),
        in_specs=[pl.BlockSpec((tm, k), lambda i, j: (i, 0)),
                  pl.BlockSpec((k, tn), lambda i, j: (0, j + off))] + list(extra_specs),
        out_specs=out_specs,
        out_shape=out_shape,
        name=name,
        compiler_params=_params(("parallel", "arbitrary"), vmem_mib),
    )(a, w, *extra)
    return res if extra_out_shape else res[0]


def _rglru_kernel(xr_ref, yr_ref, cw_ref, cb_ref, wa_ref, wi_ref, ba_ref, bi_ref, lam_ref,
                  o_ref, xbuf, a_scr, u_scr, h_scr, nat_scr, tail_scr, hc_scr):
    ts, tc = xr_ref.shape
    nv = ts // SUBLANES
    halo = (CONV_WIDTH - 1) * SUBLANES
    t = pl.program_id(2)

    def grp(g):
        return slice(halo + g * SUBLANES, halo + (g + 1) * SUBLANES)

    @pl.when(t == 0)
    def _():
        tail_scr[...] = jnp.zeros_like(tail_scr)
        hc_scr[...] = jnp.zeros_like(hc_scr)

    lane_blocks = [slice(cb * LANES, (cb + 1) * LANES) for cb in range(tc // LANES)]
    for cb, ls in enumerate(lane_blocks):
        nat_scr[cb] = xr_ref[:, ls]
    for v in range(nv):
        for cb, ls in enumerate(lane_blocks):
            xbuf[grp(v), ls] = nat_scr[cb, pl.ds(v, SUBLANES, stride=nv), :]
    sub = lax.broadcasted_iota(jnp.int32, (SUBLANES, tc), 0)
    for d in range(1, CONV_WIDTH):
        keep = slice((CONV_WIDTH - 1 - d) * SUBLANES, (CONV_WIDTH - d) * SUBLANES)
        cur = xbuf[grp(nv - d), :]
        xbuf[grp(-d), :] = pltpu.roll(jnp.where(sub == SUBLANES - 1, tail_scr[keep, :], cur), 1, 0)
        tail_scr[keep, :] = cur

    xc = jnp.zeros((ts, tc), F32) + cb_ref[...]
    for kk in range(CONV_WIDTH):
        start = halo - (CONV_WIDTH - 1 - kk) * SUBLANES
        xc = xc + cw_ref[kk:kk + 1, :] * xbuf[start:start + ts, :]

    half_a = (-0.5 * LRU_C) * jax.nn.softplus(-lam_ref[...])
    for nb in range(tc // RNN_BLOCK):
        sl = slice(nb * RNN_BLOCK, (nb + 1) * RNN_BLOCK)
        xb = xc[:, sl]
        xb16 = xb.astype(BF16)
        zr = jnp.dot(xb16, wa_ref[nb], preferred_element_type=F32) + ba_ref[:, sl]
        zi = jnp.dot(xb16, wi_ref[nb], preferred_element_type=F32) + bi_ref[:, sl]
        log_a = half_a[:, sl] * jnp.tanh(0.5 * zr) + half_a[:, sl]
        ig = 0.5 * jnp.tanh(0.5 * zi) + 0.5
        a_scr[:, sl] = jnp.exp(log_a)
        th = jnp.tanh(log_a)
        y = -2.0 * th / (1.0 - th)
        u_scr[:, sl] = jnp.where(y > 0.0, y * lax.rsqrt(y), 0.0) * (ig * xb)

    h_end = jnp.zeros((SUBLANES, tc), F32)
    p_end = jnp.ones((SUBLANES, tc), F32)
    for v in range(nv):
        rows = slice(v * SUBLANES, (v + 1) * SUBLANES)
        a = a_scr[rows, :]
        h_end = a * h_end + u_scr[rows, :]
        p_end = a * p_end
        h_scr[rows, :] = h_end
        a_scr[rows, :] = p_end
    h_in = hc_scr[...]
    entering = []
    for s in range(SUBLANES):
        entering.append(h_in)
        h_in = h_end[s:s + 1, :] + p_end[s:s + 1, :] * h_in
    hc_scr[...] = h_in
    h_enter = jnp.concatenate(entering, axis=0)

    for v in range(nv):
        rows = slice(v * SUBLANES, (v + 1) * SUBLANES)
        h_v = h_scr[rows, :] + a_scr[rows, :] * h_enter
        for cb, ls in enumerate(lane_blocks):
            nat_scr[cb, pl.ds(v, SUBLANES, stride=nv), :] = h_v[:, ls]
    for cb, ls in enumerate(lane_blocks):
        o_ref[:, ls] = (jax.nn.gelu(yr_ref[:, ls]) * nat_scr[cb]).astype(o_ref.dtype)


def _rglru(xy, conv_w, conv_b, w_a, b_a, w_i, b_i, lam, batch, seq, ts=512, tc=512):
    n = batch * seq
    nt = seq // ts
    ncb = D_RNN // tc
    halo = (CONV_WIDTH - 1) * SUBLANES
    row = lambda v: v.reshape(1, D_RNN)
    vec_spec = pl.BlockSpec((1, tc), lambda b, c, t: (0, c))
    gate_spec = pl.BlockSpec((tc // RNN_BLOCK, RNN_BLOCK, RNN_BLOCK), lambda b, c, t: (c, 0, 0))
    return pl.pallas_call(
        _rglru_kernel,
        grid=(batch, ncb, nt),
        in_specs=[pl.BlockSpec((ts, tc), lambda b, c, t: (b * nt + t, c)),
                  pl.BlockSpec((ts, tc), lambda b, c, t: (b * nt + t, ncb + c)),
                  pl.BlockSpec((CONV_WIDTH, tc), lambda b, c, t: (0, c)),
                  vec_spec, gate_spec, gate_spec, vec_spec, vec_spec, vec_spec],
        out_specs=pl.BlockSpec((ts, tc), lambda b, c, t: (b * nt + t, c)),
        out_shape=jax.ShapeDtypeStruct((n, D_RNN), BF16),
        scratch_shapes=[pltpu.VMEM((halo + ts, tc), F32),
                        pltpu.VMEM((ts, tc), F32),
                        pltpu.VMEM((ts, tc), F32),
                        pltpu.VMEM((ts, tc), F32),
                        pltpu.VMEM((tc // LANES, ts, LANES), F32),
                        pltpu.VMEM((halo, tc), F32),
                        pltpu.VMEM((1, tc), F32)],
        name="rglru",
        compiler_params=_params(("parallel", "parallel", "arbitrary"), 32),
    )(xy, xy, conv_w, row(conv_b), w_a.astype(BF16), w_i.astype(BF16), row(b_a), row(b_i), row(lam))


def _moba_kernel(q_ref, k_ref, v_ref, km_ref, o_ref, bias_scr, s_scr, acc_scr, *, heads):
    j = pl.program_id(2)
    nblk = km_ref.shape[0]
    blk = MOBA_BLOCK
    c = (HEAD_DIM ** -0.5) * LOG2_E
    head_slices = [slice(h * HEAD_DIM, (h + 1) * HEAD_DIM) for h in range(heads)]

    def raw_scores(block):
        start = pl.multiple_of(block * blk, blk)
        return [lax.dot_general(k_ref[pl.ds(start, blk), hs], q_ref[:, hs], _NT,
                                preferred_element_type=F32) for hs in head_slices]

    def score_into(slot, block, raw=None):
        raw = raw_scores(block) if raw is None else raw
        for h in range(heads):
            s_scr[slot, h] = raw[h] + bias_scr[h, pl.ds(block, 1), :]

    ones_rows = jnp.ones((BF16_ROWS, blk), BF16)

    def attend(slot, mask, block, state):
        start = pl.multiple_of(block * blk, blk)
        half = blk // 2
        soft = []
        for h in range(heads):
            m = state[h]
            s_lo = mask(s_scr[slot, h, 0:half, :], 0)
            s_hi = mask(s_scr[slot, h, half:blk, :], half)
            m_blk = jnp.maximum(jnp.max(s_lo, axis=0, keepdims=True), jnp.max(s_hi, axis=0, keepdims=True))
            m_new = jnp.maximum(m, m_blk)
            alpha = jnp.exp2((m - m_new) * c)
            p = jnp.exp2((mask(s_scr[slot, h], 0) - m_new) * c)
            soft.append((m_new, alpha, p.astype(BF16)))
        out = []
        for h, hs in enumerate(head_slices):
            m_new, alpha, p = soft[h]
            vt = jnp.concatenate([v_ref[hs, pl.ds(start, blk)], ones_rows], axis=0)
            pv = jnp.dot(vt, p, preferred_element_type=F32)
            acc_scr[h] = alpha * acc_scr[h] + pv
            out.append(m_new)
        return tuple(out)

    no_mask = lambda s, row0: s

    raw0 = raw_scores(0)
    for h, hs in enumerate(head_slices):
        q = q_ref[:, hs]
        km = km_ref[:, hs]
        km_hi = km.astype(BF16)
        km_lo = (km - km_hi.astype(F32)).astype(BF16)
        gate = (lax.dot_general(km_hi, q, _NT, preferred_element_type=F32)
                + lax.dot_general(km_lo, q, _NT, preferred_element_type=F32))
        bidx = lax.broadcasted_iota(jnp.int32, gate.shape, 0)
        past = bidx < j
        g = jnp.where(past, gate, -jnp.inf)
        rank = jnp.zeros(gate.shape, jnp.int32)
        for other in range(nblk):
            go = g[other:other + 1, :]
            beats = jnp.where(go > g, 1, jnp.where(go == g, jnp.where(bidx > other, 1, 0), 0))
            rank = rank + beats
        bias_scr[h] = jnp.where(past, jnp.where(rank < MOBA_TOPK, 0.0, NEG_INF),
                                jnp.where(bidx == j, 0.0, NEG_INF))
    score_into(0, 0, raw0)

    def pair_body(t, state):
        first = 2 * t
        score_into(1, first + 1)
        state = attend(0, no_mask, first, state)
        score_into(0, first + 2)
        return attend(1, no_mask, first + 1, state)

    def odd_body(state):
        score_into(1, j)
        return attend(0, no_mask, j - 1, state)

    acc_scr[...] = jnp.zeros_like(acc_scr)
    init = tuple(jnp.full((1, blk), NEG_INF, F32) for _ in range(heads))
    state = lax.fori_loop(0, lax.shift_right_logical(j, 1), pair_body, init)
    odd = lax.bitwise_and(j, 1)
    state = lax.cond(odd == 1, odd_body, lambda st: st, state)

    def causal_mask(s, row0):
        kpos = row0 + lax.broadcasted_iota(jnp.int32, s.shape, 0)
        qpos = lax.broadcasted_iota(jnp.int32, s.shape, 1)
        return jnp.where(kpos <= qpos, s, NEG_INF)

    attend(odd, causal_mask, j, state)
    for h, hs in enumerate(head_slices):
        o_ref[:, hs] = (acc_scr[h, 0:HEAD_DIM, :] / acc_scr[h, HEAD_DIM:HEAD_DIM + 1, :]
                        ).T.astype(o_ref.dtype)


def _moba(qk, v_t, kmean, batch, seq, heads=4):
    n = batch * seq
    nblk = seq // MOBA_BLOCK
    groups = N_HEADS // heads
    width = heads * HEAD_DIM
    return pl.pallas_call(
        functools.partial(_moba_kernel, heads=heads),
        grid=(batch, groups, nblk),
        in_specs=[pl.BlockSpec((MOBA_BLOCK, width), lambda b, h, j: (b * nblk + j, h)),
                  pl.BlockSpec((seq, width), lambda b, h, j: (b, groups + h)),
                  pl.BlockSpec((width, seq), lambda b, h, j: (h, b)),
                  pl.BlockSpec((nblk, width), lambda b, h, j: (b, groups + h))],
        out_specs=pl.BlockSpec((MOBA_BLOCK, width), lambda b, h, j: (b * nblk + j, h)),
        out_shape=jax.ShapeDtypeStruct((n, D_ATTN), BF16),
        scratch_shapes=[pltpu.VMEM((heads, nblk, MOBA_BLOCK), F32),
                        pltpu.VMEM((2, heads, MOBA_BLOCK, MOBA_BLOCK), F32),
                        pltpu.VMEM((heads, HEAD_DIM + BF16_ROWS, MOBA_BLOCK), F32)],
        name="moba",
        compiler_params=_params(("parallel", "parallel", "arbitrary"), 56),
    )(qk, qk, v_t, kmean)


def _merge_kernel(gh_ref, o_ref, wr_ref, wa_ref, gr_ref, ga_ref, out_ref):
    rnn = jnp.dot(gh_ref[...], wr_ref[...], preferred_element_type=F32)
    att = jnp.dot(o_ref[...], wa_ref[...], preferred_element_type=F32)
    out_ref[...] = (gr_ref[...].astype(F32) * rnn + ga_ref[...].astype(F32) * att).astype(out_ref.dtype)


def _merge(gh, o, w_rnn, w_attn, gates, tm=1024, tn=512):
    m, k = gh.shape
    nj = D_MODEL // tn
    a_spec = pl.BlockSpec((tm, k), lambda i, j: (i, 0))
    w_spec = pl.BlockSpec((k, tn), lambda i, j: (0, j))
    return pl.pallas_call(
        _merge_kernel,
        grid=(m // tm, nj),
        in_specs=[a_spec, a_spec, w_spec, w_spec,
                  pl.BlockSpec((tm, tn), lambda i, j: (i, j)),
                  pl.BlockSpec((tm, tn), lambda i, j: (i, nj + j))],
        out_specs=pl.BlockSpec((tm, tn), lambda i, j: (i, j)),
        out_shape=jax.ShapeDtypeStruct((m, D_MODEL), BF16),
        name="merge",
        compiler_params=_params(("parallel", "arbitrary"), 48),
    )(gh, o, w_rnn, w_attn, gates, gates)


def _memkv_kernel(mem_ref, g_ref, w_ref, o_ref):
    hn = _rms(mem_ref[...], g_ref[...]).astype(BF16)
    o_ref[...] = jnp.dot(hn, w_ref[...], preferred_element_type=F32).astype(o_ref.dtype)


def _memkv(mem2d, g, w_kv, mem_len):
    m, d = mem2d.shape
    return pl.pallas_call(
        _memkv_kernel,
        grid=(m // mem_len,),
        in_specs=[pl.BlockSpec((mem_len, d), lambda i: (i, 0)),
                  pl.BlockSpec((1, d), lambda i: (0, 0)),
                  pl.BlockSpec((d, 2 * D_MEM), lambda i: (0, 0))],
        out_specs=pl.BlockSpec((mem_len, 2 * D_MEM), lambda i: (i, 0)),
        out_shape=jax.ShapeDtypeStruct((m, 2 * D_MEM), BF16),
        name="mem_kv",
        compiler_params=_params(("parallel",), 32),
    )(mem2d, g.reshape(1, d), w_kv)


def _xattn_kernel(x_ref, g_ref, wq_ref, kv_ref, wo_ref, o_ref):
    x = x_ref[...]
    hn = _rms(x, g_ref[...]).astype(BF16)
    q = jnp.dot(hn, wq_ref[...], preferred_element_type=F32).astype(BF16)
    scale = MEM_HEAD_DIM ** -0.5
    heads = []
    for hd in range(MEM_HEADS):
        sl = slice(hd * MEM_HEAD_DIM, (hd + 1) * MEM_HEAD_DIM)
        kh = kv_ref[:, sl]
        vh = kv_ref[:, D_MEM + hd * MEM_HEAD_DIM:D_MEM + (hd + 1) * MEM_HEAD_DIM]
        s = lax.dot_general(q[:, sl], kh, _NT, preferred_element_type=F32) * scale
        m = jnp.max(s, axis=-1, keepdims=True)
        p = jnp.exp(s - m)
        l = jnp.sum(p, axis=-1, keepdims=True)
        oh = jnp.dot(p.astype(BF16), vh, preferred_element_type=F32) / l
        heads.append(oh.astype(BF16))
    o_all = jnp.concatenate(heads, axis=-1)
    o_ref[...] = x + jnp.dot(o_all, wo_ref[...], preferred_element_type=F32)


def _xattn(x, g, w_q, kv, w_o, seq, mem_len, tm=512):
    m, d = x.shape
    per_batch = seq // tm
    return pl.pallas_call(
        _xattn_kernel,
        grid=(m // tm,),
        in_specs=[pl.BlockSpec((tm, d), lambda i: (i, 0)),
                  pl.BlockSpec((1, d), lambda i: (0, 0)),
                  pl.BlockSpec((d, D_MEM), lambda i: (0, 0)),
                  pl.BlockSpec((mem_len, 2 * D_MEM), lambda i: (i // per_batch, 0)),
                  pl.BlockSpec((D_MEM, d), lambda i: (0, 0))],
        out_specs=pl.BlockSpec((tm, d), lambda i: (i, 0)),
        out_shape=jax.ShapeDtypeStruct((m, d), F32),
        name="xattn",
        compiler_params=_params(("parallel",), 48),
    )(x, g.reshape(1, d), w_q, kv, w_o)


def _ffn_kernel(x_ref, g_ref, wg_ref, wu_ref, wd_ref, gf_ref, o_ref, h_scr):
    f = pl.program_id(1)

    @pl.when(f == 0)
    def _():
        h_scr[...] = _rms(x_ref[...], g_ref[...]).astype(BF16)
        o_ref[...] = jnp.zeros_like(o_ref)

    hn = h_scr[...]
    a = jnp.dot(hn, wg_ref[...], preferred_element_type=F32)
    b = jnp.dot(hn, wu_ref[...], preferred_element_type=F32)
    act = (jax.nn.silu(a) * b).astype(BF16)
    o_ref[...] += jnp.dot(act, wd_ref[...], preferred_element_type=F32)

    @pl.when(f == pl.num_programs(1) - 1)
    def _():
        o_ref[...] = _rms(x_ref[...] + o_ref[...], gf_ref[...])


def _ffn(x, g, w_gate, w_up, w_down, g_final, tm=1024, tf=512):
    m, d = x.shape
    d_ff = w_gate.shape[1]
    return pl.pallas_call(
        _ffn_kernel,
        grid=(m // tm, d_ff // tf),
        in_specs=[pl.BlockSpec((tm, d), lambda i, f: (i, 0)),
                  pl.BlockSpec((1, d), lambda i, f: (0, 0)),
                  pl.BlockSpec((d, tf), lambda i, f: (0, f)),
                  pl.BlockSpec((d, tf), lambda i, f: (0, f)),
                  pl.BlockSpec((tf, d), lambda i, f: (f, 0)),
                  pl.BlockSpec((1, d), lambda i, f: (0, 0))],
        out_specs=pl.BlockSpec((tm, d), lambda i, f: (i, 0)),
        out_shape=jax.ShapeDtypeStruct((m, d), F32),
        scratch_shapes=[pltpu.VMEM((tm, d), BF16)],
        name="ffn",
        compiler_params=_params(("parallel", "arbitrary"), 63),
    )(x, g.reshape(1, d), w_gate, w_up, w_down, g_final.reshape(1, d))


def _layer(x2d, mem2d, cosf, sinf, batch, seq, mem_len, p):
    n = x2d.shape[0]
    w_in = p["w_in"].astype(BF16)
    w_qk = _qk_weights(p["w_in"], 2 * D_RNN, 2 * D_ATTN)
    hn = _norm_bf16(x2d, p["norm_mix_g"])

    c0 = 0
    wide = 1024
    xy = _matmul(_mm_plain_kernel, "proj_xy", hn, w_in, col_off=c0, n_cols=2 * D_RNN, out_dtype=F32,
                 tn=wide)
    c0 += 2 * D_RNN
    tm, tn = 1024, wide
    rope_spec = pl.BlockSpec((tm, LANES), lambda i, j: (i, 0))
    qk, means = _matmul(
        _mm_rope_kernel, "proj_qk", hn, w_qk, col_off=0, n_cols=2 * D_ATTN, out_dtype=BF16,
        tm=tm, tn=tn, extra=(cosf, sinf), extra_specs=(rope_spec, rope_spec),
        extra_out_shape=(jax.ShapeDtypeStruct((n // tm, tm // MOBA_BLOCK, 2 * D_ATTN), F32),),
        extra_out_specs=(pl.BlockSpec((1, tm // MOBA_BLOCK, tn), lambda i, j: (i, 0, j)),))
    c0 += 2 * D_ATTN
    v_t = _matmul(_mm_transposed_kernel, "proj_v", hn, w_in, col_off=c0, n_cols=D_ATTN,
                  out_dtype=BF16, tn=wide, transposed_out=True)
    c0 += D_ATTN
    gates = _matmul(_mm_sigmoid_kernel, "proj_gates", hn, w_in, col_off=c0, n_cols=2 * D_MODEL,
                    out_dtype=BF16, tn=wide)

    gh = _rglru(xy, p["conv_w"], p["conv_b"], p["lru_w_a"], p["lru_b_a"], p["lru_w_i"],
                p["lru_b_i"], p["lru_lambda"], batch, seq)
    kmean = means.reshape(n // MOBA_BLOCK, 2 * D_ATTN)
    o = _moba(qk, v_t, kmean, batch, seq)

    merged = _merge(gh, o, p["w_rnn_proj"].astype(BF16), p["w_attn_proj"].astype(BF16), gates)
    x1 = _matmul(_mm_residual_kernel, "mix_out", merged, p["w_mix_out"].astype(BF16), col_off=0,
                 n_cols=D_MODEL, out_dtype=F32, tm=512, tn=D_MODEL,
                 extra=(x2d,), extra_specs=(pl.BlockSpec((512, D_MODEL), lambda i, j: (i, j)),))

    kv = _memkv(mem2d, p["norm_mem_g"], p["w_xkv"].astype(BF16), mem_len)
    x2 = _xattn(x1, p["norm_xq_g"], p["w_xq"].astype(BF16), kv, p["w_xo"].astype(BF16), seq, mem_len)
    return x2


def kernel(x, mem, positions, norm_mix_g, w_in, conv_w, conv_b, lru_w_a, lru_b_a, lru_w_i, lru_b_i,
           lru_lambda, w_rnn_proj, w_attn_proj, w_mix_out, norm_xq_g, norm_mem_g, w_xq, w_xkv, w_xo,
           norm_ffn_g, w_ffn_gate, w_ffn_up, w_ffn_down, norm_final_g):
    batch, seq, d = x.shape
    mem_len = mem.shape[1]
    assert w_in.shape[0] == 1, "only DEPTH == 1 is supported"
    x2d = x.reshape(batch * seq, d)
    mem2d = mem.reshape(batch * mem_len, d)
    cosf, sinf = _rope_tables(positions)
    p = dict(norm_mix_g=norm_mix_g[0], w_in=w_in[0], conv_w=conv_w[0], conv_b=conv_b[0],
             lru_w_a=lru_w_a[0], lru_b_a=lru_b_a[0], lru_w_i=lru_w_i[0], lru_b_i=lru_b_i[0],
             lru_lambda=lru_lambda[0], w_rnn_proj=w_rnn_proj[0], w_attn_proj=w_attn_proj[0],
             w_mix_out=w_mix_out[0], norm_xq_g=norm_xq_g[0], norm_mem_g=norm_mem_g[0],
             w_xq=w_xq[0], w_xkv=w_xkv[0], w_xo=w_xo[0])
    x2 = _layer(x2d, mem2d, cosf, sinf, batch, seq, mem_len, p)
    out = _ffn(x2, norm_ffn_g[0], w_ffn_gate[0].astype(BF16), w_ffn_up[0].astype(BF16),
               w_ffn_down[0].astype(BF16), norm_final_g)
    return out.reshape(batch, seq, d)
```

```python
import functools

import jax
import jax.numpy as jnp
from jax import lax
from jax.experimental import pallas as pl
from jax.experimental.pallas import tpu as pltpu

D_MODEL = 2048
N_HEADS = 16
HEAD_DIM = 128
D_ATTN = N_HEADS * HEAD_DIM
MOBA_BLOCK = 256
MOBA_TOPK = 3
ROPE_THETA = 500000.0
ROT_DIM = HEAD_DIM // 4
ROT_HALF = ROT_DIM // 2
D_RNN = 2048
N_RNN_BLOCKS = 16
RNN_BLOCK = D_RNN // N_RNN_BLOCKS
CONV_WIDTH = 4
LRU_C = 8.0
MEM_HEADS = 4
MEM_HEAD_DIM = 128
D_MEM = MEM_HEADS * MEM_HEAD_DIM
RMS_EPS = 1e-6
NEG_INF = -1e30
LOG2_E = 1.4426950408889634
SOFTMAX_LOG2_SCALE = (HEAD_DIM ** -0.5) * LOG2_E

LANES = 128
SUBLANES = 8
BF16_ROWS = 16
MXU_COLS = 2 * 256
ROT_PARTNER = LANES // 2
MIB = 1024 * 1024

BF16 = jnp.bfloat16
F32 = jnp.float32

_NT = (((1,), (1,)), ((), ()))
_TN = (((0,), (0,)), ((), ()))


def _params(semantics, vmem_mib):
    return pltpu.CompilerParams(dimension_semantics=semantics,
                                vmem_limit_bytes=vmem_mib * MIB)


def _rms(x, g):
    ms = jnp.mean(x * x, axis=-1, keepdims=True)
    return x * lax.rsqrt(ms + RMS_EPS) * g


def _norm_kernel(x_ref, g_ref, o_ref):
    o_ref[...] = _rms(x_ref[...], g_ref[...]).astype(o_ref.dtype)


def _norm_bf16(x, g, tm=512):
    m, d = x.shape
    return pl.pallas_call(
        _norm_kernel,
        grid=(m // tm,),
        in_specs=[pl.BlockSpec((tm, d), lambda i: (i, 0)),
                  pl.BlockSpec((1, d), lambda i: (0, 0))],
        out_specs=pl.BlockSpec((tm, d), lambda i: (i, 0)),
        out_shape=jax.ShapeDtypeStruct((m, d), BF16),
        name="norm_mix",
        compiler_params=_params(("parallel",), 32),
    )(x, g.reshape(1, d))


def _rope_table_kernel(pos_ref, invf_ref, cos_ref, sin_ref):
    ang = pos_ref[...] * invf_ref[...]
    lane = lax.broadcasted_iota(jnp.int32, ang.shape, 1)
    s = jnp.sin(ang)
    cos_ref[...] = jnp.cos(ang)
    sin_ref[...] = jnp.where(lane < ROT_PARTNER, -s, s)


def _rope_head_layout():
    split = ROT_DIM + ROT_PARTNER - ROT_HALF
    return [(0, ROT_HALF), (ROT_DIM, split), (ROT_HALF, ROT_DIM), (split, HEAD_DIM)]


def _qk_weight_kernel(w_ref, o_ref):
    lane = lax.broadcasted_iota(jnp.int32, (w_ref.shape[0], HEAD_DIM), 1)
    for h0 in range(0, w_ref.shape[1], HEAD_DIM):
        x = w_ref[:, h0:h0 + HEAD_DIM]
        out, at = x, 0
        for start, stop in _rope_head_layout():
            if start != at:
                moved = pltpu.roll(x, (at - start) % HEAD_DIM, 1)
                out = jnp.where((lane >= at) & (lane < at + stop - start), moved, out)
            at += stop - start
        o_ref[:, h0:h0 + HEAD_DIM] = out.astype(o_ref.dtype)


def _qk_weights(w_in, col0, n_cols, tr=256, tc=1024):
    k = w_in.shape[0]
    off = col0 // tc
    return pl.pallas_call(
        _qk_weight_kernel,
        grid=(k // tr, n_cols // tc),
        in_specs=[pl.BlockSpec((tr, tc), lambda i, j: (i, j + off))],
        out_specs=pl.BlockSpec((tr, tc), lambda i, j: (i, j)),
        out_shape=jax.ShapeDtypeStruct((k, n_cols), BF16),
        name="qk_weights",
        compiler_params=_params(("parallel", "parallel"), 32),
    )(w_in)


def _rope_tables(positions, tr=1024):
    n = positions.size
    pos = positions.astype(F32).reshape(n, 1)
    inv_freq = jnp.power(ROPE_THETA, -jnp.arange(ROT_HALF, dtype=F32) / ROT_HALF)
    gap = jnp.zeros((ROT_PARTNER - ROT_HALF,), F32)
    invf = jnp.concatenate([inv_freq, gap, inv_freq, gap]).reshape(1, LANES)
    return pl.pallas_call(
        _rope_table_kernel,
        grid=(n // tr,),
        in_specs=[pl.BlockSpec((tr, 1), lambda i: (i, 0)),
                  pl.BlockSpec((1, LANES), lambda i: (0, 0))],
        out_specs=[pl.BlockSpec((tr, LANES), lambda i: (i, 0)),
                   pl.BlockSpec((tr, LANES), lambda i: (i, 0))],
        out_shape=[jax.ShapeDtypeStruct((n, LANES), F32),
                   jax.ShapeDtypeStruct((n, LANES), F32)],
        name="rope_tables",
        compiler_params=_params(("parallel",), 32),
    )(pos, invf)


def _mm_plain_kernel(a_ref, w_ref, o_ref):
    acc = jnp.dot(a_ref[...], w_ref[...], preferred_element_type=F32)
    o_ref[...] = acc.astype(o_ref.dtype)


def _mm_transposed_kernel(a_ref, w_ref, o_ref):
    for c0 in range(0, w_ref.shape[1], MXU_COLS):
        acc = jnp.dot(a_ref[...], w_ref[:, c0:c0 + MXU_COLS], preferred_element_type=F32)
        o_ref[c0:c0 + MXU_COLS, :] = acc.T.astype(o_ref.dtype)


def _mm_sigmoid_kernel(a_ref, w_ref, o_ref):
    for c0 in range(0, w_ref.shape[1], MXU_COLS):
        acc = jnp.dot(a_ref[...], w_ref[:, c0:c0 + MXU_COLS], preferred_element_type=F32)
        o_ref[:, c0:c0 + MXU_COLS] = (0.5 * jnp.tanh(0.5 * acc) + 0.5).astype(o_ref.dtype)


def _mm_residual_kernel(a_ref, w_ref, r_ref, o_ref):
    acc = jnp.dot(a_ref[...], w_ref[...], preferred_element_type=F32)
    o_ref[...] = r_ref[...] + acc


def _mm_rope_kernel(a_ref, w_ref, cos_ref, sin_ref, o_ref, mean_ref):
    tm, tn = o_ref.shape
    q_scale = jnp.where(pl.program_id(1) < D_ATTN // tn, SOFTMAX_LOG2_SCALE, 1.0)
    cosf = cos_ref[...] * q_scale
    sinf = sin_ref[...] * q_scale
    for c0 in range(0, tn, MXU_COLS):
        acc = jnp.dot(a_ref[...], w_ref[:, c0:c0 + MXU_COLS], preferred_element_type=F32)
        for h0 in range(0, MXU_COLS, HEAD_DIM):
            a = acc[:, h0:h0 + HEAD_DIM]
            r = a * cosf + pltpu.roll(a, ROT_PARTNER, 1) * sinf
            cols = slice(c0 + h0, c0 + h0 + HEAD_DIM)
            o_ref[:, cols] = r.astype(o_ref.dtype)
            mean_ref[0, :, cols] = jnp.mean(r.reshape(tm // MOBA_BLOCK, MOBA_BLOCK, HEAD_DIM), axis=1)


def _matmul(kernel, name, a, w, *, col_off, n_cols, out_dtype, tm=1024, tn=512,
            extra=(), extra_specs=(), extra_out_shape=(), extra_out_specs=(), vmem_mib=48,
            transposed_out=False):
    m, k = a.shape
    off = col_off // tn
    if transposed_out:
        main_shape, main_spec = (n_cols, m), pl.BlockSpec((tn, tm), lambda i, j: (j, i))
    else:
        main_shape, main_spec = (m, n_cols), pl.BlockSpec((tm, tn), lambda i, j: (i, j))
    out_shape = [jax.ShapeDtypeStruct(main_shape, out_dtype)] + list(extra_out_shape)
    out_specs = [main_spec] + list(extra_out_specs)
    res = pl.pallas_call(
        kernel,
        grid=(m // tm, n_cols // tn),
        in_specs=[pl.BlockSpec((tm, k), lambda i, j: (i, 0)),
                  pl.BlockSpec((k, tn), lambda i, j: (0, j + off))] + list(extra_specs),
        out_specs=out_specs,
        out_shape=out_shape,
        name=name,
        compiler_params=_params(("parallel", "arbitrary"), vmem_mib),
    )(a, w, *extra)
    return res if extra_out_shape else res[0]


def _rglru_kernel(xr_ref, yr_ref, cw_ref, cb_ref, wa_ref, wi_ref, ba_ref, bi_ref, lam_ref,
                  o_ref, xbuf, a_scr, u_scr, h_scr, nat_scr, tail_scr, hc_scr):
    ts, tc = xr_ref.shape
    nv = ts // SUBLANES
    halo = (CONV_WIDTH - 1) * SUBLANES
    t = pl.program_id(2)

    def grp(g):
        return slice(halo + g * SUBLANES, halo + (g + 1) * SUBLANES)

    @pl.when(t == 0)
    def _():
        tail_scr[...] = jnp.zeros_like(tail_scr)
        hc_scr[...] = jnp.zeros_like(hc_scr)

    lane_blocks = [slice(cb * LANES, (cb + 1) * LANES) for cb in range(tc // LANES)]
    for cb, ls in enumerate(lane_blocks):
        nat_scr[cb] = xr_ref[:, ls]
    for v in range(nv):
        for cb, ls in enumerate(lane_blocks):
            xbuf[grp(v), ls] = nat_scr[cb, pl.ds(v, SUBLANES, stride=nv), :]
    sub = lax.broadcasted_iota(jnp.int32, (SUBLANES, tc), 0)
    for d in range(1, CONV_WIDTH):
        keep = slice((CONV_WIDTH - 1 - d) * SUBLANES, (CONV_WIDTH - d) * SUBLANES)
        cur = xbuf[grp(nv - d), :]
        xbuf[grp(-d), :] = pltpu.roll(jnp.where(sub == SUBLANES - 1, tail_scr[keep, :], cur), 1, 0)
        tail_scr[keep, :] = cur

    xc = jnp.zeros((ts, tc), F32) + cb_ref[...]
    for kk in range(CONV_WIDTH):
        start = halo - (CONV_WIDTH - 1 - kk) * SUBLANES
        xc = xc + cw_ref[kk:kk + 1, :] * xbuf[start:start + ts, :]

    half_a = (-0.5 * LRU_C) * jax.nn.softplus(-lam_ref[...])
    for nb in range(tc // RNN_BLOCK):
        sl = slice(nb * RNN_BLOCK, (nb + 1) * RNN_BLOCK)
        xb = xc[:, sl]
        xb16 = xb.astype(BF16)
        zr = jnp.dot(xb16, wa_ref[nb], preferred_element_type=F32) + ba_ref[:, sl]
        zi = jnp.dot(xb16, wi_ref[nb], preferred_element_type=F32) + bi_ref[:, sl]
        log_a = half_a[:, sl] * jnp.tanh(0.5 * zr) + half_a[:, sl]
        ig = 0.5 * jnp.tanh(0.5 * zi) + 0.5
        a_scr[:, sl] = jnp.exp(log_a)
        th = jnp.tanh(log_a)
        y = -2.0 * th / (1.0 - th)
        u_scr[:, sl] = jnp.where(y > 0.0, y * lax.rsqrt(y), 0.0) * (ig * xb)

    h_end = jnp.zeros((SUBLANES, tc), F32)
    p_end = jnp.ones((SUBLANES, tc), F32)
    for v in range(nv):
        rows = slice(v * SUBLANES, (v + 1) * SUBLANES)
        a = a_scr[rows, :]
        h_end = a * h_end + u_scr[rows, :]
        p_end = a * p_end
        h_scr[rows, :] = h_end
        a_scr[rows, :] = p_end
    h_in = hc_scr[...]
    entering = []
    for s in range(SUBLANES):
        entering.append(h_in)
        h_in = h_end[s:s + 1, :] + p_end[s:s + 1, :] * h_in
    hc_scr[...] = h_in
    h_enter = jnp.concatenate(entering, axis=0)

    for v in range(nv):
        rows = slice(v * SUBLANES, (v + 1) * SUBLANES)
        h_v = h_scr[rows, :] + a_scr[rows, :] * h_enter
        for cb, ls in enumerate(lane_blocks):
            nat_scr[cb, pl.ds(v, SUBLANES, stride=nv), :] = h_v[:, ls]
    for cb, ls in enumerate(lane_blocks):
        o_ref[:, ls] = (jax.nn.gelu(yr_ref[:, ls]) * nat_scr[cb]).astype(o_ref.dtype)


def _rglru(xy, conv_w, conv_b, w_a, b_a, w_i, b_i, lam, batch, seq, ts=512, tc=512):
    n = batch * seq
    nt = seq // ts
    ncb = D_RNN // tc
    halo = (CONV_WIDTH - 1) * SUBLANES
    row = lambda v: v.reshape(1, D_RNN)
    vec_spec = pl.BlockSpec((1, tc), lambda b, c, t: (0, c))
    gate_spec = pl.BlockSpec((tc // RNN_BLOCK, RNN_BLOCK, RNN_BLOCK), lambda b, c, t: (c, 0, 0))
    return pl.pallas_call(
        _rglru_kernel,
        grid=(batch, ncb, nt),
        in_specs=[pl.BlockSpec((ts, tc), lambda b, c, t: (b * nt + t, c)),
                  pl.BlockSpec((ts, tc), lambda b, c, t: (b * nt + t, ncb + c)),
                  pl.BlockSpec((CONV_WIDTH, tc), lambda b, c, t: (0, c)),
                  vec_spec, gate_spec, gate_spec, vec_spec, vec_spec, vec_spec],
        out_specs=pl.BlockSpec((ts, tc), lambda b, c, t: (b * nt + t, c)),
        out_shape=jax.ShapeDtypeStruct((n, D_RNN), BF16),
        scratch_shapes=[pltpu.VMEM((halo + ts, tc), F32),
                        pltpu.VMEM((ts, tc), F32),
                        pltpu.VMEM((ts, tc), F32),
                        pltpu.VMEM((ts, tc), F32),
                        pltpu.VMEM((tc // LANES, ts, LANES), F32),
                        pltpu.VMEM((halo, tc), F32),
                        pltpu.VMEM((1, tc), F32)],
        name="rglru",
        compiler_params=_params(("parallel", "parallel", "arbitrary"), 32),
    )(xy, xy, conv_w, row(conv_b), w_a.astype(BF16), w_i.astype(BF16), row(b_a), row(b_i), row(lam))


def _moba_kernel(q_ref, k_ref, v_ref, km_ref, o_ref, bias_scr, s_scr, acc_scr, *, heads):
    j = pl.program_id(2)
    nblk = km_ref.shape[0]
    blk = MOBA_BLOCK
    head_slices = [slice(h * HEAD_DIM, (h + 1) * HEAD_DIM) for h in range(heads)]

    def score_into(slot, block):
        start = pl.multiple_of(block * blk, blk)
        for h, hs in enumerate(head_slices):
            s_scr[slot, h] = lax.dot_general(k_ref[pl.ds(start, blk), hs], q_ref[:, hs], _NT,
                                             preferred_element_type=F32)

    ones_rows = jnp.ones((BF16_ROWS, blk), BF16)

    def attend(slot, mask, block, state):
        start = pl.multiple_of(block * blk, blk)
        half = blk // 2
        soft = []
        for h in range(heads):
            m = state[h]
            unselected = bias_scr[h, pl.ds(block, 1), :] < 0.0
            s_lo = mask(s_scr[slot, h, 0:half, :], 0)
            s_hi = mask(s_scr[slot, h, half:blk, :], half)
            m_blk = jnp.maximum(jnp.max(s_lo, axis=0, keepdims=True), jnp.max(s_hi, axis=0, keepdims=True))
            m_new = jnp.where(unselected, m, jnp.maximum(m, m_blk))
            alpha = jnp.exp2(m - m_new)
            p = jnp.exp2(mask(s_scr[slot, h], 0) - jnp.where(unselected, -NEG_INF, m_new))
            soft.append((m_new, alpha, p.astype(BF16)))
        out = []
        for h, hs in enumerate(head_slices):
            m_new, alpha, p = soft[h]
            vt = jnp.concatenate([v_ref[hs, pl.ds(start, blk)], ones_rows], axis=0)
            pv = jnp.dot(vt, p, preferred_element_type=F32)
            acc_scr[h] = alpha * acc_scr[h] + pv
            out.append(m_new)
        return tuple(out)

    no_mask = lambda s, row0: s

    gates = []
    for hs in head_slices:
        q = q_ref[:, hs]
        km = km_ref[:, hs]
        km_hi = km.astype(BF16)
        km_lo = (km - km_hi.astype(F32)).astype(BF16)
        gates.append(lax.dot_general(km_hi, q, _NT, preferred_element_type=F32)
                     + lax.dot_general(km_lo, q, _NT, preferred_element_type=F32))
    score_into(0, 0)
    for h, gate in enumerate(gates):
        bidx = lax.broadcasted_iota(jnp.int32, gate.shape, 0)
        past = bidx < j
        g = jnp.where(past, gate, -jnp.inf)
        rank = jnp.zeros(gate.shape, jnp.int32)
        for other in range(nblk):
            go = g[other:other + 1, :]
            beats = jnp.where(go > g, 1, jnp.where(go == g, jnp.where(bidx > other, 1, 0), 0))
            rank = rank + beats
        bias_scr[h] = jnp.where(past, jnp.where(rank < MOBA_TOPK, 0.0, NEG_INF),
                                jnp.where(bidx == j, 0.0, NEG_INF))

    def pair_body(t, state):
        first = 2 * t
        score_into(1, first + 1)
        state = attend(0, no_mask, first, state)
        score_into(0, first + 2)
        return attend(1, no_mask, first + 1, state)

    def odd_body(state):
        score_into(1, j)
        return attend(0, no_mask, j - 1, state)

    acc_scr[...] = jnp.zeros_like(acc_scr)
    init = tuple(jnp.full((1, blk), NEG_INF, F32) for _ in range(heads))
    state = lax.fori_loop(0, lax.shift_right_logical(j, 1), pair_body, init)
    odd = lax.bitwise_and(j, 1)
    state = lax.cond(odd == 1, odd_body, lambda st: st, state)

    def causal_mask(s, row0):
        kpos = row0 + lax.broadcasted_iota(jnp.int32, s.shape, 0)
        qpos = lax.broadcasted_iota(jnp.int32, s.shape, 1)
        return jnp.where(kpos <= qpos, s, NEG_INF)

    attend(odd, causal_mask, j, state)
    for h, hs in enumerate(head_slices):
        o_ref[:, hs] = (acc_scr[h, 0:HEAD_DIM, :] / acc_scr[h, HEAD_DIM:HEAD_DIM + 1, :]
                        ).T.astype(o_ref.dtype)


def _moba(qk, v_t, kmean, batch, seq, heads=4):
    n = batch * seq
    nblk = seq // MOBA_BLOCK
    groups = N_HEADS // heads
    width = heads * HEAD_DIM
    return pl.pallas_call(
        functools.partial(_moba_kernel, heads=heads),
        grid=(batch, groups, nblk),
        in_specs=[pl.BlockSpec((MOBA_BLOCK, width), lambda b, h, j: (b * nblk + j, h)),
                  pl.BlockSpec((seq, width), lambda b, h, j: (b, groups + h)),
                  pl.BlockSpec((width, seq), lambda b, h, j: (h, b)),
                  pl.BlockSpec((nblk, width), lambda b, h, j: (b, groups + h))],
        out_specs=pl.BlockSpec((MOBA_BLOCK, width), lambda b, h, j: (b * nblk + j, h)),
        out_shape=jax.ShapeDtypeStruct((n, D_ATTN), BF16),
        scratch_shapes=[pltpu.VMEM((heads, nblk, MOBA_BLOCK), F32),
                        pltpu.VMEM((2, heads, MOBA_BLOCK, MOBA_BLOCK), F32),
                        pltpu.VMEM((heads, HEAD_DIM + BF16_ROWS, MOBA_BLOCK), F32)],
        name="moba",
        compiler_params=_params(("parallel", "parallel", "arbitrary"), 56),
    )(qk, qk, v_t, kmean)


def _merge_kernel(gh_ref, o_ref, wr_ref, wa_ref, gr_ref, ga_ref, out_ref):
    rnn = jnp.dot(gh_ref[...], wr_ref[...], preferred_element_type=F32)
    att = jnp.dot(o_ref[...], wa_ref[...], preferred_element_type=F32)
    out_ref[...] = (gr_ref[...].astype(F32) * rnn + ga_ref[...].astype(F32) * att).astype(out_ref.dtype)


def _merge(gh, o, w_rnn, w_attn, gates, tm=1024, tn=512):
    m, k = gh.shape
    nj = D_MODEL // tn
    a_spec = pl.BlockSpec((tm, k), lambda i, j: (i, 0))
    w_spec = pl.BlockSpec((k, tn), lambda i, j: (0, j))
    return pl.pallas_call(
        _merge_kernel,
        grid=(m // tm, nj),
        in_specs=[a_spec, a_spec, w_spec, w_spec,
                  pl.BlockSpec((tm, tn), lambda i, j: (i, j)),
                  pl.BlockSpec((tm, tn), lambda i, j: (i, nj + j))],
        out_specs=pl.BlockSpec((tm, tn), lambda i, j: (i, j)),
        out_shape=jax.ShapeDtypeStruct((m, D_MODEL), BF16),
        name="merge",
        compiler_params=_params(("parallel", "arbitrary"), 48),
    )(gh, o, w_rnn, w_attn, gates, gates)


def _memkv_kernel(mem_ref, g_ref, w_ref, o_ref):
    hn = _rms(mem_ref[...], g_ref[...]).astype(BF16)
    o_ref[...] = jnp.dot(hn, w_ref[...], preferred_element_type=F32).astype(o_ref.dtype)


def _memkv(mem2d, g, w_kv, mem_len):
    m, d = mem2d.shape
    return pl.pallas_call(
        _memkv_kernel,
        grid=(m // mem_len,),
        in_specs=[pl.BlockSpec((mem_len, d), lambda i: (i, 0)),
                  pl.BlockSpec((1, d), lambda i: (0, 0)),
                  pl.BlockSpec((d, 2 * D_MEM), lambda i: (0, 0))],
        out_specs=pl.BlockSpec((mem_len, 2 * D_MEM), lambda i: (i, 0)),
        out_shape=jax.ShapeDtypeStruct((m, 2 * D_MEM), BF16),
        name="mem_kv",
        compiler_params=_params(("parallel",), 32),
    )(mem2d, g.reshape(1, d), w_kv)


def _xattn_kernel(x_ref, g_ref, wq_ref, kv_ref, wo_ref, o_ref):
    x = x_ref[...]
    hn = _rms(x, g_ref[...]).astype(BF16)
    q = jnp.dot(hn, wq_ref[...], preferred_element_type=F32).astype(BF16)
    scale = MEM_HEAD_DIM ** -0.5
    heads = []
    for hd in range(MEM_HEADS):
        sl = slice(hd * MEM_HEAD_DIM, (hd + 1) * MEM_HEAD_DIM)
        kh = kv_ref[:, sl]
        vh = kv_ref[:, D_MEM + hd * MEM_HEAD_DIM:D_MEM + (hd + 1) * MEM_HEAD_DIM]
        s = lax.dot_general(q[:, sl], kh, _NT, preferred_element_type=F32) * scale
        m = jnp.max(s, axis=-1, keepdims=True)
        p = jnp.exp(s - m)
        l = jnp.sum(p, axis=-1, keepdims=True)
        oh = jnp.dot(p.astype(BF16), vh, preferred_element_type=F32) / l
        heads.append(oh.astype(BF16))
    o_all = jnp.concatenate(heads, axis=-1)
    o_ref[...] = x + jnp.dot(o_all, wo_ref[...], preferred_element_type=F32)


def _xattn(x, g, w_q, kv, w_o, seq, mem_len, tm=512):
    m, d = x.shape
    per_batch = seq // tm
    return pl.pallas_call(
        _xattn_kernel,
        grid=(m // tm,),
        in_specs=[pl.BlockSpec((tm, d), lambda i: (i, 0)),
                  pl.BlockSpec((1, d), lambda i: (0, 0)),
                  pl.BlockSpec((d, D_MEM), lambda i: (0, 0)),
                  pl.BlockSpec((mem_len, 2 * D_MEM), lambda i: (i // per_batch, 0)),
                  pl.BlockSpec((D_MEM, d), lambda i: (0, 0))],
        out_specs=pl.BlockSpec((tm, d), lambda i: (i, 0)),
        out_shape=jax.ShapeDtypeStruct((m, d), F32),
        name="xattn",
        compiler_params=_params(("parallel",), 48),
    )(x, g.reshape(1, d), w_q, kv, w_o)


def _ffn_kernel(x_ref, g_ref, wg_ref, wu_ref, wd_ref, gf_ref, o_ref, h_scr):
    f = pl.program_id(1)

    @pl.when(f == 0)
    def _():
        h_scr[...] = _rms(x_ref[...], g_ref[...]).astype(BF16)
        o_ref[...] = jnp.zeros_like(o_ref)

    hn = h_scr[...]
    a = jnp.dot(hn, wg_ref[...], preferred_element_type=F32)
    b = jnp.dot(hn, wu_ref[...], preferred_element_type=F32)
    act = (jax.nn.silu(a) * b).astype(BF16)
    o_ref[...] += jnp.dot(act, wd_ref[...], preferred_element_type=F32)

    @pl.when(f == pl.num_programs(1) - 1)
    def _():
        o_ref[...] = _rms(x_ref[...] + o_ref[...], gf_ref[...])


def _ffn(x, g, w_gate, w_up, w_down, g_final, tm=1024, tf=512):
    m, d = x.shape
    d_ff = w_gate.shape[1]
    return pl.pallas_call(
        _ffn_kernel,
        grid=(m // tm, d_ff // tf),
        in_specs=[pl.BlockSpec((tm, d), lambda i, f: (i, 0)),
                  pl.BlockSpec((1, d), lambda i, f: (0, 0)),
                  pl.BlockSpec((d, tf), lambda i, f: (0, f)),
                  pl.BlockSpec((d, tf), lambda i, f: (0, f)),
                  pl.BlockSpec((tf, d), lambda i, f: (f, 0)),
                  pl.BlockSpec((1, d), lambda i, f: (0, 0))],
        out_specs=pl.BlockSpec((tm, d), lambda i, f: (i, 0)),
        out_shape=jax.ShapeDtypeStruct((m, d), F32),
        scratch_shapes=[pltpu.VMEM((tm, d), BF16)],
        name="ffn",
        compiler_params=_params(("parallel", "arbitrary"), 63),
    )(x, g.reshape(1, d), w_gate, w_up, w_down, g_final.reshape(1, d))


def _layer(x2d, mem2d, cosf, sinf, batch, seq, mem_len, p):
    n = x2d.shape[0]
    w_in = p["w_in"].astype(BF16)
    w_qk = _qk_weights(p["w_in"], 2 * D_RNN, 2 * D_ATTN)
    hn = _norm_bf16(x2d, p["norm_mix_g"])

    c0 = 0
    wide = 1024
    xy = _matmul(_mm_plain_kernel, "proj_xy", hn, w_in, col_off=c0, n_cols=2 * D_RNN, out_dtype=F32,
                 tn=wide)
    c0 += 2 * D_RNN
    tm, tn = 1024, wide
    rope_spec = pl.BlockSpec((tm, LANES), lambda i, j: (i, 0))
    qk, means = _matmul(
        _mm_rope_kernel, "proj_qk", hn, w_qk, col_off=0, n_cols=2 * D_ATTN, out_dtype=BF16,
        tm=tm, tn=tn, extra=(cosf, sinf), extra_specs=(rope_spec, rope_spec),
        extra_out_shape=(jax.ShapeDtypeStruct((n // tm, tm // MOBA_BLOCK, 2 * D_ATTN), F32),),
        extra_out_specs=(pl.BlockSpec((1, tm // MOBA_BLOCK, tn), lambda i, j: (i, 0, j)),))
    c0 += 2 * D_ATTN
    v_t = _matmul(_mm_transposed_kernel, "proj_v", hn, w_in, col_off=c0, n_cols=D_ATTN,
                  out_dtype=BF16, tn=wide, transposed_out=True)
    c0 += D_ATTN
    gates = _matmul(_mm_sigmoid_kernel, "proj_gates", hn, w_in, col_off=c0, n_cols=2 * D_MODEL,
                    out_dtype=BF16, tn=wide)

    gh = _rglru(xy, p["conv_w"], p["conv_b"], p["lru_w_a"], p["lru_b_a"], p["lru_w_i"],
                p["lru_b_i"], p["lru_lambda"], batch, seq)
    kmean = means.reshape(n // MOBA_BLOCK, 2 * D_ATTN)
    o = _moba(qk, v_t, kmean, batch, seq)

    merged = _merge(gh, o, p["w_rnn_proj"].astype(BF16), p["w_attn_proj"].astype(BF16), gates)
    x1 = _matmul(_mm_residual_kernel, "mix_out", merged, p["w_mix_out"].astype(BF16), col_off=0,
                 n_cols=D_MODEL, out_dtype=F32, tm=512, tn=D_MODEL,
                 extra=(x2d,), extra_specs=(pl.BlockSpec((512, D_MODEL), lambda i, j: (i, j)),))

    kv = _memkv(mem2d, p["norm_mem_g"], p["w_xkv"].astype(BF16), mem_len)
    x2 = _xattn(x1, p["norm_xq_g"], p["w_xq"].astype(BF16), kv, p["w_xo"].astype(BF16), seq, mem_len)
    return x2


def kernel(x, mem, positions, norm_mix_g, w_in, conv_w, conv_b, lru_w_a, lru_b_a, lru_w_i, lru_b_i,
           lru_lambda, w_rnn_proj, w_attn_proj, w_mix_out, norm_xq_g, norm_mem_g, w_xq, w_xkv, w_xo,
           norm_ffn_g, w_ffn_gate, w_ffn_up, w_ffn_down, norm_final_g):
    batch, seq, d = x.shape
    mem_len = mem.shape[1]
    assert w_in.shape[0] == 1, "only DEPTH == 1 is supported"
    x2d = x.reshape(batch * seq, d)
    mem2d = mem.reshape(batch * mem_len, d)
    cosf, sinf = _rope_tables(positions)
    p = dict(norm_mix_g=norm_mix_g[0], w_in=w_in[0], conv_w=conv_w[0], conv_b=conv_b[0],
             lru_w_a=lru_w_a[0], lru_b_a=lru_b_a[0], lru_w_i=lru_w_i[0], lru_b_i=lru_b_i[0],
             lru_lambda=lru_lambda[0], w_rnn_proj=w_rnn_proj[0], w_attn_proj=w_attn_proj[0],
             w_mix_out=w_mix_out[0], norm_xq_g=norm_xq_g[0], norm_mem_g=norm_mem_g[0],
             w_xq=w_xq[0], w_xkv=w_xkv[0], w_xo=w_xo[0])
    x2 = _layer(x2d, mem2d, cosf, sinf, batch, seq, mem_len, p)
    out = _ffn(x2, norm_ffn_g[0], w_ffn_gate[0].astype(BF16), w_ffn_up[0].astype(BF16),
               w_ffn_down[0].astype(BF16), norm_final_g)
    return out.reshape(batch, seq, d)
```

```python
import functools

import jax
import jax.numpy as jnp
from jax import lax
from jax.experimental import pallas as pl
from jax.experimental.pallas import tpu as pltpu

D_MODEL = 2048
N_HEADS = 16
HEAD_DIM = 128
D_ATTN = N_HEADS * HEAD_DIM
MOBA_BLOCK = 256
MOBA_TOPK = 3
ROPE_THETA = 500000.0
ROT_DIM = HEAD_DIM // 4
ROT_HALF = ROT_DIM // 2
D_RNN = 2048
N_RNN_BLOCKS = 16
RNN_BLOCK = D_RNN // N_RNN_BLOCKS
CONV_WIDTH = 4
LRU_C = 8.0
MEM_HEADS = 4
MEM_HEAD_DIM = 128
D_MEM = MEM_HEADS * MEM_HEAD_DIM
RMS_EPS = 1e-6
NEG_INF = -1e30
LOG2_E = 1.4426950408889634
SOFTMAX_LOG2_SCALE = (HEAD_DIM ** -0.5) * LOG2_E

LANES = 128
SUBLANES = 8
BF16_ROWS = 16
MXU_COLS = 2 * 256
ROT_PARTNER = LANES // 2
MIB = 1024 * 1024

BF16 = jnp.bfloat16
F32 = jnp.float32

_NT = (((1,), (1,)), ((), ()))
_TN = (((0,), (0,)), ((), ()))


def _params(semantics, vmem_mib):
    return pltpu.CompilerParams(dimension_semantics=semantics,
                                vmem_limit_bytes=vmem_mib * MIB)


def _rms(x, g):
    ms = jnp.mean(x * x, axis=-1, keepdims=True)
    return x * lax.rsqrt(ms + RMS_EPS) * g


def _norm_kernel(x_ref, g_ref, o_ref):
    o_ref[...] = _rms(x_ref[...], g_ref[...]).astype(o_ref.dtype)


def _norm_bf16(x, g, tm=512):
    m, d = x.shape
    return pl.pallas_call(
        _norm_kernel,
        grid=(m // tm,),
        in_specs=[pl.BlockSpec((tm, d), lambda i: (i, 0)),
                  pl.BlockSpec((1, d), lambda i: (0, 0))],
        out_specs=pl.BlockSpec((tm, d), lambda i: (i, 0)),
        out_shape=jax.ShapeDtypeStruct((m, d), BF16),
        name="norm_mix",
        compiler_params=_params(("parallel",), 32),
    )(x, g.reshape(1, d))


def _rope_table_kernel(pos_ref, invf_ref, cos_ref, sin_ref):
    ang = pos_ref[...] * invf_ref[...]
    lane = lax.broadcasted_iota(jnp.int32, ang.shape, 1)
    s = jnp.sin(ang)
    cos_ref[...] = jnp.cos(ang)
    sin_ref[...] = jnp.where(lane < ROT_PARTNER, -s, s)


def _rope_head_layout():
    split = ROT_DIM + ROT_PARTNER - ROT_HALF
    return [(0, ROT_HALF), (ROT_DIM, split), (ROT_HALF, ROT_DIM), (split, HEAD_DIM)]


def _cast_weights(w_ref, w_scr):
    @pl.when(pl.program_id(1) == 0)
    def _():
        w_scr[...] = w_ref[...].astype(w_scr.dtype)


def _cast_qk_weights(w_ref, w_scr):
    @pl.when(pl.program_id(1) == 0)
    def _():
        rows = 256
        lane = lax.broadcasted_iota(jnp.int32, (rows, HEAD_DIM), 1)
        for r0 in range(0, w_ref.shape[0], rows):
            for h0 in range(0, w_ref.shape[1], HEAD_DIM):
                x = w_ref[r0:r0 + rows, h0:h0 + HEAD_DIM]
                out, at = x, 0
                for start, stop in _rope_head_layout():
                    if start != at:
                        moved = pltpu.roll(x, (at - start) % HEAD_DIM, 1)
                        out = jnp.where((lane >= at) & (lane < at + stop - start), moved, out)
                    at += stop - start
                w_scr[r0:r0 + rows, h0:h0 + HEAD_DIM] = out.astype(w_scr.dtype)


def _rope_tables(positions, tr=1024):
    n = positions.size
    pos = positions.astype(F32).reshape(n, 1)
    inv_freq = jnp.power(ROPE_THETA, -jnp.arange(ROT_HALF, dtype=F32) / ROT_HALF)
    gap = jnp.zeros((ROT_PARTNER - ROT_HALF,), F32)
    invf = jnp.concatenate([inv_freq, gap, inv_freq, gap]).reshape(1, LANES)
    return pl.pallas_call(
        _rope_table_kernel,
        grid=(n // tr,),
        in_specs=[pl.BlockSpec((tr, 1), lambda i: (i, 0)),
                  pl.BlockSpec((1, LANES), lambda i: (0, 0))],
        out_specs=[pl.BlockSpec((tr, LANES), lambda i: (i, 0)),
                   pl.BlockSpec((tr, LANES), lambda i: (i, 0))],
        out_shape=[jax.ShapeDtypeStruct((n, LANES), F32),
                   jax.ShapeDtypeStruct((n, LANES), F32)],
        name="rope_tables",
        compiler_params=_params(("parallel",), 32),
    )(pos, invf)


def _mm_plain_kernel(a_ref, w_ref, o_ref, w_scr):
    _cast_weights(w_ref, w_scr)
    acc = jnp.dot(a_ref[...], w_scr[...], preferred_element_type=F32)
    o_ref[...] = acc.astype(o_ref.dtype)


def _mm_transposed_kernel(a_ref, w_ref, o_ref, w_scr):
    _cast_weights(w_ref, w_scr)
    for c0 in range(0, w_scr.shape[1], MXU_COLS):
        acc = jnp.dot(a_ref[...], w_scr[:, c0:c0 + MXU_COLS], preferred_element_type=F32)
        o_ref[c0:c0 + MXU_COLS, :] = acc.T.astype(o_ref.dtype)


def _mm_sigmoid_kernel(a_ref, w_ref, o_ref, w_scr):
    _cast_weights(w_ref, w_scr)
    for c0 in range(0, w_scr.shape[1], MXU_COLS):
        acc = jnp.dot(a_ref[...], w_scr[:, c0:c0 + MXU_COLS], preferred_element_type=F32)
        o_ref[:, c0:c0 + MXU_COLS] = (0.5 * jnp.tanh(0.5 * acc) + 0.5).astype(o_ref.dtype)


def _mm_residual_kernel(a_ref, w_ref, r_ref, o_ref, w_scr):
    _cast_weights(w_ref, w_scr)
    acc = jnp.dot(a_ref[...], w_scr[...], preferred_element_type=F32)
    o_ref[...] = r_ref[...] + acc


def _mm_rope_kernel(a_ref, w_ref, cos_ref, sin_ref, o_ref, mean_ref, w_scr):
    _cast_qk_weights(w_ref, w_scr)
    tm, tn = o_ref.shape
    q_scale = jnp.where(pl.program_id(0) < D_ATTN // tn, SOFTMAX_LOG2_SCALE, 1.0)
    cosf = cos_ref[...] * q_scale
    sinf = sin_ref[...] * q_scale
    for c0 in range(0, tn, MXU_COLS):
        acc = jnp.dot(a_ref[...], w_scr[:, c0:c0 + MXU_COLS], preferred_element_type=F32)
        for h0 in range(0, MXU_COLS, HEAD_DIM):
            a = acc[:, h0:h0 + HEAD_DIM]
            r = a * cosf + pltpu.roll(a, ROT_PARTNER, 1) * sinf
            cols = slice(c0 + h0, c0 + h0 + HEAD_DIM)
            o_ref[:, cols] = r.astype(o_ref.dtype)
            mean_ref[0, :, cols] = jnp.mean(r.reshape(tm // MOBA_BLOCK, MOBA_BLOCK, HEAD_DIM), axis=1)


def _matmul(kernel, name, a, w, *, col_off, n_cols, out_dtype, tm=1024, tn=512,
            extra=(), extra_blocks=(), extra_out_shape=(), extra_out_blocks=(), vmem_mib=48,
            transposed_out=False):
    m, k = a.shape
    off = col_off // tn
    spec = lambda shape, fn: pl.BlockSpec(shape, lambda j, i: fn(i, j))
    if transposed_out:
        main_shape, main_spec = (n_cols, m), spec((tn, tm), lambda i, j: (j, i))
    else:
        main_shape, main_spec = (m, n_cols), spec((tm, tn), lambda i, j: (i, j))
    out_shape = [jax.ShapeDtypeStruct(main_shape, out_dtype)] + list(extra_out_shape)
    out_specs = [main_spec] + [spec(*blk) for blk in extra_out_blocks]
    res = pl.pallas_call(
        kernel,
        grid=(n_cols // tn, m // tm),
        in_specs=[spec((tm, k), lambda i, j: (i, 0)),
                  spec((k, tn), lambda i, j: (0, j + off))] + [spec(*blk) for blk in extra_blocks],
        out_specs=out_specs,
        out_shape=out_shape,
        scratch_shapes=[pltpu.VMEM((k, tn), BF16)],
        name=name,
        compiler_params=_params(("parallel", "arbitrary"), vmem_mib),
    )(a, w, *extra)
    return res if extra_out_shape else res[0]


def _rglru_kernel(xr_ref, yr_ref, cw_ref, cb_ref, wa_ref, wi_ref, ba_ref, bi_ref, lam_ref,
                  o_ref, xbuf, a_scr, u_scr, h_scr, nat_scr, tail_scr, hc_scr):
    ts, tc = xr_ref.shape
    nv = ts // SUBLANES
    halo = (CONV_WIDTH - 1) * SUBLANES
    t = pl.program_id(2)

    def grp(g):
        return slice(halo + g * SUBLANES, halo + (g + 1) * SUBLANES)

    @pl.when(t == 0)
    def _():
        tail_scr[...] = jnp.zeros_like(tail_scr)
        hc_scr[...] = jnp.zeros_like(hc_scr)

    lane_blocks = [slice(cb * LANES, (cb + 1) * LANES) for cb in range(tc // LANES)]
    for cb, ls in enumerate(lane_blocks):
        nat_scr[cb] = xr_ref[:, ls]
    for v in range(nv):
        for cb, ls in enumerate(lane_blocks):
            xbuf[grp(v), ls] = nat_scr[cb, pl.ds(v, SUBLANES, stride=nv), :]
    sub = lax.broadcasted_iota(jnp.int32, (SUBLANES, tc), 0)
    for d in range(1, CONV_WIDTH):
        keep = slice((CONV_WIDTH - 1 - d) * SUBLANES, (CONV_WIDTH - d) * SUBLANES)
        cur = xbuf[grp(nv - d), :]
        xbuf[grp(-d), :] = pltpu.roll(jnp.where(sub == SUBLANES - 1, tail_scr[keep, :], cur), 1, 0)
        tail_scr[keep, :] = cur

    xc = jnp.zeros((ts, tc), F32) + cb_ref[...]
    for kk in range(CONV_WIDTH):
        start = halo - (CONV_WIDTH - 1 - kk) * SUBLANES
        xc = xc + cw_ref[kk:kk + 1, :] * xbuf[start:start + ts, :]

    half_a = (-0.5 * LRU_C) * jax.nn.softplus(-lam_ref[...])
    for nb in range(tc // RNN_BLOCK):
        sl = slice(nb * RNN_BLOCK, (nb + 1) * RNN_BLOCK)
        xb = xc[:, sl]
        xb16 = xb.astype(BF16)
        zr = jnp.dot(xb16, wa_ref[nb], preferred_element_type=F32) + ba_ref[:, sl]
        zi = jnp.dot(xb16, wi_ref[nb], preferred_element_type=F32) + bi_ref[:, sl]
        log_a = half_a[:, sl] * jnp.tanh(0.5 * zr) + half_a[:, sl]
        ig = 0.5 * jnp.tanh(0.5 * zi) + 0.5
        a_scr[:, sl] = jnp.exp(log_a)
        th = jnp.tanh(log_a)
        y = -2.0 * th / (1.0 - th)
        u_scr[:, sl] = jnp.where(y > 0.0, y * lax.rsqrt(y), 0.0) * (ig * xb)

    h_end = jnp.zeros((SUBLANES, tc), F32)
    p_end = jnp.ones((SUBLANES, tc), F32)
    for v in range(nv):
        rows = slice(v * SUBLANES, (v + 1) * SUBLANES)
        a = a_scr[rows, :]
        h_end = a * h_end + u_scr[rows, :]
        p_end = a * p_end
        h_scr[rows, :] = h_end
        a_scr[rows, :] = p_end
    h_in = hc_scr[...]
    entering = []
    for s in range(SUBLANES):
        entering.append(h_in)
        h_in = h_end[s:s + 1, :] + p_end[s:s + 1, :] * h_in
    hc_scr[...] = h_in
    h_enter = jnp.concatenate(entering, axis=0)

    for v in range(nv):
        rows = slice(v * SUBLANES, (v + 1) * SUBLANES)
        h_v = h_scr[rows, :] + a_scr[rows, :] * h_enter
        for cb, ls in enumerate(lane_blocks):
            nat_scr[cb, pl.ds(v, SUBLANES, stride=nv), :] = h_v[:, ls]
    for cb, ls in enumerate(lane_blocks):
        o_ref[:, ls] = (jax.nn.gelu(yr_ref[:, ls]) * nat_scr[cb]).astype(o_ref.dtype)


def _rglru(xy, conv_w, conv_b, w_a, b_a, w_i, b_i, lam, batch, seq, ts=512, tc=512):
    n = batch * seq
    nt = seq // ts
    ncb = D_RNN // tc
    halo = (CONV_WIDTH - 1) * SUBLANES
    row = lambda v: v.reshape(1, D_RNN)
    vec_spec = pl.BlockSpec((1, tc), lambda b, c, t: (0, c))
    gate_spec = pl.BlockSpec((tc // RNN_BLOCK, RNN_BLOCK, RNN_BLOCK), lambda b, c, t: (c, 0, 0))
    return pl.pallas_call(
        _rglru_kernel,
        grid=(batch, ncb, nt),
        in_specs=[pl.BlockSpec((ts, tc), lambda b, c, t: (b * nt + t, c)),
                  pl.BlockSpec((ts, tc), lambda b, c, t: (b * nt + t, ncb + c)),
                  pl.BlockSpec((CONV_WIDTH, tc), lambda b, c, t: (0, c)),
                  vec_spec, gate_spec, gate_spec, vec_spec, vec_spec, vec_spec],
        out_specs=pl.BlockSpec((ts, tc), lambda b, c, t: (b * nt + t, c)),
        out_shape=jax.ShapeDtypeStruct((n, D_RNN), BF16),
        scratch_shapes=[pltpu.VMEM((halo + ts, tc), F32),
                        pltpu.VMEM((ts, tc), F32),
                        pltpu.VMEM((ts, tc), F32),
                        pltpu.VMEM((ts, tc), F32),
                        pltpu.VMEM((tc // LANES, ts, LANES), F32),
                        pltpu.VMEM((halo, tc), F32),
                        pltpu.VMEM((1, tc), F32)],
        name="rglru",
        compiler_params=_params(("parallel", "parallel", "arbitrary"), 32),
    )(xy, xy, conv_w, row(conv_b), w_a.astype(BF16), w_i.astype(BF16), row(b_a), row(b_i), row(lam))


def _moba_kernel(q_ref, k_ref, v_ref, km_ref, o_ref, bias_scr, s_scr, acc_scr, *, heads):
    j = pl.program_id(2)
    nblk = km_ref.shape[0]
    blk = MOBA_BLOCK
    head_slices = [slice(h * HEAD_DIM, (h + 1) * HEAD_DIM) for h in range(heads)]

    def score_into(slot, block):
        start = pl.multiple_of(block * blk, blk)
        for h, hs in enumerate(head_slices):
            s_scr[slot, h] = lax.dot_general(k_ref[pl.ds(start, blk), hs], q_ref[:, hs], _NT,
                                             preferred_element_type=F32)

    ones_rows = jnp.ones((BF16_ROWS, blk), BF16)

    def attend(slot, mask, block, state):
        start = pl.multiple_of(block * blk, blk)
        half = blk // 2
        soft = []
        for h in range(heads):
            m = state[h]
            unselected = bias_scr[h, pl.ds(block, 1), :] < 0.0
            s_lo = mask(s_scr[slot, h, 0:half, :], 0)
            s_hi = mask(s_scr[slot, h, half:blk, :], half)
            m_blk = jnp.maximum(jnp.max(s_lo, axis=0, keepdims=True), jnp.max(s_hi, axis=0, keepdims=True))
            m_new = jnp.where(unselected, m, jnp.maximum(m, m_blk))
            alpha = jnp.exp2(m - m_new)
            p = jnp.exp2(mask(s_scr[slot, h], 0) - jnp.where(unselected, -NEG_INF, m_new))
            soft.append((m_new, alpha, p.astype(BF16)))
        out = []
        for h, hs in enumerate(head_slices):
            m_new, alpha, p = soft[h]
            vt = jnp.concatenate([v_ref[hs, pl.ds(start, blk)], ones_rows], axis=0)
            pv = jnp.dot(vt, p, preferred_element_type=F32)
            acc_scr[h] = alpha * acc_scr[h] + pv
            out.append(m_new)
        return tuple(out)

    no_mask = lambda s, row0: s

    gates = []
    for hs in head_slices:
        q = q_ref[:, hs]
        km = km_ref[:, hs]
        km_hi = km.astype(BF16)
        km_lo = (km - km_hi.astype(F32)).astype(BF16)
        gates.append(lax.dot_general(km_hi, q, _NT, preferred_element_type=F32)
                     + lax.dot_general(km_lo, q, _NT, preferred_element_type=F32))
    score_into(0, 0)
    for h, gate in enumerate(gates):
        bidx = lax.broadcasted_iota(jnp.int32, gate.shape, 0)
        past = bidx < j
        g = jnp.where(past, gate, -jnp.inf)
        rank = jnp.zeros(gate.shape, jnp.int32)
        for other in range(nblk):
            go = g[other:other + 1, :]
            beats = jnp.where(go > g, 1, jnp.where(go == g, jnp.where(bidx > other, 1, 0), 0))
            rank = rank + beats
        bias_scr[h] = jnp.where(past, jnp.where(rank < MOBA_TOPK, 0.0, NEG_INF),
                                jnp.where(bidx == j, 0.0, NEG_INF))

    def pair_body(t, state):
        first = 2 * t
        score_into(1, first + 1)
        state = attend(0, no_mask, first, state)
        score_into(0, first + 2)
        return attend(1, no_mask, first + 1, state)

    def odd_body(state):
        score_into(1, j)
        return attend(0, no_mask, j - 1, state)

    acc_scr[...] = jnp.zeros_like(acc_scr)
    init = tuple(jnp.full((1, blk), NEG_INF, F32) for _ in range(heads))
    state = lax.fori_loop(0, lax.shift_right_logical(j, 1), pair_body, init)
    odd = lax.bitwise_and(j, 1)
    state = lax.cond(odd == 1, odd_body, lambda st: st, state)

    def causal_mask(s, row0):
        kpos = row0 + lax.broadcasted_iota(jnp.int32, s.shape, 0)
        qpos = lax.broadcasted_iota(jnp.int32, s.shape, 1)
        return jnp.where(kpos <= qpos, s, NEG_INF)

    attend(odd, causal_mask, j, state)
    for h, hs in enumerate(head_slices):
        o_ref[:, hs] = (acc_scr[h, 0:HEAD_DIM, :] / acc_scr[h, HEAD_DIM:HEAD_DIM + 1, :]
                        ).T.astype(o_ref.dtype)


def _moba(qk, v_t, kmean, batch, seq, heads=4):
    n = batch * seq
    nblk = seq // MOBA_BLOCK
    groups = N_HEADS // heads
    width = heads * HEAD_DIM
    return pl.pallas_call(
        functools.partial(_moba_kernel, heads=heads),
        grid=(batch, groups, nblk),
        in_specs=[pl.BlockSpec((MOBA_BLOCK, width), lambda b, h, j: (b * nblk + j, h)),
                  pl.BlockSpec((seq, width), lambda b, h, j: (b, groups + h)),
                  pl.BlockSpec((width, seq), lambda b, h, j: (h, b)),
                  pl.BlockSpec((nblk, width), lambda b, h, j: (b, groups + h))],
        out_specs=pl.BlockSpec((MOBA_BLOCK, width), lambda b, h, j: (b * nblk + j, h)),
        out_shape=jax.ShapeDtypeStruct((n, D_ATTN), BF16),
        scratch_shapes=[pltpu.VMEM((heads, nblk, MOBA_BLOCK), F32),
                        pltpu.VMEM((2, heads, MOBA_BLOCK, MOBA_BLOCK), F32),
                        pltpu.VMEM((heads, HEAD_DIM + BF16_ROWS, MOBA_BLOCK), F32)],
        name="moba",
        compiler_params=_params(("parallel", "parallel", "arbitrary"), 56),
    )(qk, qk, v_t, kmean)


def _merge_kernel(gh_ref, o_ref, wr_ref, wa_ref, gr_ref, ga_ref, out_ref, wr_scr, wa_scr):
    _cast_weights(wr_ref, wr_scr)
    _cast_weights(wa_ref, wa_scr)
    rnn = jnp.dot(gh_ref[...], wr_scr[...], preferred_element_type=F32)
    att = jnp.dot(o_ref[...], wa_scr[...], preferred_element_type=F32)
    out_ref[...] = (gr_ref[...].astype(F32) * rnn + ga_ref[...].astype(F32) * att).astype(out_ref.dtype)


def _merge(gh, o, w_rnn, w_attn, gates, tm=1024, tn=512):
    m, k = gh.shape
    nj = D_MODEL // tn
    a_spec = pl.BlockSpec((tm, k), lambda j, i: (i, 0))
    w_spec = pl.BlockSpec((k, tn), lambda j, i: (0, j))
    return pl.pallas_call(
        _merge_kernel,
        grid=(nj, m // tm),
        in_specs=[a_spec, a_spec, w_spec, w_spec,
                  pl.BlockSpec((tm, tn), lambda j, i: (i, j)),
                  pl.BlockSpec((tm, tn), lambda j, i: (i, nj + j))],
        out_specs=pl.BlockSpec((tm, tn), lambda j, i: (i, j)),
        out_shape=jax.ShapeDtypeStruct((m, D_MODEL), BF16),
        scratch_shapes=[pltpu.VMEM((k, tn), BF16), pltpu.VMEM((k, tn), BF16)],
        name="merge",
        compiler_params=_params(("parallel", "arbitrary"), 48),
    )(gh, o, w_rnn, w_attn, gates, gates)


def _memkv_kernel(mem_ref, g_ref, w_ref, o_ref):
    hn = _rms(mem_ref[...], g_ref[...]).astype(BF16)
    o_ref[...] = jnp.dot(hn, w_ref[...], preferred_element_type=F32).astype(o_ref.dtype)


def _memkv(mem2d, g, w_kv, mem_len):
    m, d = mem2d.shape
    return pl.pallas_call(
        _memkv_kernel,
        grid=(m // mem_len,),
        in_specs=[pl.BlockSpec((mem_len, d), lambda i: (i, 0)),
                  pl.BlockSpec((1, d), lambda i: (0, 0)),
                  pl.BlockSpec((d, 2 * D_MEM), lambda i: (0, 0))],
        out_specs=pl.BlockSpec((mem_len, 2 * D_MEM), lambda i: (i, 0)),
        out_shape=jax.ShapeDtypeStruct((m, 2 * D_MEM), BF16),
        name="mem_kv",
        compiler_params=_params(("parallel",), 32),
    )(mem2d, g.reshape(1, d), w_kv)


def _xattn_kernel(x_ref, g_ref, wq_ref, kv_ref, wo_ref, o_ref):
    x = x_ref[...]
    hn = _rms(x, g_ref[...]).astype(BF16)
    q = jnp.dot(hn, wq_ref[...], preferred_element_type=F32).astype(BF16)
    scale = MEM_HEAD_DIM ** -0.5
    heads = []
    for hd in range(MEM_HEADS):
        sl = slice(hd * MEM_HEAD_DIM, (hd + 1) * MEM_HEAD_DIM)
        kh = kv_ref[:, sl]
        vh = kv_ref[:, D_MEM + hd * MEM_HEAD_DIM:D_MEM + (hd + 1) * MEM_HEAD_DIM]
        s = lax.dot_general(q[:, sl], kh, _NT, preferred_element_type=F32) * scale
        m = jnp.max(s, axis=-1, keepdims=True)
        p = jnp.exp(s - m)
        l = jnp.sum(p, axis=-1, keepdims=True)
        oh = jnp.dot(p.astype(BF16), vh, preferred_element_type=F32) / l
        heads.append(oh.astype(BF16))
    o_all = jnp.concatenate(heads, axis=-1)
    o_ref[...] = x + jnp.dot(o_all, wo_ref[...], preferred_element_type=F32)


def _xattn(x, g, w_q, kv, w_o, seq, mem_len, tm=512):
    m, d = x.shape
    per_batch = seq // tm
    return pl.pallas_call(
        _xattn_kernel,
        grid=(m // tm,),
        in_specs=[pl.BlockSpec((tm, d), lambda i: (i, 0)),
                  pl.BlockSpec((1, d), lambda i: (0, 0)),
                  pl.BlockSpec((d, D_MEM), lambda i: (0, 0)),
                  pl.BlockSpec((mem_len, 2 * D_MEM), lambda i: (i // per_batch, 0)),
                  pl.BlockSpec((D_MEM, d), lambda i: (0, 0))],
        out_specs=pl.BlockSpec((tm, d), lambda i: (i, 0)),
        out_shape=jax.ShapeDtypeStruct((m, d), F32),
        name="xattn",
        compiler_params=_params(("parallel",), 48),
    )(x, g.reshape(1, d), w_q, kv, w_o)


def _ffn_kernel(x_ref, g_ref, wg_ref, wu_ref, wd_ref, gf_ref, o_ref, h_scr):
    f = pl.program_id(1)

    @pl.when(f == 0)
    def _():
        h_scr[...] = _rms(x_ref[...], g_ref[...]).astype(BF16)
        o_ref[...] = jnp.zeros_like(o_ref)

    hn = h_scr[...]
    a = jnp.dot(hn, wg_ref[...], preferred_element_type=F32)
    b = jnp.dot(hn, wu_ref[...], preferred_element_type=F32)
    act = (jax.nn.silu(a) * b).astype(BF16)
    o_ref[...] += jnp.dot(act, wd_ref[...], preferred_element_type=F32)

    @pl.when(f == pl.num_programs(1) - 1)
    def _():
        o_ref[...] = _rms(x_ref[...] + o_ref[...], gf_ref[...])


def _ffn(x, g, w_gate, w_up, w_down, g_final, tm=1024, tf=512):
    m, d = x.shape
    d_ff = w_gate.shape[1]
    return pl.pallas_call(
        _ffn_kernel,
        grid=(m // tm, d_ff // tf),
        in_specs=[pl.BlockSpec((tm, d), lambda i, f: (i, 0)),
                  pl.BlockSpec((1, d), lambda i, f: (0, 0)),
                  pl.BlockSpec((d, tf), lambda i, f: (0, f)),
                  pl.BlockSpec((d, tf), lambda i, f: (0, f)),
                  pl.BlockSpec((tf, d), lambda i, f: (f, 0)),
                  pl.BlockSpec((1, d), lambda i, f: (0, 0))],
        out_specs=pl.BlockSpec((tm, d), lambda i, f: (i, 0)),
        out_shape=jax.ShapeDtypeStruct((m, d), F32),
        scratch_shapes=[pltpu.VMEM((tm, d), BF16)],
        name="ffn",
        compiler_params=_params(("parallel", "arbitrary"), 63),
    )(x, g.reshape(1, d), w_gate, w_up, w_down, g_final.reshape(1, d))


def _layer(x2d, mem2d, cosf, sinf, batch, seq, mem_len, p):
    n = x2d.shape[0]
    w_in = p["w_in"]
    hn = _norm_bf16(x2d, p["norm_mix_g"])

    c0 = 0
    wide = 1024
    xy = _matmul(_mm_plain_kernel, "proj_xy", hn, w_in, col_off=c0, n_cols=2 * D_RNN, out_dtype=F32,
                 tn=wide)
    c0 += 2 * D_RNN
    tm, tn = 1024, wide
    rope_block = ((tm, LANES), lambda i, j: (i, 0))
    qk, means = _matmul(
        _mm_rope_kernel, "proj_qk", hn, w_in, col_off=c0, n_cols=2 * D_ATTN, out_dtype=BF16,
        tm=tm, tn=tn, extra=(cosf, sinf), extra_blocks=(rope_block, rope_block),
        extra_out_shape=(jax.ShapeDtypeStruct((n // tm, tm // MOBA_BLOCK, 2 * D_ATTN), F32),),
        extra_out_blocks=(((1, tm // MOBA_BLOCK, tn), lambda i, j: (i, 0, j)),))
    c0 += 2 * D_ATTN
    v_t = _matmul(_mm_transposed_kernel, "proj_v", hn, w_in, col_off=c0, n_cols=D_ATTN,
                  out_dtype=BF16, tn=wide, transposed_out=True)
    c0 += D_ATTN
    gates = _matmul(_mm_sigmoid_kernel, "proj_gates", hn, w_in, col_off=c0, n_cols=2 * D_MODEL,
                    out_dtype=BF16, tn=wide)

    gh = _rglru(xy, p["conv_w"], p["conv_b"], p["lru_w_a"], p["lru_b_a"], p["lru_w_i"],
                p["lru_b_i"], p["lru_lambda"], batch, seq)
    kmean = means.reshape(n // MOBA_BLOCK, 2 * D_ATTN)
    o = _moba(qk, v_t, kmean, batch, seq)

    merged = _merge(gh, o, p["w_rnn_proj"], p["w_attn_proj"], gates)
    x1 = _matmul(_mm_residual_kernel, "mix_out", merged, p["w_mix_out"].astype(BF16), col_off=0,
                 n_cols=D_MODEL, out_dtype=F32, tm=512, tn=D_MODEL,
                 extra=(x2d,), extra_blocks=(((512, D_MODEL), lambda i, j: (i, j)),))

    kv = _memkv(mem2d, p["norm_mem_g"], p["w_xkv"].astype(BF16), mem_len)
    x2 = _xattn(x1, p["norm_xq_g"], p["w_xq"].astype(BF16), kv, p["w_xo"].astype(BF16), seq, mem_len)
    return x2


def kernel(x, mem, positions, norm_mix_g, w_in, conv_w, conv_b, lru_w_a, lru_b_a, lru_w_i, lru_b_i,
           lru_lambda, w_rnn_proj, w_attn_proj, w_mix_out, norm_xq_g, norm_mem_g, w_xq, w_xkv, w_xo,
           norm_ffn_g, w_ffn_gate, w_ffn_up, w_ffn_down, norm_final_g):
    batch, seq, d = x.shape
    mem_len = mem.shape[1]
    assert w_in.shape[0] == 1, "only DEPTH == 1 is supported"
    x2d = x.reshape(batch * seq, d)
    mem2d = mem.reshape(batch * mem_len, d)
    cosf, sinf = _rope_tables(positions)
    p = dict(norm_mix_g=norm_mix_g[0], w_in=w_in[0], conv_w=conv_w[0], conv_b=conv_b[0],
             lru_w_a=lru_w_a[0], lru_b_a=lru_b_a[0], lru_w_i=lru_w_i[0], lru_b_i=lru_b_i[0],
             lru_lambda=lru_lambda[0], w_rnn_proj=w_rnn_proj[0], w_attn_proj=w_attn_proj[0],
             w_mix_out=w_mix_out[0], norm_xq_g=norm_xq_g[0], norm_mem_g=norm_mem_g[0],
             w_xq=w_xq[0], w_xkv=w_xkv[0], w_xo=w_xo[0])
    x2 = _layer(x2d, mem2d, cosf, sinf, batch, seq, mem_len, p)
    out = _ffn(x2, norm_ffn_g[0], w_ffn_gate[0].astype(BF16), w_ffn_up[0].astype(BF16),
               w_ffn_down[0].astype(BF16), norm_final_g)
    return out.reshape(batch, seq, d)
```

```python
import functools

import jax
import jax.numpy as jnp
from jax import lax
from jax.experimental import pallas as pl
from jax.experimental.pallas import tpu as pltpu

D_MODEL = 2048
N_HEADS = 16
HEAD_DIM = 128
D_ATTN = N_HEADS * HEAD_DIM
MOBA_BLOCK = 256
MOBA_TOPK = 3
ROPE_THETA = 500000.0
ROT_DIM = HEAD_DIM // 4
ROT_HALF = ROT_DIM // 2
D_RNN = 2048
N_RNN_BLOCKS = 16
RNN_BLOCK = D_RNN // N_RNN_BLOCKS
CONV_WIDTH = 4
LRU_C = 8.0
MEM_HEADS = 4
MEM_HEAD_DIM = 128
D_MEM = MEM_HEADS * MEM_HEAD_DIM
RMS_EPS = 1e-6
NEG_INF = -1e30
LOG2_E = 1.4426950408889634
SOFTMAX_LOG2_SCALE = (HEAD_DIM ** -0.5) * LOG2_E

LANES = 128
SUBLANES = 8
BF16_ROWS = 16
MXU_COLS = 2 * 256
ROT_PARTNER = LANES // 2
MIB = 1024 * 1024

BF16 = jnp.bfloat16
F32 = jnp.float32

_NT = (((1,), (1,)), ((), ()))
_TN = (((0,), (0,)), ((), ()))


def _params(semantics, vmem_mib):
    return pltpu.CompilerParams(dimension_semantics=semantics,
                                vmem_limit_bytes=vmem_mib * MIB)


def _rms(x, g):
    ms = jnp.mean(x * x, axis=-1, keepdims=True)
    return x * lax.rsqrt(ms + RMS_EPS) * g


def _norm_kernel(x_ref, g_ref, o_ref):
    o_ref[...] = _rms(x_ref[...], g_ref[...]).astype(o_ref.dtype)


def _norm_bf16(x, g, tm=512):
    m, d = x.shape
    return pl.pallas_call(
        _norm_kernel,
        grid=(m // tm,),
        in_specs=[pl.BlockSpec((tm, d), lambda i: (i, 0)),
                  pl.BlockSpec((1, d), lambda i: (0, 0))],
        out_specs=pl.BlockSpec((tm, d), lambda i: (i, 0)),
        out_shape=jax.ShapeDtypeStruct((m, d), BF16),
        name="norm_mix",
        compiler_params=_params(("parallel",), 32),
    )(x, g.reshape(1, d))


def _rope_table_kernel(pos_ref, invf_ref, cos_ref, sin_ref):
    ang = pos_ref[...] * invf_ref[...]
    lane = lax.broadcasted_iota(jnp.int32, ang.shape, 1)
    s = jnp.sin(ang)
    cos_ref[...] = jnp.cos(ang)
    sin_ref[...] = jnp.where(lane < ROT_PARTNER, -s, s)


def _rope_head_layout():
    split = ROT_DIM + ROT_PARTNER - ROT_HALF
    return [(0, ROT_HALF), (ROT_DIM, split), (ROT_HALF, ROT_DIM), (split, HEAD_DIM)]


def _cast_weights(w_ref, w_scr):
    @pl.when(pl.program_id(1) == 0)
    def _():
        w_scr[...] = w_ref[...].astype(w_scr.dtype)


def _cast_qk_weights(w_ref, w_scr):
    @pl.when(pl.program_id(1) == 0)
    def _():
        rows = 256
        lane = lax.broadcasted_iota(jnp.int32, (rows, HEAD_DIM), 1)
        for r0 in range(0, w_ref.shape[0], rows):
            for h0 in range(0, w_ref.shape[1], HEAD_DIM):
                x = w_ref[r0:r0 + rows, h0:h0 + HEAD_DIM]
                out, at = x, 0
                for start, stop in _rope_head_layout():
                    if start != at:
                        moved = pltpu.roll(x, (at - start) % HEAD_DIM, 1)
                        out = jnp.where((lane >= at) & (lane < at + stop - start), moved, out)
                    at += stop - start
                w_scr[r0:r0 + rows, h0:h0 + HEAD_DIM] = out.astype(w_scr.dtype)


def _rope_tables(positions, tr=1024):
    n = positions.size
    pos = positions.astype(F32).reshape(n, 1)
    inv_freq = jnp.power(ROPE_THETA, -jnp.arange(ROT_HALF, dtype=F32) / ROT_HALF)
    gap = jnp.zeros((ROT_PARTNER - ROT_HALF,), F32)
    invf = jnp.concatenate([inv_freq, gap, inv_freq, gap]).reshape(1, LANES)
    return pl.pallas_call(
        _rope_table_kernel,
        grid=(n // tr,),
        in_specs=[pl.BlockSpec((tr, 1), lambda i: (i, 0)),
                  pl.BlockSpec((1, LANES), lambda i: (0, 0))],
        out_specs=[pl.BlockSpec((tr, LANES), lambda i: (i, 0)),
                   pl.BlockSpec((tr, LANES), lambda i: (i, 0))],
        out_shape=[jax.ShapeDtypeStruct((n, LANES), F32),
                   jax.ShapeDtypeStruct((n, LANES), F32)],
        name="rope_tables",
        compiler_params=_params(("parallel",), 32),
    )(pos, invf)


def _mm_plain_kernel(a_ref, w_ref, o_ref, w_scr):
    _cast_weights(w_ref, w_scr)
    acc = jnp.dot(a_ref[...], w_scr[...], preferred_element_type=F32)
    o_ref[...] = acc.astype(o_ref.dtype)


def _mm_transposed_kernel(a_ref, w_ref, o_ref, w_scr):
    _cast_weights(w_ref, w_scr)
    for c0 in range(0, w_scr.shape[1], MXU_COLS):
        acc = jnp.dot(a_ref[...], w_scr[:, c0:c0 + MXU_COLS], preferred_element_type=F32)
        o_ref[c0:c0 + MXU_COLS, :] = acc.T.astype(o_ref.dtype)


def _mm_sigmoid_kernel(a_ref, w_ref, o_ref, w_scr):
    _cast_weights(w_ref, w_scr)
    for c0 in range(0, w_scr.shape[1], MXU_COLS):
        acc = jnp.dot(a_ref[...], w_scr[:, c0:c0 + MXU_COLS], preferred_element_type=F32)
        o_ref[:, c0:c0 + MXU_COLS] = (0.5 * jnp.tanh(0.5 * acc) + 0.5).astype(o_ref.dtype)


def _mm_residual_kernel(a_ref, w_ref, r_ref, o_ref, w_scr):
    _cast_weights(w_ref, w_scr)
    acc = jnp.dot(a_ref[...], w_scr[...], preferred_element_type=F32)
    o_ref[...] = r_ref[...] + acc


def _mm_rope_kernel(a_ref, w_ref, cos_ref, sin_ref, o_ref, mean_ref, w_scr):
    _cast_qk_weights(w_ref, w_scr)
    tm, tn = o_ref.shape
    q_scale = jnp.where(pl.program_id(0) < D_ATTN // tn, SOFTMAX_LOG2_SCALE, 1.0)
    cosf = cos_ref[...] * q_scale
    sinf = sin_ref[...] * q_scale
    for c0 in range(0, tn, MXU_COLS):
        acc = jnp.dot(a_ref[...], w_scr[:, c0:c0 + MXU_COLS], preferred_element_type=F32)
        for h0 in range(0, MXU_COLS, HEAD_DIM):
            a = acc[:, h0:h0 + HEAD_DIM]
            r = a * cosf + pltpu.roll(a, ROT_PARTNER, 1) * sinf
            cols = slice(c0 + h0, c0 + h0 + HEAD_DIM)
            o_ref[:, cols] = r.astype(o_ref.dtype)
            mean_ref[0, :, cols] = jnp.mean(r.reshape(tm // MOBA_BLOCK, MOBA_BLOCK, HEAD_DIM), axis=1)


def _matmul(kernel, name, a, w, *, col_off, n_cols, out_dtype, tm=1024, tn=512,
            extra=(), extra_blocks=(), extra_out_shape=(), extra_out_blocks=(), vmem_mib=48,
            transposed_out=False):
    m, k = a.shape
    off = col_off // tn
    spec = lambda shape, fn: pl.BlockSpec(shape, lambda j, i: fn(i, j))
    if transposed_out:
        main_shape, main_spec = (n_cols, m), spec((tn, tm), lambda i, j: (j, i))
    else:
        main_shape, main_spec = (m, n_cols), spec((tm, tn), lambda i, j: (i, j))
    out_shape = [jax.ShapeDtypeStruct(main_shape, out_dtype)] + list(extra_out_shape)
    out_specs = [main_spec] + [spec(*blk) for blk in extra_out_blocks]
    res = pl.pallas_call(
        kernel,
        grid=(n_cols // tn, m // tm),
        in_specs=[spec((tm, k), lambda i, j: (i, 0)),
                  spec((k, tn), lambda i, j: (0, j + off))] + [spec(*blk) for blk in extra_blocks],
        out_specs=out_specs,
        out_shape=out_shape,
        scratch_shapes=[pltpu.VMEM((k, tn), BF16)],
        name=name,
        compiler_params=_params(("parallel", "arbitrary"), vmem_mib),
    )(a, w, *extra)
    return res if extra_out_shape else res[0]


def _rglru_kernel(xr_ref, yr_ref, cw_ref, cb_ref, wa_ref, wi_ref, ba_ref, bi_ref, lam_ref,
                  o_ref, xbuf, a_scr, u_scr, h_scr, nat_scr, tail_scr, hc_scr):
    ts, tc = xr_ref.shape
    nv = ts // SUBLANES
    halo = (CONV_WIDTH - 1) * SUBLANES
    t = pl.program_id(2)

    def grp(g):
        return slice(halo + g * SUBLANES, halo + (g + 1) * SUBLANES)

    @pl.when(t == 0)
    def _():
        tail_scr[...] = jnp.zeros_like(tail_scr)
        hc_scr[...] = jnp.zeros_like(hc_scr)

    lane_blocks = [slice(cb * LANES, (cb + 1) * LANES) for cb in range(tc // LANES)]
    per_seg = nv // SUBLANES
    for cb, ls in enumerate(lane_blocks):
        for s in range(SUBLANES):
            for q in range(per_seg):
                t0 = s * nv + q * SUBLANES
                dst = pl.ds(halo + q * SUBLANES * SUBLANES + s, SUBLANES, stride=SUBLANES)
                xbuf[cb, dst, :] = xr_ref[t0:t0 + SUBLANES, ls]
    sub = lax.broadcasted_iota(jnp.int32, (SUBLANES, LANES), 0)
    for cb, ls in enumerate(lane_blocks):
        for d in range(1, CONV_WIDTH):
            keep = slice((CONV_WIDTH - 1 - d) * SUBLANES, (CONV_WIDTH - d) * SUBLANES)
            cur = xbuf[cb, grp(nv - d), :]
            xbuf[cb, grp(-d), :] = pltpu.roll(
                jnp.where(sub == SUBLANES - 1, tail_scr[keep, ls], cur), 1, 0)
            tail_scr[keep, ls] = cur

    half_a = (-0.5 * LRU_C) * jax.nn.softplus(-lam_ref[...])
    chunk = 128
    for nb, sl in enumerate(lane_blocks):
        for r0 in range(0, ts, chunk):
            rows = slice(r0, r0 + chunk)
            xb = jnp.zeros((chunk, LANES), F32) + cb_ref[:, sl]
            for kk in range(CONV_WIDTH):
                start = r0 + halo - (CONV_WIDTH - 1 - kk) * SUBLANES
                xb = xb + cw_ref[kk:kk + 1, sl] * xbuf[nb, start:start + chunk, :]
            xb16 = xb.astype(BF16)
            zr = jnp.dot(xb16, wa_ref[nb], preferred_element_type=F32) + ba_ref[:, sl]
            zi = jnp.dot(xb16, wi_ref[nb], preferred_element_type=F32) + bi_ref[:, sl]
            log_a = half_a[:, sl] * jnp.tanh(0.5 * zr) + half_a[:, sl]
            ig = 0.5 * jnp.tanh(0.5 * zi) + 0.5
            a_scr[rows, sl] = jnp.exp(log_a)
            th = jnp.tanh(log_a)
            y = -2.0 * th / (1.0 - th)
            u_scr[rows, sl] = jnp.where(y > 0.0, y * lax.rsqrt(y), 0.0) * (ig * xb)

    h_end = jnp.zeros((SUBLANES, tc), F32)
    p_end = jnp.ones((SUBLANES, tc), F32)
    for v in range(nv):
        rows = slice(v * SUBLANES, (v + 1) * SUBLANES)
        a = a_scr[rows, :]
        h_end = a * h_end + u_scr[rows, :]
        p_end = a * p_end
        h_scr[rows, :] = h_end
        a_scr[rows, :] = p_end
    h_in = hc_scr[...]
    entering = []
    for s in range(SUBLANES):
        entering.append(h_in)
        h_in = h_end[s:s + 1, :] + p_end[s:s + 1, :] * h_in
    hc_scr[...] = h_in
    h_enter = jnp.concatenate(entering, axis=0)

    for v in range(nv):
        rows = slice(v * SUBLANES, (v + 1) * SUBLANES)
        h_v = h_scr[rows, :] + a_scr[rows, :] * h_enter
        for cb, ls in enumerate(lane_blocks):
            nat_scr[cb, pl.ds(v, SUBLANES, stride=nv), :] = h_v[:, ls]
    for cb, ls in enumerate(lane_blocks):
        o_ref[:, ls] = (jax.nn.gelu(yr_ref[:, ls]) * nat_scr[cb]).astype(o_ref.dtype)


def _rglru(xy, conv_w, conv_b, w_a, b_a, w_i, b_i, lam, batch, seq, ts=512, tc=512):
    assert RNN_BLOCK == LANES, "the kernel walks gate blocks and 128-lane blocks together"
    n = batch * seq
    nt = seq // ts
    ncb = D_RNN // tc
    halo = (CONV_WIDTH - 1) * SUBLANES
    row = lambda v: v.reshape(1, D_RNN)
    vec_spec = pl.BlockSpec((1, tc), lambda b, c, t: (0, c))
    gate_spec = pl.BlockSpec((tc // RNN_BLOCK, RNN_BLOCK, RNN_BLOCK), lambda b, c, t: (c, 0, 0))
    return pl.pallas_call(
        _rglru_kernel,
        grid=(batch, ncb, nt),
        in_specs=[pl.BlockSpec((ts, tc), lambda b, c, t: (b * nt + t, c)),
                  pl.BlockSpec((ts, tc), lambda b, c, t: (b * nt + t, ncb + c)),
                  pl.BlockSpec((CONV_WIDTH, tc), lambda b, c, t: (0, c)),
                  vec_spec, gate_spec, gate_spec, vec_spec, vec_spec, vec_spec],
        out_specs=pl.BlockSpec((ts, tc), lambda b, c, t: (b * nt + t, c)),
        out_shape=jax.ShapeDtypeStruct((n, D_RNN), BF16),
        scratch_shapes=[pltpu.VMEM((tc // LANES, halo + ts, LANES), F32),
                        pltpu.VMEM((ts, tc), F32),
                        pltpu.VMEM((ts, tc), F32),
                        pltpu.VMEM((ts, tc), F32),
                        pltpu.VMEM((tc // LANES, ts, LANES), F32),
                        pltpu.VMEM((halo, tc), F32),
                        pltpu.VMEM((1, tc), F32)],
        name="rglru",
        compiler_params=_params(("parallel", "parallel", "arbitrary"), 32),
    )(xy, xy, conv_w, row(conv_b), w_a.astype(BF16), w_i.astype(BF16), row(b_a), row(b_i), row(lam))


def _moba_kernel(q_ref, k_ref, v_ref, km_ref, o_ref, bias_scr, s_scr, acc_scr, *, heads, tiles):
    first = pl.program_id(2) * tiles

    def one(u, carry):
        qrows = pl.ds(pl.multiple_of(u * MOBA_BLOCK, MOBA_BLOCK), MOBA_BLOCK)
        _moba_tile(first + u, qrows, q_ref, k_ref, v_ref, km_ref, o_ref, bias_scr, s_scr, acc_scr, heads)
        return carry

    lax.fori_loop(0, tiles, one, 0)


def _moba_tile(j, qrows, q_ref, k_ref, v_ref, km_ref, o_ref, bias_scr, s_scr, acc_scr, heads):
    nblk = km_ref.shape[0]
    blk = MOBA_BLOCK
    head_slices = [slice(h * HEAD_DIM, (h + 1) * HEAD_DIM) for h in range(heads)]

    def score_into(slot, block):
        start = pl.multiple_of(block * blk, blk)
        for h, hs in enumerate(head_slices):
            s_scr[slot, h] = lax.dot_general(k_ref[pl.ds(start, blk), hs], q_ref[qrows, hs], _NT,
                                             preferred_element_type=F32)

    ones_rows = jnp.ones((BF16_ROWS, blk), BF16)

    def attend(slot, mask, block, state):
        start = pl.multiple_of(block * blk, blk)
        half = blk // 2
        soft = []
        for h in range(heads):
            m = state[h]
            unselected = bias_scr[h, pl.ds(block, 1), :] < 0.0
            s_lo = mask(s_scr[slot, h, 0:half, :], 0)
            s_hi = mask(s_scr[slot, h, half:blk, :], half)
            m_blk = jnp.maximum(jnp.max(s_lo, axis=0, keepdims=True), jnp.max(s_hi, axis=0, keepdims=True))
            m_new = jnp.where(unselected, m, jnp.maximum(m, m_blk))
            alpha = jnp.exp2(m - m_new)
            p = jnp.exp2(mask(s_scr[slot, h], 0) - jnp.where(unselected, -NEG_INF, m_new))
            soft.append((m_new, alpha, p.astype(BF16)))
        out = []
        for h, hs in enumerate(head_slices):
            m_new, alpha, p = soft[h]
            vt = jnp.concatenate([v_ref[hs, pl.ds(start, blk)], ones_rows], axis=0)
            pv = jnp.dot(vt, p, preferred_element_type=F32)
            acc_scr[h] = alpha * acc_scr[h] + pv
            out.append(m_new)
        return tuple(out)

    no_mask = lambda s, row0: s

    gates = []
    for hs in head_slices:
        q = q_ref[qrows, hs]
        km = km_ref[:, hs]
        km_hi = km.astype(BF16)
        km_lo = (km - km_hi.astype(F32)).astype(BF16)
        gates.append(lax.dot_general(km_hi, q, _NT, preferred_element_type=F32)
                     + lax.dot_general(km_lo, q, _NT, preferred_element_type=F32))
    score_into(0, 0)
    for h, gate in enumerate(gates):
        bidx = lax.broadcasted_iota(jnp.int32, gate.shape, 0)
        past = bidx < j
        g = jnp.where(past, gate, -jnp.inf)
        rank = jnp.zeros(gate.shape, jnp.int32)
        for other in range(nblk):
            go = g[other:other + 1, :]
            beats = jnp.where(go > g, 1, jnp.where(go == g, jnp.where(bidx > other, 1, 0), 0))
            rank = rank + beats
        bias_scr[h] = jnp.where(past, jnp.where(rank < MOBA_TOPK, 0.0, NEG_INF),
                                jnp.where(bidx == j, 0.0, NEG_INF))

    def pair_body(t, state):
        first = 2 * t
        score_into(1, first + 1)
        state = attend(0, no_mask, first, state)
        score_into(0, first + 2)
        return attend(1, no_mask, first + 1, state)

    def odd_body(state):
        score_into(1, j)
        return attend(0, no_mask, j - 1, state)

    acc_scr[...] = jnp.zeros_like(acc_scr)
    init = tuple(jnp.full((1, blk), NEG_INF, F32) for _ in range(heads))
    state = lax.fori_loop(0, lax.shift_right_logical(j, 1), pair_body, init)
    odd = lax.bitwise_and(j, 1)
    state = lax.cond(odd == 1, odd_body, lambda st: st, state)

    def causal_mask(s, row0):
        kpos = row0 + lax.broadcasted_iota(jnp.int32, s.shape, 0)
        qpos = lax.broadcasted_iota(jnp.int32, s.shape, 1)
        return jnp.where(kpos <= qpos, s, NEG_INF)

    attend(odd, causal_mask, j, state)
    for h, hs in enumerate(head_slices):
        o_ref[qrows, hs] = (acc_scr[h, 0:HEAD_DIM, :] / acc_scr[h, HEAD_DIM:HEAD_DIM + 1, :]
                        ).T.astype(o_ref.dtype)


def _moba(qk, v_t, kmean, batch, seq, heads=4, tiles=4):
    n = batch * seq
    nblk = seq // MOBA_BLOCK
    groups = N_HEADS // heads
    width = heads * HEAD_DIM
    steps = nblk // tiles
    rows = tiles * MOBA_BLOCK
    return pl.pallas_call(
        functools.partial(_moba_kernel, heads=heads, tiles=tiles),
        grid=(batch, groups, steps),
        in_specs=[pl.BlockSpec((rows, width), lambda b, h, j: (b * steps + j, h)),
                  pl.BlockSpec((seq, width), lambda b, h, j: (b, groups + h)),
                  pl.BlockSpec((width, seq), lambda b, h, j: (h, b)),
                  pl.BlockSpec((nblk, width), lambda b, h, j: (b, groups + h))],
        out_specs=pl.BlockSpec((rows, width), lambda b, h, j: (b * steps + j, h)),
        out_shape=jax.ShapeDtypeStruct((n, D_ATTN), BF16),
        scratch_shapes=[pltpu.VMEM((heads, nblk, MOBA_BLOCK), F32),
                        pltpu.VMEM((2, heads, MOBA_BLOCK, MOBA_BLOCK), F32),
                        pltpu.VMEM((heads, HEAD_DIM + BF16_ROWS, MOBA_BLOCK), F32)],
        name="moba",
        compiler_params=_params(("parallel", "parallel", "arbitrary"), 56),
    )(qk, qk, v_t, kmean)


def _merge_kernel(gh_ref, o_ref, wr_ref, wa_ref, gr_ref, ga_ref, out_ref, wr_scr, wa_scr):
    _cast_weights(wr_ref, wr_scr)
    _cast_weights(wa_ref, wa_scr)
    rnn = jnp.dot(gh_ref[...], wr_scr[...], preferred_element_type=F32)
    att = jnp.dot(o_ref[...], wa_scr[...], preferred_element_type=F32)
    out_ref[...] = (gr_ref[...].astype(F32) * rnn + ga_ref[...].astype(F32) * att).astype(out_ref.dtype)


def _merge(gh, o, w_rnn, w_attn, gates, tm=1024, tn=512):
    m, k = gh.shape
    nj = D_MODEL // tn
    a_spec = pl.BlockSpec((tm, k), lambda j, i: (i, 0))
    w_spec = pl.BlockSpec((k, tn), lambda j, i: (0, j))
    return pl.pallas_call(
        _merge_kernel,
        grid=(nj, m // tm),
        in_specs=[a_spec, a_spec, w_spec, w_spec,
                  pl.BlockSpec((tm, tn), lambda j, i: (i, j)),
                  pl.BlockSpec((tm, tn), lambda j, i: (i, nj + j))],
        out_specs=pl.BlockSpec((tm, tn), lambda j, i: (i, j)),
        out_shape=jax.ShapeDtypeStruct((m, D_MODEL), BF16),
        scratch_shapes=[pltpu.VMEM((k, tn), BF16), pltpu.VMEM((k, tn), BF16)],
        name="merge",
        compiler_params=_params(("parallel", "arbitrary"), 48),
    )(gh, o, w_rnn, w_attn, gates, gates)


def _memkv_kernel(mem_ref, g_ref, w_ref, o_ref):
    hn = _rms(mem_ref[...], g_ref[...]).astype(BF16)
    o_ref[...] = jnp.dot(hn, w_ref[...], preferred_element_type=F32).astype(o_ref.dtype)


def _memkv(mem2d, g, w_kv, mem_len):
    m, d = mem2d.shape
    return pl.pallas_call(
        _memkv_kernel,
        grid=(m // mem_len,),
        in_specs=[pl.BlockSpec((mem_len, d), lambda i: (i, 0)),
                  pl.BlockSpec((1, d), lambda i: (0, 0)),
                  pl.BlockSpec((d, 2 * D_MEM), lambda i: (0, 0))],
        out_specs=pl.BlockSpec((mem_len, 2 * D_MEM), lambda i: (i, 0)),
        out_shape=jax.ShapeDtypeStruct((m, 2 * D_MEM), BF16),
        name="mem_kv",
        compiler_params=_params(("parallel",), 32),
    )(mem2d, g.reshape(1, d), w_kv)


def _xattn_kernel(x_ref, g_ref, wq_ref, kv_ref, wo_ref, o_ref):
    x = x_ref[...]
    hn = _rms(x, g_ref[...]).astype(BF16)
    q = jnp.dot(hn, wq_ref[...], preferred_element_type=F32).astype(BF16)
    scale = MEM_HEAD_DIM ** -0.5
    heads = []
    for hd in range(MEM_HEADS):
        sl = slice(hd * MEM_HEAD_DIM, (hd + 1) * MEM_HEAD_DIM)
        kh = kv_ref[:, sl]
        vh = kv_ref[:, D_MEM + hd * MEM_HEAD_DIM:D_MEM + (hd + 1) * MEM_HEAD_DIM]
        s = lax.dot_general(q[:, sl], kh, _NT, preferred_element_type=F32) * scale
        m = jnp.max(s, axis=-1, keepdims=True)
        p = jnp.exp(s - m)
        l = jnp.sum(p, axis=-1, keepdims=True)
        oh = jnp.dot(p.astype(BF16), vh, preferred_element_type=F32) / l
        heads.append(oh.astype(BF16))
    o_all = jnp.concatenate(heads, axis=-1)
    o_ref[...] = x + jnp.dot(o_all, wo_ref[...], preferred_element_type=F32)


def _xattn(x, g, w_q, kv, w_o, seq, mem_len, tm=512):
    m, d = x.shape
    per_batch = seq // tm
    return pl.pallas_call(
        _xattn_kernel,
        grid=(m // tm,),
        in_specs=[pl.BlockSpec((tm, d), lambda i: (i, 0)),
                  pl.BlockSpec((1, d), lambda i: (0, 0)),
                  pl.BlockSpec((d, D_MEM), lambda i: (0, 0)),
                  pl.BlockSpec((mem_len, 2 * D_MEM), lambda i: (i // per_batch, 0)),
                  pl.BlockSpec((D_MEM, d), lambda i: (0, 0))],
        out_specs=pl.BlockSpec((tm, d), lambda i: (i, 0)),
        out_shape=jax.ShapeDtypeStruct((m, d), F32),
        name="xattn",
        compiler_params=_params(("parallel",), 48),
    )(x, g.reshape(1, d), w_q, kv, w_o)


def _ffn_kernel(x_ref, g_ref, wg_ref, wu_ref, wd_ref, gf_ref, o_ref, h_scr):
    f = pl.program_id(1)

    @pl.when(f == 0)
    def _():
        h_scr[...] = _rms(x_ref[...], g_ref[...]).astype(BF16)
        o_ref[...] = jnp.zeros_like(o_ref)

    hn = h_scr[...]
    a = jnp.dot(hn, wg_ref[...], preferred_element_type=F32)
    b = jnp.dot(hn, wu_ref[...], preferred_element_type=F32)
    act = (jax.nn.silu(a) * b).astype(BF16)
    o_ref[...] += jnp.dot(act, wd_ref[...], preferred_element_type=F32)

    @pl.when(f == pl.num_programs(1) - 1)
    def _():
        o_ref[...] = _rms(x_ref[...] + o_ref[...], gf_ref[...])


def _ffn(x, g, w_gate, w_up, w_down, g_final, tm=1024, tf=512):
    m, d = x.shape
    d_ff = w_gate.shape[1]
    return pl.pallas_call(
        _ffn_kernel,
        grid=(m // tm, d_ff // tf),
        in_specs=[pl.BlockSpec((tm, d), lambda i, f: (i, 0)),
                  pl.BlockSpec((1, d), lambda i, f: (0, 0)),
                  pl.BlockSpec((d, tf), lambda i, f: (0, f)),
                  pl.BlockSpec((d, tf), lambda i, f: (0, f)),
                  pl.BlockSpec((tf, d), lambda i, f: (f, 0)),
                  pl.BlockSpec((1, d), lambda i, f: (0, 0))],
        out_specs=pl.BlockSpec((tm, d), lambda i, f: (i, 0)),
        out_shape=jax.ShapeDtypeStruct((m, d), F32),
        scratch_shapes=[pltpu.VMEM((tm, d), BF16)],
        name="ffn",
        compiler_params=_params(("parallel", "arbitrary"), 63),
    )(x, g.reshape(1, d), w_gate, w_up, w_down, g_final.reshape(1, d))


def _layer(x2d, mem2d, cosf, sinf, batch, seq, mem_len, p):
    n = x2d.shape[0]
    w_in = p["w_in"]
    hn = _norm_bf16(x2d, p["norm_mix_g"])

    c0 = 0
    wide = 1024
    xy = _matmul(_mm_plain_kernel, "proj_xy", hn, w_in, col_off=c0, n_cols=2 * D_RNN, out_dtype=F32,
                 tn=wide)
    c0 += 2 * D_RNN
    tm, tn = 1024, wide
    rope_block = ((tm, LANES), lambda i, j: (i, 0))
    qk, means = _matmul(
        _mm_rope_kernel, "proj_qk", hn, w_in, col_off=c0, n_cols=2 * D_ATTN, out_dtype=BF16,
        tm=tm, tn=tn, extra=(cosf, sinf), extra_blocks=(rope_block, rope_block),
        extra_out_shape=(jax.ShapeDtypeStruct((n // tm, tm // MOBA_BLOCK, 2 * D_ATTN), F32),),
        extra_out_blocks=(((1, tm // MOBA_BLOCK, tn), lambda i, j: (i, 0, j)),))
    c0 += 2 * D_ATTN
    v_t = _matmul(_mm_transposed_kernel, "proj_v", hn, w_in, col_off=c0, n_cols=D_ATTN,
                  out_dtype=BF16, tn=wide, transposed_out=True)
    c0 += D_ATTN
    gates = _matmul(_mm_sigmoid_kernel, "proj_gates", hn, w_in, col_off=c0, n_cols=2 * D_MODEL,
                    out_dtype=BF16, tn=wide)

    gh = _rglru(xy, p["conv_w"], p["conv_b"], p["lru_w_a"], p["lru_b_a"], p["lru_w_i"],
                p["lru_b_i"], p["lru_lambda"], batch, seq)
    kmean = means.reshape(n // MOBA_BLOCK, 2 * D_ATTN)
    o = _moba(qk, v_t, kmean, batch, seq)

    merged = _merge(gh, o, p["w_rnn_proj"], p["w_attn_proj"], gates)
    x1 = _matmul(_mm_residual_kernel, "mix_out", merged, p["w_mix_out"].astype(BF16), col_off=0,
                 n_cols=D_MODEL, out_dtype=F32, tm=512, tn=D_MODEL,
                 extra=(x2d,), extra_blocks=(((512, D_MODEL), lambda i, j: (i, j)),))

    kv = _memkv(mem2d, p["norm_mem_g"], p["w_xkv"].astype(BF16), mem_len)
    x2 = _xattn(x1, p["norm_xq_g"], p["w_xq"].astype(BF16), kv, p["w_xo"].astype(BF16), seq, mem_len)
    return x2


def kernel(x, mem, positions, norm_mix_g, w_in, conv_w, conv_b, lru_w_a, lru_b_a, lru_w_i, lru_b_i,
           lru_lambda, w_rnn_proj, w_attn_proj, w_mix_out, norm_xq_g, norm_mem_g, w_xq, w_xkv, w_xo,
           norm_ffn_g, w_ffn_gate, w_ffn_up, w_ffn_down, norm_final_g):
    batch, seq, d = x.shape
    mem_len = mem.shape[1]
    assert w_in.shape[0] == 1, "only DEPTH == 1 is supported"
    x2d = x.reshape(batch * seq, d)
    mem2d = mem.reshape(batch * mem_len, d)
    cosf, sinf = _rope_tables(positions)
    p = dict(norm_mix_g=norm_mix_g[0], w_in=w_in[0], conv_w=conv_w[0], conv_b=conv_b[0],
             lru_w_a=lru_w_a[0], lru_b_a=lru_b_a[0], lru_w_i=lru_w_i[0], lru_b_i=lru_b_i[0],
             lru_lambda=lru_lambda[0], w_rnn_proj=w_rnn_proj[0], w_attn_proj=w_attn_proj[0],
             w_mix_out=w_mix_out[0], norm_xq_g=norm_xq_g[0], norm_mem_g=norm_mem_g[0],
             w_xq=w_xq[0], w_xkv=w_xkv[0], w_xo=w_xo[0])
    x2 = _layer(x2d, mem2d, cosf, sinf, batch, seq, mem_len, p)
    out = _ffn(x2, norm_ffn_g[0], w_ffn_gate[0].astype(BF16), w_ffn_up[0].astype(BF16),
               w_ffn_down[0].astype(BF16), norm_final_g)
    return out.reshape(batch, seq, d)
```

```python
import functools

import jax
import jax.numpy as jnp
from jax import lax
from jax.experimental import pallas as pl
from jax.experimental.pallas import tpu as pltpu

D_MODEL = 2048
N_HEADS = 16
HEAD_DIM = 128
D_ATTN = N_HEADS * HEAD_DIM
MOBA_BLOCK = 256
MOBA_TOPK = 3
ROPE_THETA = 500000.0
ROT_DIM = HEAD_DIM // 4
ROT_HALF = ROT_DIM // 2
D_RNN = 2048
N_RNN_BLOCKS = 16
RNN_BLOCK = D_RNN // N_RNN_BLOCKS
CONV_WIDTH = 4
LRU_C = 8.0
MEM_HEADS = 4
MEM_HEAD_DIM = 128
D_MEM = MEM_HEADS * MEM_HEAD_DIM
RMS_EPS = 1e-6
NEG_INF = -1e30
LOG2_E = 1.4426950408889634
SOFTMAX_LOG2_SCALE = (HEAD_DIM ** -0.5) * LOG2_E

LANES = 128
SUBLANES = 8
BF16_ROWS = 16
MXU_COLS = 2 * 256
ROT_PARTNER = LANES // 2
MIB = 1024 * 1024

BF16 = jnp.bfloat16
F32 = jnp.float32

_NT = (((1,), (1,)), ((), ()))
_TN = (((0,), (0,)), ((), ()))


def _params(semantics, vmem_mib):
    return pltpu.CompilerParams(dimension_semantics=semantics,
                                vmem_limit_bytes=vmem_mib * MIB)


def _rms(x, g):
    ms = jnp.mean(x * x, axis=-1, keepdims=True)
    return x * lax.rsqrt(ms + RMS_EPS) * g


def _norm_kernel(x_ref, g_ref, o_ref):
    o_ref[...] = _rms(x_ref[...], g_ref[...]).astype(o_ref.dtype)


def _norm_bf16(x, g, tm=512):
    m, d = x.shape
    return pl.pallas_call(
        _norm_kernel,
        grid=(m // tm,),
        in_specs=[pl.BlockSpec((tm, d), lambda i: (i, 0)),
                  pl.BlockSpec((1, d), lambda i: (0, 0))],
        out_specs=pl.BlockSpec((tm, d), lambda i: (i, 0)),
        out_shape=jax.ShapeDtypeStruct((m, d), BF16),
        name="norm_mix",
        compiler_params=_params(("parallel",), 32),
    )(x, g.reshape(1, d))


def _rope_table_kernel(pos_ref, invf_ref, cos_ref, sin_ref):
    ang = pos_ref[...] * invf_ref[...]
    lane = lax.broadcasted_iota(jnp.int32, ang.shape, 1)
    s = jnp.sin(ang)
    cos_ref[...] = jnp.cos(ang)
    sin_ref[...] = jnp.where(lane < ROT_PARTNER, -s, s)


def _rope_head_layout():
    split = ROT_DIM + ROT_PARTNER - ROT_HALF
    return [(0, ROT_HALF), (ROT_DIM, split), (ROT_HALF, ROT_DIM), (split, HEAD_DIM)]


def _cast_weights(w_ref, w_scr):
    @pl.when(pl.program_id(1) == 0)
    def _():
        w_scr[...] = w_ref[...].astype(w_scr.dtype)


def _cast_qk_weights(w_ref, w_scr):
    @pl.when(pl.program_id(1) == 0)
    def _():
        rows = 256
        lane = lax.broadcasted_iota(jnp.int32, (rows, HEAD_DIM), 1)
        for r0 in range(0, w_ref.shape[0], rows):
            for h0 in range(0, w_ref.shape[1], HEAD_DIM):
                x = w_ref[r0:r0 + rows, h0:h0 + HEAD_DIM]
                out, at = x, 0
                for start, stop in _rope_head_layout():
                    if start != at:
                        moved = pltpu.roll(x, (at - start) % HEAD_DIM, 1)
                        out = jnp.where((lane >= at) & (lane < at + stop - start), moved, out)
                    at += stop - start
                w_scr[r0:r0 + rows, h0:h0 + HEAD_DIM] = out.astype(w_scr.dtype)


def _rope_tables(positions, tr=1024):
    n = positions.size
    pos = positions.astype(F32).reshape(n, 1)
    inv_freq = jnp.power(ROPE_THETA, -jnp.arange(ROT_HALF, dtype=F32) / ROT_HALF)
    gap = jnp.zeros((ROT_PARTNER - ROT_HALF,), F32)
    invf = jnp.concatenate([inv_freq, gap, inv_freq, gap]).reshape(1, LANES)
    return pl.pallas_call(
        _rope_table_kernel,
        grid=(n // tr,),
        in_specs=[pl.BlockSpec((tr, 1), lambda i: (i, 0)),
                  pl.BlockSpec((1, LANES), lambda i: (0, 0))],
        out_specs=[pl.BlockSpec((tr, LANES), lambda i: (i, 0)),
                   pl.BlockSpec((tr, LANES), lambda i: (i, 0))],
        out_shape=[jax.ShapeDtypeStruct((n, LANES), F32),
                   jax.ShapeDtypeStruct((n, LANES), F32)],
        name="rope_tables",
        compiler_params=_params(("parallel",), 32),
    )(pos, invf)


def _mm_plain_kernel(a_ref, w_ref, o_ref, w_scr):
    _cast_weights(w_ref, w_scr)
    acc = jnp.dot(a_ref[...], w_scr[...], preferred_element_type=F32)
    o_ref[...] = acc.astype(o_ref.dtype)


def _mm_transposed_kernel(a_ref, w_ref, o_ref, w_scr):
    _cast_weights(w_ref, w_scr)
    for c0 in range(0, w_scr.shape[1], MXU_COLS):
        acc = jnp.dot(a_ref[...], w_scr[:, c0:c0 + MXU_COLS], preferred_element_type=F32)
        o_ref[c0:c0 + MXU_COLS, :] = acc.T.astype(o_ref.dtype)


def _mm_sigmoid_kernel(a_ref, w_ref, o_ref, w_scr):
    _cast_weights(w_ref, w_scr)
    for c0 in range(0, w_scr.shape[1], MXU_COLS):
        acc = jnp.dot(a_ref[...], w_scr[:, c0:c0 + MXU_COLS], preferred_element_type=F32)
        o_ref[:, c0:c0 + MXU_COLS] = (0.5 * jnp.tanh(0.5 * acc) + 0.5).astype(o_ref.dtype)


def _mm_residual_kernel(a_ref, w_ref, r_ref, o_ref, w_scr):
    _cast_weights(w_ref, w_scr)
    acc = jnp.dot(a_ref[...], w_scr[...], preferred_element_type=F32)
    o_ref[...] = r_ref[...] + acc


def _mm_rope_kernel(a_ref, w_ref, cos_ref, sin_ref, o_ref, mean_ref, w_scr):
    _cast_qk_weights(w_ref, w_scr)
    tm, tn = o_ref.shape
    q_scale = jnp.where(pl.program_id(0) < D_ATTN // tn, SOFTMAX_LOG2_SCALE, 1.0)
    cosf = cos_ref[...] * q_scale
    sinf = sin_ref[...] * q_scale
    for c0 in range(0, tn, MXU_COLS):
        acc = jnp.dot(a_ref[...], w_scr[:, c0:c0 + MXU_COLS], preferred_element_type=F32)
        for h0 in range(0, MXU_COLS, HEAD_DIM):
            a = acc[:, h0:h0 + HEAD_DIM]
            r = a * cosf + pltpu.roll(a, ROT_PARTNER, 1) * sinf
            cols = slice(c0 + h0, c0 + h0 + HEAD_DIM)
            o_ref[:, cols] = r.astype(o_ref.dtype)
            mean_ref[0, :, cols] = jnp.mean(r.reshape(tm // MOBA_BLOCK, MOBA_BLOCK, HEAD_DIM), axis=1)


def _matmul(kernel, name, a, w, *, col_off, n_cols, out_dtype, tm=1024, tn=512,
            extra=(), extra_blocks=(), extra_out_shape=(), extra_out_blocks=(), vmem_mib=48,
            transposed_out=False):
    m, k = a.shape
    off = col_off // tn
    spec = lambda shape, fn: pl.BlockSpec(shape, lambda j, i: fn(i, j))
    if transposed_out:
        main_shape, main_spec = (n_cols, m), spec((tn, tm), lambda i, j: (j, i))
    else:
        main_shape, main_spec = (m, n_cols), spec((tm, tn), lambda i, j: (i, j))
    out_shape = [jax.ShapeDtypeStruct(main_shape, out_dtype)] + list(extra_out_shape)
    out_specs = [main_spec] + [spec(*blk) for blk in extra_out_blocks]
    res = pl.pallas_call(
        kernel,
        grid=(n_cols // tn, m // tm),
        in_specs=[spec((tm, k), lambda i, j: (i, 0)),
                  spec((k, tn), lambda i, j: (0, j + off))] + [spec(*blk) for blk in extra_blocks],
        out_specs=out_specs,
        out_shape=out_shape,
        scratch_shapes=[pltpu.VMEM((k, tn), BF16)],
        name=name,
        compiler_params=_params(("parallel", "arbitrary"), vmem_mib),
    )(a, w, *extra)
    return res if extra_out_shape else res[0]


def _rglru_kernel(xr_ref, yr_ref, cw_ref, cb_ref, wa_ref, wi_ref, ba_ref, bi_ref, lam_ref,
                  o_ref, xbuf, a_scr, u_scr, h_scr, nat_scr, tail_scr, hc_scr):
    ts, tc = xr_ref.shape
    nv = ts // SUBLANES
    halo = (CONV_WIDTH - 1) * SUBLANES
    t = pl.program_id(2)

    def grp(g):
        return slice(halo + g * SUBLANES, halo + (g + 1) * SUBLANES)

    @pl.when(t == 0)
    def _():
        tail_scr[...] = jnp.zeros_like(tail_scr)
        hc_scr[...] = jnp.zeros_like(hc_scr)

    lane_blocks = [slice(cb * LANES, (cb + 1) * LANES) for cb in range(tc // LANES)]
    per_seg = nv // SUBLANES
    for cb, ls in enumerate(lane_blocks):
        for s in range(SUBLANES):
            for q in range(per_seg):
                t0 = s * nv + q * SUBLANES
                dst = pl.ds(halo + q * SUBLANES * SUBLANES + s, SUBLANES, stride=SUBLANES)
                xbuf[cb, dst, :] = xr_ref[t0:t0 + SUBLANES, ls]
    sub = lax.broadcasted_iota(jnp.int32, (SUBLANES, LANES), 0)
    for cb, ls in enumerate(lane_blocks):
        for d in range(1, CONV_WIDTH):
            keep = slice((CONV_WIDTH - 1 - d) * SUBLANES, (CONV_WIDTH - d) * SUBLANES)
            cur = xbuf[cb, grp(nv - d), :]
            xbuf[cb, grp(-d), :] = pltpu.roll(
                jnp.where(sub == SUBLANES - 1, tail_scr[keep, ls], cur), 1, 0)
            tail_scr[keep, ls] = cur

    half_a = (-0.5 * LRU_C) * jax.nn.softplus(-lam_ref[...])
    for nb, sl in enumerate(lane_blocks):
        xb = jnp.zeros((ts, LANES), F32) + cb_ref[:, sl]
        for kk in range(CONV_WIDTH):
            start = halo - (CONV_WIDTH - 1 - kk) * SUBLANES
            xb = xb + cw_ref[kk:kk + 1, sl] * xbuf[nb, start:start + ts, :]
        xb16 = xb.astype(BF16)
        zr = jnp.dot(xb16, wa_ref[nb], preferred_element_type=F32) + ba_ref[:, sl]
        zi = jnp.dot(xb16, wi_ref[nb], preferred_element_type=F32) + bi_ref[:, sl]
        log_a = half_a[:, sl] * jnp.tanh(0.5 * zr) + half_a[:, sl]
        ig = 0.5 * jnp.tanh(0.5 * zi) + 0.5
        a_scr[:, sl] = jnp.exp(log_a)
        th = jnp.tanh(log_a)
        y = -2.0 * th / (1.0 - th)
        u_scr[:, sl] = jnp.where(y > 0.0, y * lax.rsqrt(y), 0.0) * (ig * xb)

    h_end = jnp.zeros((SUBLANES, tc), F32)
    p_end = jnp.ones((SUBLANES, tc), F32)
    for v in range(nv):
        rows = slice(v * SUBLANES, (v + 1) * SUBLANES)
        a = a_scr[rows, :]
        h_end = a * h_end + u_scr[rows, :]
        p_end = a * p_end
        h_scr[rows, :] = h_end
        a_scr[rows, :] = p_end
    h_in = hc_scr[...]
    entering = []
    for s in range(SUBLANES):
        entering.append(h_in)
        h_in = h_end[s:s + 1, :] + p_end[s:s + 1, :] * h_in
    hc_scr[...] = h_in
    h_enter = jnp.concatenate(entering, axis=0)

    for v in range(nv):
        rows = slice(v * SUBLANES, (v + 1) * SUBLANES)
        h_v = h_scr[rows, :] + a_scr[rows, :] * h_enter
        for cb, ls in enumerate(lane_blocks):
            nat_scr[cb, pl.ds(v, SUBLANES, stride=nv), :] = h_v[:, ls]
    for cb, ls in enumerate(lane_blocks):
        o_ref[:, ls] = (jax.nn.gelu(yr_ref[:, ls]) * nat_scr[cb]).astype(o_ref.dtype)


def _rglru(xy, conv_w, conv_b, w_a, b_a, w_i, b_i, lam, batch, seq, ts=512, tc=512):
    assert RNN_BLOCK == LANES, "the kernel walks gate blocks and 128-lane blocks together"
    n = batch * seq
    nt = seq // ts
    ncb = D_RNN // tc
    halo = (CONV_WIDTH - 1) * SUBLANES
    row = lambda v: v.reshape(1, D_RNN)
    vec_spec = pl.BlockSpec((1, tc), lambda b, c, t: (0, c))
    gate_spec = pl.BlockSpec((tc // RNN_BLOCK, RNN_BLOCK, RNN_BLOCK), lambda b, c, t: (c, 0, 0))
    return pl.pallas_call(
        _rglru_kernel,
        grid=(batch, ncb, nt),
        in_specs=[pl.BlockSpec((ts, tc), lambda b, c, t: (b * nt + t, c)),
                  pl.BlockSpec((ts, tc), lambda b, c, t: (b * nt + t, ncb + c)),
                  pl.BlockSpec((CONV_WIDTH, tc), lambda b, c, t: (0, c)),
                  vec_spec, gate_spec, gate_spec, vec_spec, vec_spec, vec_spec],
        out_specs=pl.BlockSpec((ts, tc), lambda b, c, t: (b * nt + t, c)),
        out_shape=jax.ShapeDtypeStruct((n, D_RNN), BF16),
        scratch_shapes=[pltpu.VMEM((tc // LANES, halo + ts, LANES), F32),
                        pltpu.VMEM((ts, tc), F32),
                        pltpu.VMEM((ts, tc), F32),
                        pltpu.VMEM((ts, tc), F32),
                        pltpu.VMEM((tc // LANES, ts, LANES), F32),
                        pltpu.VMEM((halo, tc), F32),
                        pltpu.VMEM((1, tc), F32)],
        name="rglru",
        compiler_params=_params(("parallel", "parallel", "arbitrary"), 32),
    )(xy, xy, conv_w, row(conv_b), w_a.astype(BF16), w_i.astype(BF16), row(b_a), row(b_i), row(lam))


def _moba_kernel(q_ref, k_ref, v_ref, km_ref, o_ref, bias_scr, s_scr, acc_scr, *, heads, tiles):
    first = pl.program_id(2) * tiles

    def one(u, carry):
        qrows = pl.ds(pl.multiple_of(u * MOBA_BLOCK, MOBA_BLOCK), MOBA_BLOCK)
        _moba_tile(first + u, qrows, q_ref, k_ref, v_ref, km_ref, o_ref, bias_scr, s_scr, acc_scr, heads)
        return carry

    lax.fori_loop(0, tiles, one, 0)


def _moba_tile(j, qrows, q_ref, k_ref, v_ref, km_ref, o_ref, bias_scr, s_scr, acc_scr, heads):
    nblk = km_ref.shape[0]
    blk = MOBA_BLOCK
    head_slices = [slice(h * HEAD_DIM, (h + 1) * HEAD_DIM) for h in range(heads)]

    def score_into(slot, block):
        start = pl.multiple_of(block * blk, blk)
        for h, hs in enumerate(head_slices):
            s_scr[slot, h] = lax.dot_general(k_ref[pl.ds(start, blk), hs], q_ref[qrows, hs], _NT,
                                             preferred_element_type=F32)

    ones_rows = jnp.ones((BF16_ROWS, blk), BF16)

    def attend(slot, mask, block, state):
        start = pl.multiple_of(block * blk, blk)
        half = blk // 2
        soft = []
        for h in range(heads):
            m = state[h]
            unselected = bias_scr[h, pl.ds(block, 1), :] < 0.0
            s_lo = mask(s_scr[slot, h, 0:half, :], 0)
            s_hi = mask(s_scr[slot, h, half:blk, :], half)
            m_blk = jnp.maximum(jnp.max(s_lo, axis=0, keepdims=True), jnp.max(s_hi, axis=0, keepdims=True))
            m_new = jnp.where(unselected, m, jnp.maximum(m, m_blk))
            alpha = jnp.exp2(m - m_new)
            p = jnp.exp2(mask(s_scr[slot, h], 0) - jnp.where(unselected, -NEG_INF, m_new))
            soft.append((m_new, alpha, p.astype(BF16)))
        out = []
        for h, hs in enumerate(head_slices):
            m_new, alpha, p = soft[h]
            vt = jnp.concatenate([v_ref[hs, pl.ds(start, blk)], ones_rows], axis=0)
            pv = jnp.dot(vt, p, preferred_element_type=F32)
            acc_scr[h] = alpha * acc_scr[h] + pv
            out.append(m_new)
        return tuple(out)

    no_mask = lambda s, row0: s

    gates = []
    for hs in head_slices:
        q = q_ref[qrows, hs]
        km = km_ref[:, hs]
        km_hi = km.astype(BF16)
        km_lo = (km - km_hi.astype(F32)).astype(BF16)
        gates.append(lax.dot_general(km_hi, q, _NT, preferred_element_type=F32)
                     + lax.dot_general(km_lo, q, _NT, preferred_element_type=F32))
    score_into(0, 0)
    for h, gate in enumerate(gates):
        bidx = lax.broadcasted_iota(jnp.int32, gate.shape, 0)
        past = bidx < j
        g = jnp.where(past, gate, -jnp.inf)
        rank = jnp.zeros(gate.shape, jnp.int32)
        for other in range(nblk):
            go = g[other:other + 1, :]
            beats = jnp.where(go > g, 1, jnp.where(go == g, jnp.where(bidx > other, 1, 0), 0))
            rank = rank + beats
        bias_scr[h] = jnp.where(past, jnp.where(rank < MOBA_TOPK, 0.0, NEG_INF),
                                jnp.where(bidx == j, 0.0, NEG_INF))

    def pair_body(t, state):
        first = 2 * t
        score_into(1, first + 1)
        state = attend(0, no_mask, first, state)
        score_into(0, first + 2)
        return attend(1, no_mask, first + 1, state)

    def odd_body(state):
        score_into(1, j)
        return attend(0, no_mask, j - 1, state)

    acc_scr[...] = jnp.zeros_like(acc_scr)
    init = tuple(jnp.full((1, blk), NEG_INF, F32) for _ in range(heads))
    state = lax.fori_loop(0, lax.shift_right_logical(j, 1), pair_body, init)
    odd = lax.bitwise_and(j, 1)
    state = lax.cond(odd == 1, odd_body, lambda st: st, state)

    def causal_mask(s, row0):
        kpos = row0 + lax.broadcasted_iota(jnp.int32, s.shape, 0)
        qpos = lax.broadcasted_iota(jnp.int32, s.shape, 1)
        return jnp.where(kpos <= qpos, s, NEG_INF)

    attend(odd, causal_mask, j, state)
    for h, hs in enumerate(head_slices):
        o_ref[qrows, hs] = (acc_scr[h, 0:HEAD_DIM, :] / acc_scr[h, HEAD_DIM:HEAD_DIM + 1, :]
                        ).T.astype(o_ref.dtype)


def _moba(qk, v_t, kmean, batch, seq, heads=4, tiles=8):
    n = batch * seq
    nblk = seq // MOBA_BLOCK
    groups = N_HEADS // heads
    width = heads * HEAD_DIM
    steps = nblk // tiles
    rows = tiles * MOBA_BLOCK
    return pl.pallas_call(
        functools.partial(_moba_kernel, heads=heads, tiles=tiles),
        grid=(batch, groups, steps),
        in_specs=[pl.BlockSpec((rows, width), lambda b, h, j: (b * steps + j, h)),
                  pl.BlockSpec((seq, width), lambda b, h, j: (b, groups + h)),
                  pl.BlockSpec((width, seq), lambda b, h, j: (h, b)),
                  pl.BlockSpec((nblk, width), lambda b, h, j: (b, groups + h))],
        out_specs=pl.BlockSpec((rows, width), lambda b, h, j: (b * steps + j, h)),
        out_shape=jax.ShapeDtypeStruct((n, D_ATTN), BF16),
        scratch_shapes=[pltpu.VMEM((heads, nblk, MOBA_BLOCK), F32),
                        pltpu.VMEM((2, heads, MOBA_BLOCK, MOBA_BLOCK), F32),
                        pltpu.VMEM((heads, HEAD_DIM + BF16_ROWS, MOBA_BLOCK), F32)],
        name="moba",
        compiler_params=_params(("parallel", "parallel", "arbitrary"), 56),
    )(qk, qk, v_t, kmean)


def _merge_kernel(gh_ref, o_ref, wr_ref, wa_ref, gr_ref, ga_ref, out_ref, wr_scr, wa_scr):
    _cast_weights(wr_ref, wr_scr)
    _cast_weights(wa_ref, wa_scr)
    rnn = jnp.dot(gh_ref[...], wr_scr[...], preferred_element_type=F32)
    att = jnp.dot(o_ref[...], wa_scr[...], preferred_element_type=F32)
    out_ref[...] = (gr_ref[...].astype(F32) * rnn + ga_ref[...].astype(F32) * att).astype(out_ref.dtype)


def _merge(gh, o, w_rnn, w_attn, gates, tm=1024, tn=512):
    m, k = gh.shape
    nj = D_MODEL // tn
    a_spec = pl.BlockSpec((tm, k), lambda j, i: (i, 0))
    w_spec = pl.BlockSpec((k, tn), lambda j, i: (0, j))
    return pl.pallas_call(
        _merge_kernel,
        grid=(nj, m // tm),
        in_specs=[a_spec, a_spec, w_spec, w_spec,
                  pl.BlockSpec((tm, tn), lambda j, i: (i, j)),
                  pl.BlockSpec((tm, tn), lambda j, i: (i, nj + j))],
        out_specs=pl.BlockSpec((tm, tn), lambda j, i: (i, j)),
        out_shape=jax.ShapeDtypeStruct((m, D_MODEL), BF16),
        scratch_shapes=[pltpu.VMEM((k, tn), BF16), pltpu.VMEM((k, tn), BF16)],
        name="merge",
        compiler_params=_params(("parallel", "arbitrary"), 48),
    )(gh, o, w_rnn, w_attn, gates, gates)


def _memkv_kernel(mem_ref, g_ref, w_ref, o_ref):
    hn = _rms(mem_ref[...], g_ref[...]).astype(BF16)
    o_ref[...] = jnp.dot(hn, w_ref[...], preferred_element_type=F32).astype(o_ref.dtype)


def _memkv(mem2d, g, w_kv, mem_len):
    m, d = mem2d.shape
    return pl.pallas_call(
        _memkv_kernel,
        grid=(m // mem_len,),
        in_specs=[pl.BlockSpec((mem_len, d), lambda i: (i, 0)),
                  pl.BlockSpec((1, d), lambda i: (0, 0)),
                  pl.BlockSpec((d, 2 * D_MEM), lambda i: (0, 0))],
        out_specs=pl.BlockSpec((mem_len, 2 * D_MEM), lambda i: (i, 0)),
        out_shape=jax.ShapeDtypeStruct((m, 2 * D_MEM), BF16),
        name="mem_kv",
        compiler_params=_params(("parallel",), 32),
    )(mem2d, g.reshape(1, d), w_kv)


def _xattn_kernel(x_ref, g_ref, wq_ref, kv_ref, wo_ref, o_ref):
    x = x_ref[...]
    hn = _rms(x, g_ref[...]).astype(BF16)
    q = jnp.dot(hn, wq_ref[...], preferred_element_type=F32).astype(BF16)
    scale = MEM_HEAD_DIM ** -0.5
    heads = []
    for hd in range(MEM_HEADS):
        sl = slice(hd * MEM_HEAD_DIM, (hd + 1) * MEM_HEAD_DIM)
        kh = kv_ref[:, sl]
        vh = kv_ref[:, D_MEM + hd * MEM_HEAD_DIM:D_MEM + (hd + 1) * MEM_HEAD_DIM]
        s = lax.dot_general(q[:, sl], kh, _NT, preferred_element_type=F32) * scale
        m = jnp.max(s, axis=-1, keepdims=True)
        p = jnp.exp(s - m)
        l = jnp.sum(p, axis=-1, keepdims=True)
        oh = jnp.dot(p.astype(BF16), vh, preferred_element_type=F32) / l
        heads.append(oh.astype(BF16))
    o_all = jnp.concatenate(heads, axis=-1)
    o_ref[...] = x + jnp.dot(o_all, wo_ref[...], preferred_element_type=F32)


def _xattn(x, g, w_q, kv, w_o, seq, mem_len, tm=1024):
    m, d = x.shape
    per_batch = seq // tm
    return pl.pallas_call(
        _xattn_kernel,
        grid=(m // tm,),
        in_specs=[pl.BlockSpec((tm, d), lambda i: (i, 0)),
                  pl.BlockSpec((1, d), lambda i: (0, 0)),
                  pl.BlockSpec((d, D_MEM), lambda i: (0, 0)),
                  pl.BlockSpec((mem_len, 2 * D_MEM), lambda i: (i // per_batch, 0)),
                  pl.BlockSpec((D_MEM, d), lambda i: (0, 0))],
        out_specs=pl.BlockSpec((tm, d), lambda i: (i, 0)),
        out_shape=jax.ShapeDtypeStruct((m, d), F32),
        name="xattn",
        compiler_params=_params(("parallel",), 48),
    )(x, g.reshape(1, d), w_q, kv, w_o)


def _ffn_kernel(x_ref, g_ref, wg_ref, wu_ref, wd_ref, gf_ref, o_ref, h_scr):
    f = pl.program_id(1)

    @pl.when(f == 0)
    def _():
        h_scr[...] = _rms(x_ref[...], g_ref[...]).astype(BF16)
        o_ref[...] = jnp.zeros_like(o_ref)

    hn = h_scr[...]
    a = jnp.dot(hn, wg_ref[...], preferred_element_type=F32)
    b = jnp.dot(hn, wu_ref[...], preferred_element_type=F32)
    act = (jax.nn.silu(a) * b).astype(BF16)
    o_ref[...] += jnp.dot(act, wd_ref[...], preferred_element_type=F32)

    @pl.when(f == pl.num_programs(1) - 1)
    def _():
        o_ref[...] = _rms(x_ref[...] + o_ref[...], gf_ref[...])


def _ffn(x, g, w_gate, w_up, w_down, g_final, tm=1024, tf=512):
    m, d = x.shape
    d_ff = w_gate.shape[1]
    return pl.pallas_call(
        _ffn_kernel,
        grid=(m // tm, d_ff // tf),
        in_specs=[pl.BlockSpec((tm, d), lambda i, f: (i, 0)),
                  pl.BlockSpec((1, d), lambda i, f: (0, 0)),
                  pl.BlockSpec((d, tf), lambda i, f: (0, f)),
                  pl.BlockSpec((d, tf), lambda i, f: (0, f)),
                  pl.BlockSpec((tf, d), lambda i, f: (f, 0)),
                  pl.BlockSpec((1, d), lambda i, f: (0, 0))],
        out_specs=pl.BlockSpec((tm, d), lambda i, f: (i, 0)),
        out_shape=jax.ShapeDtypeStruct((m, d), F32),
        scratch_shapes=[pltpu.VMEM((tm, d), BF16)],
        name="ffn",
        compiler_params=_params(("parallel", "arbitrary"), 63),
    )(x, g.reshape(1, d), w_gate, w_up, w_down, g_final.reshape(1, d))


def _layer(x2d, mem2d, cosf, sinf, batch, seq, mem_len, p):
    n = x2d.shape[0]
    w_in = p["w_in"]
    hn = _norm_bf16(x2d, p["norm_mix_g"])

    c0 = 0
    wide = 1024
    xy = _matmul(_mm_plain_kernel, "proj_xy", hn, w_in, col_off=c0, n_cols=2 * D_RNN, out_dtype=F32,
                 tn=wide)
    c0 += 2 * D_RNN
    tm, tn = 1024, wide
    rope_block = ((tm, LANES), lambda i, j: (i, 0))
    qk, means = _matmul(
        _mm_rope_kernel, "proj_qk", hn, w_in, col_off=c0, n_cols=2 * D_ATTN, out_dtype=BF16,
        tm=tm, tn=tn, extra=(cosf, sinf), extra_blocks=(rope_block, rope_block),
        extra_out_shape=(jax.ShapeDtypeStruct((n // tm, tm // MOBA_BLOCK, 2 * D_ATTN), F32),),
        extra_out_blocks=(((1, tm // MOBA_BLOCK, tn), lambda i, j: (i, 0, j)),))
    c0 += 2 * D_ATTN
    v_t = _matmul(_mm_transposed_kernel, "proj_v", hn, w_in, col_off=c0, n_cols=D_ATTN,
                  out_dtype=BF16, tn=wide, transposed_out=True)
    c0 += D_ATTN
    gates = _matmul(_mm_sigmoid_kernel, "proj_gates", hn, w_in, col_off=c0, n_cols=2 * D_MODEL,
                    out_dtype=BF16, tn=wide)

    gh = _rglru(xy, p["conv_w"], p["conv_b"], p["lru_w_a"], p["lru_b_a"], p["lru_w_i"],
                p["lru_b_i"], p["lru_lambda"], batch, seq)
    kmean = means.reshape(n // MOBA_BLOCK, 2 * D_ATTN)
    o = _moba(qk, v_t, kmean, batch, seq)

    merged = _merge(gh, o, p["w_rnn_proj"], p["w_attn_proj"], gates)
    x1 = _matmul(_mm_residual_kernel, "mix_out", merged, p["w_mix_out"].astype(BF16), col_off=0,
                 n_cols=D_MODEL, out_dtype=F32, tm=512, tn=D_MODEL,
                 extra=(x2d,), extra_blocks=(((512, D_MODEL), lambda i, j: (i, j)),))

    kv = _memkv(mem2d, p["norm_mem_g"], p["w_xkv"].astype(BF16), mem_len)
    x2 = _xattn(x1, p["norm_xq_g"], p["w_xq"].astype(BF16), kv, p["w_xo"].astype(BF16), seq, mem_len)
    return x2


def kernel(x, mem, positions, norm_mix_g, w_in, conv_w, conv_b, lru_w_a, lru_b_a, lru_w_i, lru_b_i,
           lru_lambda, w_rnn_proj, w_attn_proj, w_mix_out, norm_xq_g, norm_mem_g, w_xq, w_xkv, w_xo,
           norm_ffn_g, w_ffn_gate, w_ffn_up, w_ffn_down, norm_final_g):
    batch, seq, d = x.shape
    mem_len = mem.shape[1]
    assert w_in.shape[0] == 1, "only DEPTH == 1 is supported"
    x2d = x.reshape(batch * seq, d)
    mem2d = mem.reshape(batch * mem_len, d)
    cosf, sinf = _rope_tables(positions)
    p = dict(norm_mix_g=norm_mix_g[0], w_in=w_in[0], conv_w=conv_w[0], conv_b=conv_b[0],
             lru_w_a=lru_w_a[0], lru_b_a=lru_b_a[0], lru_w_i=lru_w_i[0], lru_b_i=lru_b_i[0],
             lru_lambda=lru_lambda[0], w_rnn_proj=w_rnn_proj[0], w_attn_proj=w_attn_proj[0],
             w_mix_out=w_mix_out[0], norm_xq_g=norm_xq_g[0], norm_mem_g=norm_mem_g[0],
             w_xq=w_xq[0], w_xkv=w_xkv[0], w_xo=w_xo[0])
    x2 = _layer(x2d, mem2d, cosf, sinf, batch, seq, mem_len, p)
    out = _ffn(x2, norm_ffn_g[0], w_ffn_gate[0].astype(BF16), w_ffn_up[0].astype(BF16),
               w_ffn_down[0].astype(BF16), norm_final_g)
    return out.reshape(batch, seq, d)
```

```python
import functools

import jax
import jax.numpy as jnp
from jax import lax
from jax.experimental import pallas as pl
from jax.experimental.pallas import tpu as pltpu

D_MODEL = 2048
N_HEADS = 16
HEAD_DIM = 128
D_ATTN = N_HEADS * HEAD_DIM
MOBA_BLOCK = 256
MOBA_TOPK = 3
ROPE_THETA = 500000.0
ROT_DIM = HEAD_DIM // 4
ROT_HALF = ROT_DIM // 2
D_RNN = 2048
N_RNN_BLOCKS = 16
RNN_BLOCK = D_RNN // N_RNN_BLOCKS
CONV_WIDTH = 4
LRU_C = 8.0
MEM_HEADS = 4
MEM_HEAD_DIM = 128
D_MEM = MEM_HEADS * MEM_HEAD_DIM
RMS_EPS = 1e-6
NEG_INF = -1e30
LOG2_E = 1.4426950408889634
SOFTMAX_LOG2_SCALE = (HEAD_DIM ** -0.5) * LOG2_E

LANES = 128
SUBLANES = 8
BF16_ROWS = 16
MXU_COLS = 2 * 256
ROT_PARTNER = LANES // 2
MIB = 1024 * 1024

BF16 = jnp.bfloat16
F32 = jnp.float32

_NT = (((1,), (1,)), ((), ()))
_TN = (((0,), (0,)), ((), ()))


def _params(semantics, vmem_mib):
    return pltpu.CompilerParams(dimension_semantics=semantics,
                                vmem_limit_bytes=vmem_mib * MIB)


def _rms(x, g):
    ms = jnp.mean(x * x, axis=-1, keepdims=True)
    return x * lax.rsqrt(ms + RMS_EPS) * g


def _norm_kernel(x_ref, g_ref, o_ref):
    o_ref[...] = _rms(x_ref[...], g_ref[...]).astype(o_ref.dtype)


def _norm_bf16(x, g, tm=512):
    m, d = x.shape
    return pl.pallas_call(
        _norm_kernel,
        grid=(m // tm,),
        in_specs=[pl.BlockSpec((tm, d), lambda i: (i, 0)),
                  pl.BlockSpec((1, d), lambda i: (0, 0))],
        out_specs=pl.BlockSpec((tm, d), lambda i: (i, 0)),
        out_shape=jax.ShapeDtypeStruct((m, d), BF16),
        name="norm_mix",
        compiler_params=_params(("parallel",), 32),
    )(x, g.reshape(1, d))


def _rope_table_kernel(pos_ref, invf_ref, cos_ref, sin_ref):
    ang = pos_ref[...] * invf_ref[...]
    lane = lax.broadcasted_iota(jnp.int32, ang.shape, 1)
    s = jnp.sin(ang)
    cos_ref[...] = jnp.cos(ang)
    sin_ref[...] = jnp.where(lane < ROT_PARTNER, -s, s)


def _rope_head_layout():
    split = ROT_DIM + ROT_PARTNER - ROT_HALF
    return [(0, ROT_HALF), (ROT_DIM, split), (ROT_HALF, ROT_DIM), (split, HEAD_DIM)]


def _cast_weights(w_ref, w_scr):
    @pl.when(pl.program_id(1) == 0)
    def _():
        w_scr[...] = w_ref[...].astype(w_scr.dtype)


def _cast_qk_weights(w_ref, w_scr):
    @pl.when(pl.program_id(1) == 0)
    def _():
        rows = 256
        lane = lax.broadcasted_iota(jnp.int32, (rows, HEAD_DIM), 1)
        for r0 in range(0, w_ref.shape[0], rows):
            for h0 in range(0, w_ref.shape[1], HEAD_DIM):
                x = w_ref[r0:r0 + rows, h0:h0 + HEAD_DIM]
                out, at = x, 0
                for start, stop in _rope_head_layout():
                    if start != at:
                        moved = pltpu.roll(x, (at - start) % HEAD_DIM, 1)
                        out = jnp.where((lane >= at) & (lane < at + stop - start), moved, out)
                    at += stop - start
                w_scr[r0:r0 + rows, h0:h0 + HEAD_DIM] = out.astype(w_scr.dtype)


def _rope_tables(positions, tr=1024):
    n = positions.size
    pos = positions.astype(F32).reshape(n, 1)
    inv_freq = jnp.power(ROPE_THETA, -jnp.arange(ROT_HALF, dtype=F32) / ROT_HALF)
    gap = jnp.zeros((ROT_PARTNER - ROT_HALF,), F32)
    invf = jnp.concatenate([inv_freq, gap, inv_freq, gap]).reshape(1, LANES)
    return pl.pallas_call(
        _rope_table_kernel,
        grid=(n // tr,),
        in_specs=[pl.BlockSpec((tr, 1), lambda i: (i, 0)),
                  pl.BlockSpec((1, LANES), lambda i: (0, 0))],
        out_specs=[pl.BlockSpec((tr, LANES), lambda i: (i, 0)),
                   pl.BlockSpec((tr, LANES), lambda i: (i, 0))],
        out_shape=[jax.ShapeDtypeStruct((n, LANES), F32),
                   jax.ShapeDtypeStruct((n, LANES), F32)],
        name="rope_tables",
        compiler_params=_params(("parallel",), 32),
    )(pos, invf)


def _mm_plain_kernel(a_ref, w_ref, o_ref, w_scr):
    _cast_weights(w_ref, w_scr)
    acc = jnp.dot(a_ref[...], w_scr[...], preferred_element_type=F32)
    o_ref[...] = acc.astype(o_ref.dtype)


def _mm_transposed_kernel(a_ref, w_ref, o_ref, w_scr):
    _cast_weights(w_ref, w_scr)
    for c0 in range(0, w_scr.shape[1], MXU_COLS):
        acc = jnp.dot(a_ref[...], w_scr[:, c0:c0 + MXU_COLS], preferred_element_type=F32)
        o_ref[c0:c0 + MXU_COLS, :] = acc.T.astype(o_ref.dtype)


def _mm_sigmoid_kernel(a_ref, w_ref, o_ref, w_scr):
    _cast_weights(w_ref, w_scr)
    for c0 in range(0, w_scr.shape[1], MXU_COLS):
        acc = jnp.dot(a_ref[...], w_scr[:, c0:c0 + MXU_COLS], preferred_element_type=F32)
        o_ref[:, c0:c0 + MXU_COLS] = (0.5 * jnp.tanh(0.5 * acc) + 0.5).astype(o_ref.dtype)


def _mm_residual_kernel(a_ref, w_ref, r_ref, o_ref, w_scr):
    _cast_weights(w_ref, w_scr)
    acc = jnp.dot(a_ref[...], w_scr[...], preferred_element_type=F32)
    o_ref[...] = r_ref[...] + acc


def _mm_rope_kernel(a_ref, w_ref, cos_ref, sin_ref, o_ref, mean_ref, w_scr):
    _cast_qk_weights(w_ref, w_scr)
    tm, tn = o_ref.shape
    q_scale = jnp.where(pl.program_id(0) < D_ATTN // tn, SOFTMAX_LOG2_SCALE, 1.0)
    cosf = cos_ref[...] * q_scale
    sinf = sin_ref[...] * q_scale
    for c0 in range(0, tn, MXU_COLS):
        acc = jnp.dot(a_ref[...], w_scr[:, c0:c0 + MXU_COLS], preferred_element_type=F32)
        for h0 in range(0, MXU_COLS, HEAD_DIM):
            a = acc[:, h0:h0 + HEAD_DIM]
            r = a * cosf + pltpu.roll(a, ROT_PARTNER, 1) * sinf
            cols = slice(c0 + h0, c0 + h0 + HEAD_DIM)
            o_ref[:, cols] = r.astype(o_ref.dtype)
            mean_ref[0, :, cols] = jnp.mean(r.reshape(tm // MOBA_BLOCK, MOBA_BLOCK, HEAD_DIM), axis=1)


def _matmul(kernel, name, a, w, *, col_off, n_cols, out_dtype, tm=1024, tn=512,
            extra=(), extra_blocks=(), extra_out_shape=(), extra_out_blocks=(), vmem_mib=48,
            transposed_out=False):
    m, k = a.shape
    off = col_off // tn
    spec = lambda shape, fn: pl.BlockSpec(shape, lambda j, i: fn(i, j))
    if transposed_out:
        main_shape, main_spec = (n_cols, m), spec((tn, tm), lambda i, j: (j, i))
    else:
        main_shape, main_spec = (m, n_cols), spec((tm, tn), lambda i, j: (i, j))
    out_shape = [jax.ShapeDtypeStruct(main_shape, out_dtype)] + list(extra_out_shape)
    out_specs = [main_spec] + [spec(*blk) for blk in extra_out_blocks]
    res = pl.pallas_call(
        kernel,
        grid=(n_cols // tn, m // tm),
        in_specs=[spec((tm, k), lambda i, j: (i, 0)),
                  spec((k, tn), lambda i, j: (0, j + off))] + [spec(*blk) for blk in extra_blocks],
        out_specs=out_specs,
        out_shape=out_shape,
        scratch_shapes=[pltpu.VMEM((k, tn), BF16)],
        name=name,
        compiler_params=_params(("parallel", "arbitrary"), vmem_mib),
    )(a, w, *extra)
    return res if extra_out_shape else res[0]


def _rglru_kernel(xr_ref, yr_ref, cw_ref, cb_ref, wa_ref, wi_ref, ba_ref, bi_ref, lam_ref,
                  o_ref, xbuf, a_scr, u_scr, h_scr, nat_scr, tail_scr, hc_scr):
    ts, tc = xr_ref.shape
    nv = ts // SUBLANES
    halo = (CONV_WIDTH - 1) * SUBLANES
    t = pl.program_id(2)

    def grp(g):
        return slice(halo + g * SUBLANES, halo + (g + 1) * SUBLANES)

    @pl.when(t == 0)
    def _():
        tail_scr[...] = jnp.zeros_like(tail_scr)
        hc_scr[...] = jnp.zeros_like(hc_scr)

    lane_blocks = [slice(cb * LANES, (cb + 1) * LANES) for cb in range(tc // LANES)]
    per_seg = nv // SUBLANES
    for cb, ls in enumerate(lane_blocks):
        for s in range(SUBLANES):
            for q in range(per_seg):
                t0 = s * nv + q * SUBLANES
                dst = pl.ds(halo + q * SUBLANES * SUBLANES + s, SUBLANES, stride=SUBLANES)
                xbuf[cb, dst, :] = xr_ref[t0:t0 + SUBLANES, ls]
    sub = lax.broadcasted_iota(jnp.int32, (SUBLANES, LANES), 0)
    for cb, ls in enumerate(lane_blocks):
        for d in range(1, CONV_WIDTH):
            keep = slice((CONV_WIDTH - 1 - d) * SUBLANES, (CONV_WIDTH - d) * SUBLANES)
            cur = xbuf[cb, grp(nv - d), :]
            xbuf[cb, grp(-d), :] = pltpu.roll(
                jnp.where(sub == SUBLANES - 1, tail_scr[keep, ls], cur), 1, 0)
            tail_scr[keep, ls] = cur

    half_a = (-0.5 * LRU_C) * jax.nn.softplus(-lam_ref[...])
    for nb, sl in enumerate(lane_blocks):
        xb = jnp.zeros((ts, LANES), F32) + cb_ref[:, sl]
        for kk in range(CONV_WIDTH):
            start = halo - (CONV_WIDTH - 1 - kk) * SUBLANES
            xb = xb + cw_ref[kk:kk + 1, sl] * xbuf[nb, start:start + ts, :]
        xb16 = xb.astype(BF16)
        zr = jnp.dot(xb16, wa_ref[nb], preferred_element_type=F32) + ba_ref[:, sl]
        zi = jnp.dot(xb16, wi_ref[nb], preferred_element_type=F32) + bi_ref[:, sl]
        log_a = half_a[:, sl] * jnp.tanh(0.5 * zr) + half_a[:, sl]
        ig = 0.5 * jnp.tanh(0.5 * zi) + 0.5
        a_scr[:, sl] = jnp.exp(log_a)
        th = jnp.tanh(log_a)
        y = -2.0 * th / (1.0 - th)
        u_scr[:, sl] = jnp.where(y > 0.0, y * lax.rsqrt(y), 0.0) * (ig * xb)

    h_end = jnp.zeros((SUBLANES, tc), F32)
    p_end = jnp.ones((SUBLANES, tc), F32)
    for v in range(nv):
        rows = slice(v * SUBLANES, (v + 1) * SUBLANES)
        a = a_scr[rows, :]
        h_end = a * h_end + u_scr[rows, :]
        p_end = a * p_end
        h_scr[rows, :] = h_end
        a_scr[rows, :] = p_end
    h_in = hc_scr[...]
    entering = []
    for s in range(SUBLANES):
        entering.append(h_in)
        h_in = h_end[s:s + 1, :] + p_end[s:s + 1, :] * h_in
    hc_scr[...] = h_in
    h_enter = jnp.concatenate(entering, axis=0)

    pitch = nat_scr.shape[1] // SUBLANES
    for v in range(nv):
        rows = slice(v * SUBLANES, (v + 1) * SUBLANES)
        h_v = h_scr[rows, :] + a_scr[rows, :] * h_enter
        for cb, ls in enumerate(lane_blocks):
            nat_scr[cb, pl.ds(v, SUBLANES, stride=pitch), :] = h_v[:, ls]
    for cb, ls in enumerate(lane_blocks):
        for s in range(SUBLANES):
            seg = slice(s * nv, (s + 1) * nv)
            o_ref[seg, ls] = (jax.nn.gelu(yr_ref[seg, ls])
                              * nat_scr[cb, s * pitch:s * pitch + nv, :]).astype(o_ref.dtype)


def _rglru(xy, conv_w, conv_b, w_a, b_a, w_i, b_i, lam, batch, seq, ts=512, tc=512):
    assert RNN_BLOCK == LANES, "the kernel walks gate blocks and 128-lane blocks together"
    n = batch * seq
    nt = seq // ts
    ncb = D_RNN // tc
    halo = (CONV_WIDTH - 1) * SUBLANES
    row = lambda v: v.reshape(1, D_RNN)
    vec_spec = pl.BlockSpec((1, tc), lambda b, c, t: (0, c))
    gate_spec = pl.BlockSpec((tc // RNN_BLOCK, RNN_BLOCK, RNN_BLOCK), lambda b, c, t: (c, 0, 0))
    return pl.pallas_call(
        _rglru_kernel,
        grid=(batch, ncb, nt),
        in_specs=[pl.BlockSpec((ts, tc), lambda b, c, t: (b * nt + t, c)),
                  pl.BlockSpec((ts, tc), lambda b, c, t: (b * nt + t, ncb + c)),
                  pl.BlockSpec((CONV_WIDTH, tc), lambda b, c, t: (0, c)),
                  vec_spec, gate_spec, gate_spec, vec_spec, vec_spec, vec_spec],
        out_specs=pl.BlockSpec((ts, tc), lambda b, c, t: (b * nt + t, c)),
        out_shape=jax.ShapeDtypeStruct((n, D_RNN), BF16),
        scratch_shapes=[pltpu.VMEM((tc // LANES, halo + ts, LANES), F32),
                        pltpu.VMEM((ts, tc), F32),
                        pltpu.VMEM((ts, tc), F32),
                        pltpu.VMEM((ts, tc), F32),
                        pltpu.VMEM((tc // LANES, ts + SUBLANES * SUBLANES, LANES), F32),
                        pltpu.VMEM((halo, tc), F32),
                        pltpu.VMEM((1, tc), F32)],
        name="rglru",
        compiler_params=_params(("parallel", "parallel", "arbitrary"), 32),
    )(xy, xy, conv_w, row(conv_b), w_a.astype(BF16), w_i.astype(BF16), row(b_a), row(b_i), row(lam))


def _moba_kernel(q_ref, k_ref, v_ref, km_ref, o_ref, bias_scr, s_scr, acc_scr, *, heads, tiles):
    first = pl.program_id(2) * tiles

    def one(u, carry):
        qrows = pl.ds(pl.multiple_of(u * MOBA_BLOCK, MOBA_BLOCK), MOBA_BLOCK)
        _moba_tile(first + u, qrows, q_ref, k_ref, v_ref, km_ref, o_ref, bias_scr, s_scr, acc_scr, heads)
        return carry

    lax.fori_loop(0, tiles, one, 0)


def _moba_tile(j, qrows, q_ref, k_ref, v_ref, km_ref, o_ref, bias_scr, s_scr, acc_scr, heads):
    nblk = km_ref.shape[0]
    blk = MOBA_BLOCK
    head_slices = [slice(h * HEAD_DIM, (h + 1) * HEAD_DIM) for h in range(heads)]

    def score_into(slot, block):
        start = pl.multiple_of(block * blk, blk)
        for h, hs in enumerate(head_slices):
            s_scr[slot, h] = lax.dot_general(k_ref[pl.ds(start, blk), hs], q_ref[qrows, hs], _NT,
                                             preferred_element_type=F32)

    ones_rows = jnp.ones((BF16_ROWS, blk), BF16)

    def attend(slot, mask, block, state):
        start = pl.multiple_of(block * blk, blk)
        half = blk // 2
        soft = []
        for h in range(heads):
            m = state[h]
            unselected = bias_scr[h, pl.ds(block, 1), :] < 0.0
            s_lo = mask(s_scr[slot, h, 0:half, :], 0)
            s_hi = mask(s_scr[slot, h, half:blk, :], half)
            m_blk = jnp.maximum(jnp.max(s_lo, axis=0, keepdims=True), jnp.max(s_hi, axis=0, keepdims=True))
            m_new = jnp.where(unselected, m, jnp.maximum(m, m_blk))
            alpha = jnp.exp2(m - m_new)
            p = jnp.exp2(mask(s_scr[slot, h], 0) - jnp.where(unselected, -NEG_INF, m_new))
            soft.append((m_new, alpha, p.astype(BF16)))
        out = []
        for h, hs in enumerate(head_slices):
            m_new, alpha, p = soft[h]
            vt = jnp.concatenate([v_ref[hs, pl.ds(start, blk)], ones_rows], axis=0)
            pv = jnp.dot(vt, p, preferred_element_type=F32)
            acc_scr[h] = alpha * acc_scr[h] + pv
            out.append(m_new)
        return tuple(out)

    no_mask = lambda s, row0: s

    gates = []
    for hs in head_slices:
        q = q_ref[qrows, hs]
        km = km_ref[:, hs]
        km_hi = km.astype(BF16)
        km_lo = (km - km_hi.astype(F32)).astype(BF16)
        gates.append(lax.dot_general(km_hi, q, _NT, preferred_element_type=F32)
                     + lax.dot_general(km_lo, q, _NT, preferred_element_type=F32))
    score_into(0, 0)
    for h, gate in enumerate(gates):
        bidx = lax.broadcasted_iota(jnp.int32, gate.shape, 0)
        past = bidx < j
        g = jnp.where(past, gate, -jnp.inf)
        rank = jnp.zeros(gate.shape, jnp.int32)
        for other in range(nblk):
            go = g[other:other + 1, :]
            beats = jnp.where(go > g, 1, jnp.where(go == g, jnp.where(bidx > other, 1, 0), 0))
            rank = rank + beats
        bias_scr[h] = jnp.where(past, jnp.where(rank < MOBA_TOPK, 0.0, NEG_INF),
                                jnp.where(bidx == j, 0.0, NEG_INF))

    def pair_body(t, state):
        first = 2 * t
        score_into(1, first + 1)
        state = attend(0, no_mask, first, state)
        score_into(0, first + 2)
        return attend(1, no_mask, first + 1, state)

    def odd_body(state):
        score_into(1, j)
        return attend(0, no_mask, j - 1, state)

    acc_scr[...] = jnp.zeros_like(acc_scr)
    init = tuple(jnp.full((1, blk), NEG_INF, F32) for _ in range(heads))
    state = lax.fori_loop(0, lax.shift_right_logical(j, 1), pair_body, init)
    odd = lax.bitwise_and(j, 1)
    state = lax.cond(odd == 1, odd_body, lambda st: st, state)

    def causal_mask(s, row0):
        kpos = row0 + lax.broadcasted_iota(jnp.int32, s.shape, 0)
        qpos = lax.broadcasted_iota(jnp.int32, s.shape, 1)
        return jnp.where(kpos <= qpos, s, NEG_INF)

    attend(odd, causal_mask, j, state)
    for h, hs in enumerate(head_slices):
        o_ref[qrows, hs] = (acc_scr[h, 0:HEAD_DIM, :] / acc_scr[h, HEAD_DIM:HEAD_DIM + 1, :]
                        ).T.astype(o_ref.dtype)


def _moba(qk, v_t, kmean, batch, seq, heads=4, tiles=8):
    n = batch * seq
    nblk = seq // MOBA_BLOCK
    groups = N_HEADS // heads
    width = heads * HEAD_DIM
    steps = nblk // tiles
    rows = tiles * MOBA_BLOCK
    return pl.pallas_call(
        functools.partial(_moba_kernel, heads=heads, tiles=tiles),
        grid=(batch, groups, steps),
        in_specs=[pl.BlockSpec((rows, width), lambda b, h, j: (b * steps + j, h)),
                  pl.BlockSpec((seq, width), lambda b, h, j: (b, groups + h)),
                  pl.BlockSpec((width, seq), lambda b, h, j: (h, b)),
                  pl.BlockSpec((nblk, width), lambda b, h, j: (b, groups + h))],
        out_specs=pl.BlockSpec((rows, width), lambda b, h, j: (b * steps + j, h)),
        out_shape=jax.ShapeDtypeStruct((n, D_ATTN), BF16),
        scratch_shapes=[pltpu.VMEM((heads, nblk, MOBA_BLOCK), F32),
                        pltpu.VMEM((2, heads, MOBA_BLOCK, MOBA_BLOCK), F32),
                        pltpu.VMEM((heads, HEAD_DIM + BF16_ROWS, MOBA_BLOCK), F32)],
        name="moba",
        compiler_params=_params(("parallel", "parallel", "arbitrary"), 56),
    )(qk, qk, v_t, kmean)


def _merge_kernel(gh_ref, o_ref, wr_ref, wa_ref, gr_ref, ga_ref, out_ref, wr_scr, wa_scr):
    _cast_weights(wr_ref, wr_scr)
    _cast_weights(wa_ref, wa_scr)
    rnn = jnp.dot(gh_ref[...], wr_scr[...], preferred_element_type=F32)
    att = jnp.dot(o_ref[...], wa_scr[...], preferred_element_type=F32)
    out_ref[...] = (gr_ref[...].astype(F32) * rnn + ga_ref[...].astype(F32) * att).astype(out_ref.dtype)


def _merge(gh, o, w_rnn, w_attn, gates, tm=1024, tn=512):
    m, k = gh.shape
    nj = D_MODEL // tn
    a_spec = pl.BlockSpec((tm, k), lambda j, i: (i, 0))
    w_spec = pl.BlockSpec((k, tn), lambda j, i: (0, j))
    return pl.pallas_call(
        _merge_kernel,
        grid=(nj, m // tm),
        in_specs=[a_spec, a_spec, w_spec, w_spec,
                  pl.BlockSpec((tm, tn), lambda j, i: (i, j)),
                  pl.BlockSpec((tm, tn), lambda j, i: (i, nj + j))],
        out_specs=pl.BlockSpec((tm, tn), lambda j, i: (i, j)),
        out_shape=jax.ShapeDtypeStruct((m, D_MODEL), BF16),
        scratch_shapes=[pltpu.VMEM((k, tn), BF16), pltpu.VMEM((k, tn), BF16)],
        name="merge",
        compiler_params=_params(("parallel", "arbitrary"), 48),
    )(gh, o, w_rnn, w_attn, gates, gates)


def _memkv_kernel(mem_ref, g_ref, w_ref, o_ref):
    hn = _rms(mem_ref[...], g_ref[...]).astype(BF16)
    o_ref[...] = jnp.dot(hn, w_ref[...], preferred_element_type=F32).astype(o_ref.dtype)


def _memkv(mem2d, g, w_kv, mem_len):
    m, d = mem2d.shape
    return pl.pallas_call(
        _memkv_kernel,
        grid=(m // mem_len,),
        in_specs=[pl.BlockSpec((mem_len, d), lambda i: (i, 0)),
                  pl.BlockSpec((1, d), lambda i: (0, 0)),
                  pl.BlockSpec((d, 2 * D_MEM), lambda i: (0, 0))],
        out_specs=pl.BlockSpec((mem_len, 2 * D_MEM), lambda i: (i, 0)),
        out_shape=jax.ShapeDtypeStruct((m, 2 * D_MEM), BF16),
        name="mem_kv",
        compiler_params=_params(("parallel",), 32),
    )(mem2d, g.reshape(1, d), w_kv)


def _xattn_kernel(x_ref, g_ref, wq_ref, kv_ref, wo_ref, o_ref):
    x = x_ref[...]
    hn = _rms(x, g_ref[...]).astype(BF16)
    q = jnp.dot(hn, wq_ref[...], preferred_element_type=F32).astype(BF16)
    scale = MEM_HEAD_DIM ** -0.5
    heads = []
    for hd in range(MEM_HEADS):
        sl = slice(hd * MEM_HEAD_DIM, (hd + 1) * MEM_HEAD_DIM)
        kh = kv_ref[:, sl]
        vh = kv_ref[:, D_MEM + hd * MEM_HEAD_DIM:D_MEM + (hd + 1) * MEM_HEAD_DIM]
        s = lax.dot_general(q[:, sl], kh, _NT, preferred_element_type=F32) * scale
        m = jnp.max(s, axis=-1, keepdims=True)
        p = jnp.exp(s - m)
        l = jnp.sum(p, axis=-1, keepdims=True)
        oh = jnp.dot(p.astype(BF16), vh, preferred_element_type=F32) / l
        heads.append(oh.astype(BF16))
    o_all = jnp.concatenate(heads, axis=-1)
    o_ref[...] = x + jnp.dot(o_all, wo_ref[...], preferred_element_type=F32)


def _xattn(x, g, w_q, kv, w_o, seq, mem_len, tm=1024):
    m, d = x.shape
    per_batch = seq // tm
    return pl.pallas_call(
        _xattn_kernel,
        grid=(m // tm,),
        in_specs=[pl.BlockSpec((tm, d), lambda i: (i, 0)),
                  pl.BlockSpec((1, d), lambda i: (0, 0)),
                  pl.BlockSpec((d, D_MEM), lambda i: (0, 0)),
                  pl.BlockSpec((mem_len, 2 * D_MEM), lambda i: (i // per_batch, 0)),
                  pl.BlockSpec((D_MEM, d), lambda i: (0, 0))],
        out_specs=pl.BlockSpec((tm, d), lambda i: (i, 0)),
        out_shape=jax.ShapeDtypeStruct((m, d), F32),
        name="xattn",
        compiler_params=_params(("parallel",), 48),
    )(x, g.reshape(1, d), w_q, kv, w_o)


def _ffn_kernel(x_ref, g_ref, wg_ref, wu_ref, wd_ref, gf_ref, o_ref, h_scr):
    f = pl.program_id(1)

    @pl.when(f == 0)
    def _():
        h_scr[...] = _rms(x_ref[...], g_ref[...]).astype(BF16)
        o_ref[...] = jnp.zeros_like(o_ref)

    hn = h_scr[...]
    a = jnp.dot(hn, wg_ref[...], preferred_element_type=F32)
    b = jnp.dot(hn, wu_ref[...], preferred_element_type=F32)
    act = (jax.nn.silu(a) * b).astype(BF16)
    o_ref[...] += jnp.dot(act, wd_ref[...], preferred_element_type=F32)

    @pl.when(f == pl.num_programs(1) - 1)
    def _():
        o_ref[...] = _rms(x_ref[...] + o_ref[...], gf_ref[...])


def _ffn(x, g, w_gate, w_up, w_down, g_final, tm=1024, tf=512):
    m, d = x.shape
    d_ff = w_gate.shape[1]
    return pl.pallas_call(
        _ffn_kernel,
        grid=(m // tm, d_ff // tf),
        in_specs=[pl.BlockSpec((tm, d), lambda i, f: (i, 0)),
                  pl.BlockSpec((1, d), lambda i, f: (0, 0)),
                  pl.BlockSpec((d, tf), lambda i, f: (0, f)),
                  pl.BlockSpec((d, tf), lambda i, f: (0, f)),
                  pl.BlockSpec((tf, d), lambda i, f: (f, 0)),
                  pl.BlockSpec((1, d), lambda i, f: (0, 0))],
        out_specs=pl.BlockSpec((tm, d), lambda i, f: (i, 0)),
        out_shape=jax.ShapeDtypeStruct((m, d), F32),
        scratch_shapes=[pltpu.VMEM((tm, d), BF16)],
        name="ffn",
        compiler_params=_params(("parallel", "arbitrary"), 63),
    )(x, g.reshape(1, d), w_gate, w_up, w_down, g_final.reshape(1, d))


def _layer(x2d, mem2d, cosf, sinf, batch, seq, mem_len, p):
    n = x2d.shape[0]
    w_in = p["w_in"]
    hn = _norm_bf16(x2d, p["norm_mix_g"])

    c0 = 0
    wide = 1024
    xy = _matmul(_mm_plain_kernel, "proj_xy", hn, w_in, col_off=c0, n_cols=2 * D_RNN, out_dtype=F32,
                 tn=wide)
    c0 += 2 * D_RNN
    tm, tn = 1024, wide
    rope_block = ((tm, LANES), lambda i, j: (i, 0))
    qk, means = _matmul(
        _mm_rope_kernel, "proj_qk", hn, w_in, col_off=c0, n_cols=2 * D_ATTN, out_dtype=BF16,
        tm=tm, tn=tn, extra=(cosf, sinf), extra_blocks=(rope_block, rope_block),
        extra_out_shape=(jax.ShapeDtypeStruct((n // tm, tm // MOBA_BLOCK, 2 * D_ATTN), F32),),
        extra_out_blocks=(((1, tm // MOBA_BLOCK, tn), lambda i, j: (i, 0, j)),))
    c0 += 2 * D_ATTN
    v_t = _matmul(_mm_transposed_kernel, "proj_v", hn, w_in, col_off=c0, n_cols=D_ATTN,
                  out_dtype=BF16, tn=wide, transposed_out=True)
    c0 += D_ATTN
    gates = _matmul(_mm_sigmoid_kernel, "proj_gates", hn, w_in, col_off=c0, n_cols=2 * D_MODEL,
                    out_dtype=BF16, tn=wide)

    gh = _rglru(xy, p["conv_w"], p["conv_b"], p["lru_w_a"], p["lru_b_a"], p["lru_w_i"],
                p["lru_b_i"], p["lru_lambda"], batch, seq)
    kmean = means.reshape(n // MOBA_BLOCK, 2 * D_ATTN)
    o = _moba(qk, v_t, kmean, batch, seq)

    merged = _merge(gh, o, p["w_rnn_proj"], p["w_attn_proj"], gates)
    x1 = _matmul(_mm_residual_kernel, "mix_out", merged, p["w_mix_out"].astype(BF16), col_off=0,
                 n_cols=D_MODEL, out_dtype=F32, tm=512, tn=D_MODEL,
                 extra=(x2d,), extra_blocks=(((512, D_MODEL), lambda i, j: (i, j)),))

    kv = _memkv(mem2d, p["norm_mem_g"], p["w_xkv"].astype(BF16), mem_len)
    x2 = _xattn(x1, p["norm_xq_g"], p["w_xq"].astype(BF16), kv, p["w_xo"].astype(BF16), seq, mem_len)
    return x2


def kernel(x, mem, positions, norm_mix_g, w_in, conv_w, conv_b, lru_w_a, lru_b_a, lru_w_i, lru_b_i,
           lru_lambda, w_rnn_proj, w_attn_proj, w_mix_out, norm_xq_g, norm_mem_g, w_xq, w_xkv, w_xo,
           norm_ffn_g, w_ffn_gate, w_ffn_up, w_ffn_down, norm_final_g):
    batch, seq, d = x.shape
    mem_len = mem.shape[1]
    assert w_in.shape[0] == 1, "only DEPTH == 1 is supported"
    x2d = x.reshape(batch * seq, d)
    mem2d = mem.reshape(batch * mem_len, d)
    cosf, sinf = _rope_tables(positions)
    p = dict(norm_mix_g=norm_mix_g[0], w_in=w_in[0], conv_w=conv_w[0], conv_b=conv_b[0],
             lru_w_a=lru_w_a[0], lru_b_a=lru_b_a[0], lru_w_i=lru_w_i[0], lru_b_i=lru_b_i[0],
             lru_lambda=lru_lambda[0], w_rnn_proj=w_rnn_proj[0], w_attn_proj=w_attn_proj[0],
             w_mix_out=w_mix_out[0], norm_xq_g=norm_xq_g[0], norm_mem_g=norm_mem_g[0],
             w_xq=w_xq[0], w_xkv=w_xkv[0], w_xo=w_xo[0])
    x2 = _layer(x2d, mem2d, cosf, sinf, batch, seq, mem_len, p)
    out = _ffn(x2, norm_ffn_g[0], w_ffn_gate[0].astype(BF16), w_ffn_up[0].astype(BF16),
               w_ffn_down[0].astype(BF16), norm_final_g)
    return out.reshape(batch, seq, d)
```

```python
import functools

import jax
import jax.numpy as jnp
from jax import lax
from jax.experimental import pallas as pl
from jax.experimental.pallas import tpu as pltpu

D_MODEL = 2048
N_HEADS = 16
HEAD_DIM = 128
D_ATTN = N_HEADS * HEAD_DIM
MOBA_BLOCK = 256
MOBA_TOPK = 3
ROPE_THETA = 500000.0
ROT_DIM = HEAD_DIM // 4
ROT_HALF = ROT_DIM // 2
D_RNN = 2048
N_RNN_BLOCKS = 16
RNN_BLOCK = D_RNN // N_RNN_BLOCKS
CONV_WIDTH = 4
LRU_C = 8.0
MEM_HEADS = 4
MEM_HEAD_DIM = 128
D_MEM = MEM_HEADS * MEM_HEAD_DIM
RMS_EPS = 1e-6
NEG_INF = -1e30
LOG2_E = 1.4426950408889634
SOFTMAX_LOG2_SCALE = (HEAD_DIM ** -0.5) * LOG2_E

LANES = 128
SUBLANES = 8
BF16_ROWS = 16
MXU_COLS = 2 * 256
ROT_PARTNER = LANES // 2
MIB = 1024 * 1024

BF16 = jnp.bfloat16
F32 = jnp.float32

_NT = (((1,), (1,)), ((), ()))
_TN = (((0,), (0,)), ((), ()))


def _params(semantics, vmem_mib):
    return pltpu.CompilerParams(dimension_semantics=semantics,
                                vmem_limit_bytes=vmem_mib * MIB)


def _rms(x, g):
    ms = jnp.mean(x * x, axis=-1, keepdims=True)
    return x * lax.rsqrt(ms + RMS_EPS) * g


def _norm_kernel(x_ref, g_ref, o_ref):
    o_ref[...] = _rms(x_ref[...], g_ref[...]).astype(o_ref.dtype)


def _norm_bf16(x, g, tm=512):
    m, d = x.shape
    return pl.pallas_call(
        _norm_kernel,
        grid=(m // tm,),
        in_specs=[pl.BlockSpec((tm, d), lambda i: (i, 0)),
                  pl.BlockSpec((1, d), lambda i: (0, 0))],
        out_specs=pl.BlockSpec((tm, d), lambda i: (i, 0)),
        out_shape=jax.ShapeDtypeStruct((m, d), BF16),
        name="norm_mix",
        compiler_params=_params(("parallel",), 32),
    )(x, g.reshape(1, d))


def _rope_table_kernel(pos_ref, invf_ref, cos_ref, sin_ref):
    ang = pos_ref[...] * invf_ref[...]
    lane = lax.broadcasted_iota(jnp.int32, ang.shape, 1)
    s = jnp.sin(ang)
    cos_ref[...] = jnp.cos(ang)
    sin_ref[...] = jnp.where(lane < ROT_PARTNER, -s, s)


def _rope_head_layout():
    split = ROT_DIM + ROT_PARTNER - ROT_HALF
    return [(0, ROT_HALF), (ROT_DIM, split), (ROT_HALF, ROT_DIM), (split, HEAD_DIM)]


def _cast_weights(w_ref, w_scr):
    @pl.when(pl.program_id(1) == 0)
    def _():
        w_scr[...] = w_ref[...].astype(w_scr.dtype)


def _cast_qk_weights(w_ref, w_scr):
    @pl.when(pl.program_id(1) == 0)
    def _():
        rows = 256
        lane = lax.broadcasted_iota(jnp.int32, (rows, HEAD_DIM), 1)
        for r0 in range(0, w_ref.shape[0], rows):
            for h0 in range(0, w_ref.shape[1], HEAD_DIM):
                x = w_ref[r0:r0 + rows, h0:h0 + HEAD_DIM]
                out, at = x, 0
                for start, stop in _rope_head_layout():
                    if start != at:
                        moved = pltpu.roll(x, (at - start) % HEAD_DIM, 1)
                        out = jnp.where((lane >= at) & (lane < at + stop - start), moved, out)
                    at += stop - start
                w_scr[r0:r0 + rows, h0:h0 + HEAD_DIM] = out.astype(w_scr.dtype)


def _rope_tables(positions, tr=1024):
    n = positions.size
    pos = positions.astype(F32).reshape(n, 1)
    inv_freq = jnp.power(ROPE_THETA, -jnp.arange(ROT_HALF, dtype=F32) / ROT_HALF)
    gap = jnp.zeros((ROT_PARTNER - ROT_HALF,), F32)
    invf = jnp.concatenate([inv_freq, gap, inv_freq, gap]).reshape(1, LANES)
    return pl.pallas_call(
        _rope_table_kernel,
        grid=(n // tr,),
        in_specs=[pl.BlockSpec((tr, 1), lambda i: (i, 0)),
                  pl.BlockSpec((1, LANES), lambda i: (0, 0))],
        out_specs=[pl.BlockSpec((tr, LANES), lambda i: (i, 0)),
                   pl.BlockSpec((tr, LANES), lambda i: (i, 0))],
        out_shape=[jax.ShapeDtypeStruct((n, LANES), F32),
                   jax.ShapeDtypeStruct((n, LANES), F32)],
        name="rope_tables",
        compiler_params=_params(("parallel",), 32),
    )(pos, invf)


def _mm_plain_kernel(a_ref, w_ref, o_ref, w_scr):
    _cast_weights(w_ref, w_scr)
    acc = jnp.dot(a_ref[...], w_scr[...], preferred_element_type=F32)
    o_ref[...] = acc.astype(o_ref.dtype)


def _mm_transposed_kernel(a_ref, w_ref, o_ref, w_scr):
    _cast_weights(w_ref, w_scr)
    for c0 in range(0, w_scr.shape[1], MXU_COLS):
        acc = jnp.dot(a_ref[...], w_scr[:, c0:c0 + MXU_COLS], preferred_element_type=F32)
        o_ref[c0:c0 + MXU_COLS, :] = acc.T.astype(o_ref.dtype)


def _mm_sigmoid_kernel(a_ref, w_ref, o_ref, w_scr):
    _cast_weights(w_ref, w_scr)
    for c0 in range(0, w_scr.shape[1], MXU_COLS):
        acc = jnp.dot(a_ref[...], w_scr[:, c0:c0 + MXU_COLS], preferred_element_type=F32)
        o_ref[:, c0:c0 + MXU_COLS] = (0.5 * jnp.tanh(0.5 * acc) + 0.5).astype(o_ref.dtype)


def _mm_residual_kernel(a_ref, w_ref, r_ref, o_ref, w_scr):
    _cast_weights(w_ref, w_scr)
    acc = jnp.dot(a_ref[...], w_scr[...], preferred_element_type=F32)
    o_ref[...] = r_ref[...] + acc


def _mm_rope_kernel(a_ref, w_ref, cos_ref, sin_ref, o_ref, mean_ref, w_scr):
    _cast_qk_weights(w_ref, w_scr)
    tm, tn = o_ref.shape
    q_scale = jnp.where(pl.program_id(0) < D_ATTN // tn, SOFTMAX_LOG2_SCALE, 1.0)
    cosf = cos_ref[...] * q_scale
    sinf = sin_ref[...] * q_scale
    for c0 in range(0, tn, MXU_COLS):
        acc = jnp.dot(a_ref[...], w_scr[:, c0:c0 + MXU_COLS], preferred_element_type=F32)
        for h0 in range(0, MXU_COLS, HEAD_DIM):
            a = acc[:, h0:h0 + HEAD_DIM]
            r = a * cosf + pltpu.roll(a, ROT_PARTNER, 1) * sinf
            cols = slice(c0 + h0, c0 + h0 + HEAD_DIM)
            o_ref[:, cols] = r.astype(o_ref.dtype)
            mean_ref[0, :, cols] = jnp.mean(r.reshape(tm // MOBA_BLOCK, MOBA_BLOCK, HEAD_DIM), axis=1)


def _matmul(kernel, name, a, w, *, col_off, n_cols, out_dtype, tm=1024, tn=512,
            extra=(), extra_blocks=(), extra_out_shape=(), extra_out_blocks=(), vmem_mib=48,
            transposed_out=False):
    m, k = a.shape
    off = col_off // tn
    spec = lambda shape, fn: pl.BlockSpec(shape, lambda j, i: fn(i, j))
    if transposed_out:
        main_shape, main_spec = (n_cols, m), spec((tn, tm), lambda i, j: (j, i))
    else:
        main_shape, main_spec = (m, n_cols), spec((tm, tn), lambda i, j: (i, j))
    out_shape = [jax.ShapeDtypeStruct(main_shape, out_dtype)] + list(extra_out_shape)
    out_specs = [main_spec] + [spec(*blk) for blk in extra_out_blocks]
    res = pl.pallas_call(
        kernel,
        grid=(n_cols // tn, m // tm),
        in_specs=[spec((tm, k), lambda i, j: (i, 0)),
                  spec((k, tn), lambda i, j: (0, j + off))] + [spec(*blk) for blk in extra_blocks],
        out_specs=out_specs,
        out_shape=out_shape,
        scratch_shapes=[pltpu.VMEM((k, tn), BF16)],
        name=name,
        compiler_params=_params(("parallel", "arbitrary"), vmem_mib),
    )(a, w, *extra)
    return res if extra_out_shape else res[0]


def _rglru_kernel(xr_ref, yr_ref, cw_ref, cb_ref, wa_ref, wi_ref, ba_ref, bi_ref, lam_ref,
                  o_ref, xbuf, a_scr, u_scr, h_scr, nat_scr, tail_scr, hc_scr):
    ts, tc = xr_ref.shape
    nv = ts // SUBLANES
    halo = (CONV_WIDTH - 1) * SUBLANES
    t = pl.program_id(2)

    def grp(g):
        return slice(halo + g * SUBLANES, halo + (g + 1) * SUBLANES)

    @pl.when(t == 0)
    def _():
        tail_scr[...] = jnp.zeros_like(tail_scr)
        hc_scr[...] = jnp.zeros_like(hc_scr)

    lane_blocks = [slice(cb * LANES, (cb + 1) * LANES) for cb in range(tc // LANES)]
    per_seg = nv // SUBLANES
    for cb, ls in enumerate(lane_blocks):
        for s in range(SUBLANES):
            for q in range(per_seg):
                t0 = s * nv + q * SUBLANES
                dst = pl.ds(halo + q * SUBLANES * SUBLANES + s, SUBLANES, stride=SUBLANES)
                xbuf[cb, dst, :] = xr_ref[t0:t0 + SUBLANES, ls]
    sub = lax.broadcasted_iota(jnp.int32, (SUBLANES, LANES), 0)
    for cb, ls in enumerate(lane_blocks):
        for d in range(1, CONV_WIDTH):
            keep = slice((CONV_WIDTH - 1 - d) * SUBLANES, (CONV_WIDTH - d) * SUBLANES)
            cur = xbuf[cb, grp(nv - d), :]
            xbuf[cb, grp(-d), :] = pltpu.roll(
                jnp.where(sub == SUBLANES - 1, tail_scr[keep, ls], cur), 1, 0)
            tail_scr[keep, ls] = cur

    half_a = (-0.5 * LRU_C) * jax.nn.softplus(-lam_ref[...])
    for nb, sl in enumerate(lane_blocks):
        xb = jnp.zeros((ts, LANES), F32) + cb_ref[:, sl]
        for kk in range(CONV_WIDTH):
            start = halo - (CONV_WIDTH - 1 - kk) * SUBLANES
            xb = xb + cw_ref[kk:kk + 1, sl] * xbuf[nb, start:start + ts, :]
        xb16 = xb.astype(BF16)
        zr_half = jnp.dot(xb16, wa_ref[nb], preferred_element_type=F32) + ba_ref[:, sl]
        zi_half = jnp.dot(xb16, wi_ref[nb], preferred_element_type=F32) + bi_ref[:, sl]
        log_a = half_a[:, sl] * jnp.tanh(zr_half) + half_a[:, sl]
        ig = 0.5 * jnp.tanh(zi_half) + 0.5
        a_scr[:, sl] = jnp.exp(log_a)
        th = jnp.tanh(log_a)
        y = -2.0 * th / (1.0 - th)
        u_scr[:, sl] = jnp.where(y > 0.0, y * lax.rsqrt(y), 0.0) * (ig * xb)

    h_end = jnp.zeros((SUBLANES, tc), F32)
    p_end = jnp.ones((SUBLANES, tc), F32)
    for v in range(nv):
        rows = slice(v * SUBLANES, (v + 1) * SUBLANES)
        a = a_scr[rows, :]
        h_end = a * h_end + u_scr[rows, :]
        p_end = a * p_end
        h_scr[rows, :] = h_end
        a_scr[rows, :] = p_end
    h_in = hc_scr[...]
    entering = []
    for s in range(SUBLANES):
        entering.append(h_in)
        h_in = h_end[s:s + 1, :] + p_end[s:s + 1, :] * h_in
    hc_scr[...] = h_in
    h_enter = jnp.concatenate(entering, axis=0)

    pitch = nat_scr.shape[1] // SUBLANES
    for v in range(nv):
        rows = slice(v * SUBLANES, (v + 1) * SUBLANES)
        h_v = h_scr[rows, :] + a_scr[rows, :] * h_enter
        for cb, ls in enumerate(lane_blocks):
            nat_scr[cb, pl.ds(v, SUBLANES, stride=pitch), :] = h_v[:, ls]
    for cb, ls in enumerate(lane_blocks):
        for s in range(SUBLANES):
            seg = slice(s * nv, (s + 1) * nv)
            o_ref[seg, ls] = (jax.nn.gelu(yr_ref[seg, ls])
                              * nat_scr[cb, s * pitch:s * pitch + nv, :]).astype(o_ref.dtype)


def _rglru(xy, conv_w, conv_b, w_a, b_a, w_i, b_i, lam, batch, seq, ts=512, tc=512):
    assert RNN_BLOCK == LANES, "the kernel walks gate blocks and 128-lane blocks together"
    n = batch * seq
    nt = seq // ts
    ncb = D_RNN // tc
    halo = (CONV_WIDTH - 1) * SUBLANES
    row = lambda v: v.reshape(1, D_RNN)
    vec_spec = pl.BlockSpec((1, tc), lambda b, c, t: (0, c))
    gate_spec = pl.BlockSpec((tc // RNN_BLOCK, RNN_BLOCK, RNN_BLOCK), lambda b, c, t: (c, 0, 0))
    return pl.pallas_call(
        _rglru_kernel,
        grid=(batch, ncb, nt),
        in_specs=[pl.BlockSpec((ts, tc), lambda b, c, t: (b * nt + t, c)),
                  pl.BlockSpec((ts, tc), lambda b, c, t: (b * nt + t, ncb + c)),
                  pl.BlockSpec((CONV_WIDTH, tc), lambda b, c, t: (0, c)),
                  vec_spec, gate_spec, gate_spec, vec_spec, vec_spec, vec_spec],
        out_specs=pl.BlockSpec((ts, tc), lambda b, c, t: (b * nt + t, c)),
        out_shape=jax.ShapeDtypeStruct((n, D_RNN), BF16),
        scratch_shapes=[pltpu.VMEM((tc // LANES, halo + ts, LANES), F32),
                        pltpu.VMEM((ts, tc), F32),
                        pltpu.VMEM((ts, tc), F32),
                        pltpu.VMEM((ts, tc), F32),
                        pltpu.VMEM((tc // LANES, ts + SUBLANES * SUBLANES, LANES), F32),
                        pltpu.VMEM((halo, tc), F32),
                        pltpu.VMEM((1, tc), F32)],
        name="rglru",
        compiler_params=_params(("parallel", "parallel", "arbitrary"), 32),
    )(xy, xy, conv_w, row(conv_b), (0.5 * w_a).astype(BF16), (0.5 * w_i).astype(BF16),
      row(0.5 * b_a), row(0.5 * b_i), row(lam))


def _moba_kernel(q_ref, k_ref, v_ref, km_ref, o_ref, bias_scr, s_scr, acc_scr, *, heads, tiles):
    first = pl.program_id(2) * tiles

    def one(u, carry):
        qrows = pl.ds(pl.multiple_of(u * MOBA_BLOCK, MOBA_BLOCK), MOBA_BLOCK)
        _moba_tile(first + u, qrows, q_ref, k_ref, v_ref, km_ref, o_ref, bias_scr, s_scr, acc_scr, heads)
        return carry

    lax.fori_loop(0, tiles, one, 0)


def _moba_tile(j, qrows, q_ref, k_ref, v_ref, km_ref, o_ref, bias_scr, s_scr, acc_scr, heads):
    nblk = km_ref.shape[0]
    blk = MOBA_BLOCK
    head_slices = [slice(h * HEAD_DIM, (h + 1) * HEAD_DIM) for h in range(heads)]

    def score_into(slot, block):
        start = pl.multiple_of(block * blk, blk)
        for h, hs in enumerate(head_slices):
            s_scr[slot, h] = lax.dot_general(k_ref[pl.ds(start, blk), hs], q_ref[qrows, hs], _NT,
                                             preferred_element_type=F32)

    ones_rows = jnp.ones((BF16_ROWS, blk), BF16)

    def attend(slot, mask, block, state):
        start = pl.multiple_of(block * blk, blk)
        half = blk // 2
        soft = []
        for h in range(heads):
            m = state[h]
            unselected = bias_scr[h, pl.ds(block, 1), :] < 0.0
            s_lo = mask(s_scr[slot, h, 0:half, :], 0)
            s_hi = mask(s_scr[slot, h, half:blk, :], half)
            m_blk = jnp.maximum(jnp.max(s_lo, axis=0, keepdims=True), jnp.max(s_hi, axis=0, keepdims=True))
            m_new = jnp.where(unselected, m, jnp.maximum(m, m_blk))
            alpha = jnp.exp2(m - m_new)
            p = jnp.exp2(mask(s_scr[slot, h], 0) - jnp.where(unselected, -NEG_INF, m_new))
            soft.append((m_new, alpha, p.astype(BF16)))
        out = []
        for h, hs in enumerate(head_slices):
            m_new, alpha, p = soft[h]
            vt = jnp.concatenate([v_ref[hs, pl.ds(start, blk)], ones_rows], axis=0)
            pv = jnp.dot(vt, p, preferred_element_type=F32)
            acc_scr[h] = alpha * acc_scr[h] + pv
            out.append(m_new)
        return tuple(out)

    no_mask = lambda s, row0: s

    gates = []
    for hs in head_slices:
        q = q_ref[qrows, hs]
        km = km_ref[:, hs]
        km_hi = km.astype(BF16)
        km_lo = (km - km_hi.astype(F32)).astype(BF16)
        gates.append(lax.dot_general(km_hi, q, _NT, preferred_element_type=F32)
                     + lax.dot_general(km_lo, q, _NT, preferred_element_type=F32))
    score_into(0, 0)
    for h, gate in enumerate(gates):
        bidx = lax.broadcasted_iota(jnp.int32, gate.shape, 0)
        past = bidx < j
        g = jnp.where(past, gate, -jnp.inf)
        rank = jnp.zeros(gate.shape, jnp.int32)
        for other in range(nblk):
            go = g[other:other + 1, :]
            beats = jnp.where(go > g, 1, jnp.where(go == g, jnp.where(bidx > other, 1, 0), 0))
            rank = rank + beats
        bias_scr[h] = jnp.where(past, jnp.where(rank < MOBA_TOPK, 0.0, NEG_INF),
                                jnp.where(bidx == j, 0.0, NEG_INF))

    def pair_body(t, state):
        first = 2 * t
        score_into(1, first + 1)
        state = attend(0, no_mask, first, state)
        score_into(0, first + 2)
        return attend(1, no_mask, first + 1, state)

    def odd_body(state):
        score_into(1, j)
        return attend(0, no_mask, j - 1, state)

    acc_scr[...] = jnp.zeros_like(acc_scr)
    init = tuple(jnp.full((1, blk), NEG_INF, F32) for _ in range(heads))
    state = lax.fori_loop(0, lax.shift_right_logical(j, 1), pair_body, init)
    odd = lax.bitwise_and(j, 1)
    state = lax.cond(odd == 1, odd_body, lambda st: st, state)

    def causal_mask(s, row0):
        kpos = row0 + lax.broadcasted_iota(jnp.int32, s.shape, 0)
        qpos = lax.broadcasted_iota(jnp.int32, s.shape, 1)
        return jnp.where(kpos <= qpos, s, NEG_INF)

    attend(odd, causal_mask, j, state)
    for h, hs in enumerate(head_slices):
        o_ref[qrows, hs] = (acc_scr[h, 0:HEAD_DIM, :] / acc_scr[h, HEAD_DIM:HEAD_DIM + 1, :]
                        ).T.astype(o_ref.dtype)


def _moba(qk, v_t, kmean, batch, seq, heads=8, tiles=4):
    n = batch * seq
    nblk = seq // MOBA_BLOCK
    groups = N_HEADS // heads
    width = heads * HEAD_DIM
    steps = nblk // tiles
    rows = tiles * MOBA_BLOCK
    return pl.pallas_call(
        functools.partial(_moba_kernel, heads=heads, tiles=tiles),
        grid=(batch, groups, steps),
        in_specs=[pl.BlockSpec((rows, width), lambda b, h, j: (b * steps + j, h)),
                  pl.BlockSpec((seq, width), lambda b, h, j: (b, groups + h)),
                  pl.BlockSpec((width, seq), lambda b, h, j: (h, b)),
                  pl.BlockSpec((nblk, width), lambda b, h, j: (b, groups + h))],
        out_specs=pl.BlockSpec((rows, width), lambda b, h, j: (b * steps + j, h)),
        out_shape=jax.ShapeDtypeStruct((n, D_ATTN), BF16),
        scratch_shapes=[pltpu.VMEM((heads, nblk, MOBA_BLOCK), F32),
                        pltpu.VMEM((2, heads, MOBA_BLOCK, MOBA_BLOCK), F32),
                        pltpu.VMEM((heads, HEAD_DIM + BF16_ROWS, MOBA_BLOCK), F32)],
        name="moba",
        compiler_params=_params(("parallel", "parallel", "arbitrary"), 56),
    )(qk, qk, v_t, kmean)


def _merge_kernel(gh_ref, o_ref, wr_ref, wa_ref, gr_ref, ga_ref, out_ref, wr_scr, wa_scr):
    _cast_weights(wr_ref, wr_scr)
    _cast_weights(wa_ref, wa_scr)
    rnn = jnp.dot(gh_ref[...], wr_scr[...], preferred_element_type=F32)
    att = jnp.dot(o_ref[...], wa_scr[...], preferred_element_type=F32)
    out_ref[...] = (gr_ref[...].astype(F32) * rnn + ga_ref[...].astype(F32) * att).astype(out_ref.dtype)


def _merge(gh, o, w_rnn, w_attn, gates, tm=1024, tn=512):
    m, k = gh.shape
    nj = D_MODEL // tn
    a_spec = pl.BlockSpec((tm, k), lambda j, i: (i, 0))
    w_spec = pl.BlockSpec((k, tn), lambda j, i: (0, j))
    return pl.pallas_call(
        _merge_kernel,
        grid=(nj, m // tm),
        in_specs=[a_spec, a_spec, w_spec, w_spec,
                  pl.BlockSpec((tm, tn), lambda j, i: (i, j)),
                  pl.BlockSpec((tm, tn), lambda j, i: (i, nj + j))],
        out_specs=pl.BlockSpec((tm, tn), lambda j, i: (i, j)),
        out_shape=jax.ShapeDtypeStruct((m, D_MODEL), BF16),
        scratch_shapes=[pltpu.VMEM((k, tn), BF16), pltpu.VMEM((k, tn), BF16)],
        name="merge",
        compiler_params=_params(("parallel", "arbitrary"), 48),
    )(gh, o, w_rnn, w_attn, gates, gates)


def _memkv_kernel(mem_ref, g_ref, w_ref, o_ref):
    hn = _rms(mem_ref[...], g_ref[...]).astype(BF16)
    o_ref[...] = jnp.dot(hn, w_ref[...], preferred_element_type=F32).astype(o_ref.dtype)


def _memkv(mem2d, g, w_kv, mem_len):
    m, d = mem2d.shape
    return pl.pallas_call(
        _memkv_kernel,
        grid=(m // mem_len,),
        in_specs=[pl.BlockSpec((mem_len, d), lambda i: (i, 0)),
                  pl.BlockSpec((1, d), lambda i: (0, 0)),
                  pl.BlockSpec((d, 2 * D_MEM), lambda i: (0, 0))],
        out_specs=pl.BlockSpec((mem_len, 2 * D_MEM), lambda i: (i, 0)),
        out_shape=jax.ShapeDtypeStruct((m, 2 * D_MEM), BF16),
        name="mem_kv",
        compiler_params=_params(("parallel",), 32),
    )(mem2d, g.reshape(1, d), w_kv)


def _xattn_kernel(x_ref, g_ref, wq_ref, kv_ref, wo_ref, o_ref):
    x = x_ref[...]
    hn = _rms(x, g_ref[...]).astype(BF16)
    q = jnp.dot(hn, wq_ref[...], preferred_element_type=F32).astype(BF16)
    scale = MEM_HEAD_DIM ** -0.5
    heads = []
    for hd in range(MEM_HEADS):
        sl = slice(hd * MEM_HEAD_DIM, (hd + 1) * MEM_HEAD_DIM)
        kh = kv_ref[:, sl]
        vh = kv_ref[:, D_MEM + hd * MEM_HEAD_DIM:D_MEM + (hd + 1) * MEM_HEAD_DIM]
        s = lax.dot_general(q[:, sl], kh, _NT, preferred_element_type=F32) * scale
        m = jnp.max(s, axis=-1, keepdims=True)
        p = jnp.exp(s - m)
        l = jnp.sum(p, axis=-1, keepdims=True)
        oh = jnp.dot(p.astype(BF16), vh, preferred_element_type=F32) / l
        heads.append(oh.astype(BF16))
    o_all = jnp.concatenate(heads, axis=-1)
    o_ref[...] = x + jnp.dot(o_all, wo_ref[...], preferred_element_type=F32)


def _xattn(x, g, w_q, kv, w_o, seq, mem_len, tm=1024):
    m, d = x.shape
    per_batch = seq // tm
    return pl.pallas_call(
        _xattn_kernel,
        grid=(m // tm,),
        in_specs=[pl.BlockSpec((tm, d), lambda i: (i, 0)),
                  pl.BlockSpec((1, d), lambda i: (0, 0)),
                  pl.BlockSpec((d, D_MEM), lambda i: (0, 0)),
                  pl.BlockSpec((mem_len, 2 * D_MEM), lambda i: (i // per_batch, 0)),
                  pl.BlockSpec((D_MEM, d), lambda i: (0, 0))],
        out_specs=pl.BlockSpec((tm, d), lambda i: (i, 0)),
        out_shape=jax.ShapeDtypeStruct((m, d), F32),
        name="xattn",
        compiler_params=_params(("parallel",), 48),
    )(x, g.reshape(1, d), w_q, kv, w_o)


def _ffn_kernel(x_ref, g_ref, wg_ref, wu_ref, wd_ref, gf_ref, o_ref, h_scr):
    f = pl.program_id(1)

    @pl.when(f == 0)
    def _():
        h_scr[...] = _rms(x_ref[...], g_ref[...]).astype(BF16)
        o_ref[...] = jnp.zeros_like(o_ref)

    hn = h_scr[...]
    a = jnp.dot(hn, wg_ref[...], preferred_element_type=F32)
    b = jnp.dot(hn, wu_ref[...], preferred_element_type=F32)
    act = (jax.nn.silu(a) * b).astype(BF16)
    o_ref[...] += jnp.dot(act, wd_ref[...], preferred_element_type=F32)

    @pl.when(f == pl.num_programs(1) - 1)
    def _():
        o_ref[...] = _rms(x_ref[...] + o_ref[...], gf_ref[...])


def _ffn(x, g, w_gate, w_up, w_down, g_final, tm=1024, tf=512):
    m, d = x.shape
    d_ff = w_gate.shape[1]
    return pl.pallas_call(
        _ffn_kernel,
        grid=(m // tm, d_ff // tf),
        in_specs=[pl.BlockSpec((tm, d), lambda i, f: (i, 0)),
                  pl.BlockSpec((1, d), lambda i, f: (0, 0)),
                  pl.BlockSpec((d, tf), lambda i, f: (0, f)),
                  pl.BlockSpec((d, tf), lambda i, f: (0, f)),
                  pl.BlockSpec((tf, d), lambda i, f: (f, 0)),
                  pl.BlockSpec((1, d), lambda i, f: (0, 0))],
        out_specs=pl.BlockSpec((tm, d), lambda i, f: (i, 0)),
        out_shape=jax.ShapeDtypeStruct((m, d), F32),
        scratch_shapes=[pltpu.VMEM((tm, d), BF16)],
        name="ffn",
        compiler_params=_params(("parallel", "arbitrary"), 63),
    )(x, g.reshape(1, d), w_gate, w_up, w_down, g_final.reshape(1, d))


def _layer(x2d, mem2d, cosf, sinf, batch, seq, mem_len, p):
    n = x2d.shape[0]
    w_in = p["w_in"]
    hn = _norm_bf16(x2d, p["norm_mix_g"])

    c0 = 0
    wide = 1024
    xy = _matmul(_mm_plain_kernel, "proj_xy", hn, w_in, col_off=c0, n_cols=2 * D_RNN, out_dtype=F32,
                 tn=wide)
    c0 += 2 * D_RNN
    tm, tn = 1024, wide
    rope_block = ((tm, LANES), lambda i, j: (i, 0))
    qk, means = _matmul(
        _mm_rope_kernel, "proj_qk", hn, w_in, col_off=c0, n_cols=2 * D_ATTN, out_dtype=BF16,
        tm=tm, tn=tn, extra=(cosf, sinf), extra_blocks=(rope_block, rope_block),
        extra_out_shape=(jax.ShapeDtypeStruct((n // tm, tm // MOBA_BLOCK, 2 * D_ATTN), F32),),
        extra_out_blocks=(((1, tm // MOBA_BLOCK, tn), lambda i, j: (i, 0, j)),))
    c0 += 2 * D_ATTN
    v_t = _matmul(_mm_transposed_kernel, "proj_v", hn, w_in, col_off=c0, n_cols=D_ATTN,
                  out_dtype=BF16, tn=wide, transposed_out=True)
    c0 += D_ATTN
    gates = _matmul(_mm_sigmoid_kernel, "proj_gates", hn, w_in, col_off=c0, n_cols=2 * D_MODEL,
                    out_dtype=BF16, tn=wide)

    gh = _rglru(xy, p["conv_w"], p["conv_b"], p["lru_w_a"], p["lru_b_a"], p["lru_w_i"],
                p["lru_b_i"], p["lru_lambda"], batch, seq)
    kmean = means.reshape(n // MOBA_BLOCK, 2 * D_ATTN)
    o = _moba(qk, v_t, kmean, batch, seq)

    merged = _merge(gh, o, p["w_rnn_proj"], p["w_attn_proj"], gates)
    x1 = _matmul(_mm_residual_kernel, "mix_out", merged, p["w_mix_out"].astype(BF16), col_off=0,
                 n_cols=D_MODEL, out_dtype=F32, tm=512, tn=D_MODEL,
                 extra=(x2d,), extra_blocks=(((512, D_MODEL), lambda i, j: (i, j)),))

    kv = _memkv(mem2d, p["norm_mem_g"], p["w_xkv"].astype(BF16), mem_len)
    x2 = _xattn(x1, p["norm_xq_g"], p["w_xq"].astype(BF16), kv, p["w_xo"].astype(BF16), seq, mem_len)
    return x2


def kernel(x, mem, positions, norm_mix_g, w_in, conv_w, conv_b, lru_w_a, lru_b_a, lru_w_i, lru_b_i,
           lru_lambda, w_rnn_proj, w_attn_proj, w_mix_out, norm_xq_g, norm_mem_g, w_xq, w_xkv, w_xo,
           norm_ffn_g, w_ffn_gate, w_ffn_up, w_ffn_down, norm_final_g):
    batch, seq, d = x.shape
    mem_len = mem.shape[1]
    assert w_in.shape[0] == 1, "only DEPTH == 1 is supported"
    x2d = x.reshape(batch * seq, d)
    mem2d = mem.reshape(batch * mem_len, d)
    cosf, sinf = _rope_tables(positions)
    p = dict(norm_mix_g=norm_mix_g[0], w_in=w_in[0], conv_w=conv_w[0], conv_b=conv_b[0],
             lru_w_a=lru_w_a[0], lru_b_a=lru_b_a[0], lru_w_i=lru_w_i[0], lru_b_i=lru_b_i[0],
             lru_lambda=lru_lambda[0], w_rnn_proj=w_rnn_proj[0], w_attn_proj=w_attn_proj[0],
             w_mix_out=w_mix_out[0], norm_xq_g=norm_xq_g[0], norm_mem_g=norm_mem_g[0],
             w_xq=w_xq[0], w_xkv=w_xkv[0], w_xo=w_xo[0])
    x2 = _layer(x2d, mem2d, cosf, sinf, batch, seq, mem_len, p)
    out = _ffn(x2, norm_ffn_g[0], w_ffn_gate[0].astype(BF16), w_ffn_up[0].astype(BF16),
               w_ffn_down[0].astype(BF16), norm_final_g)
    return out.reshape(batch, seq, d)
```

```python
import functools

import jax
import jax.numpy as jnp
from jax import lax
from jax.experimental import pallas as pl
from jax.experimental.pallas import tpu as pltpu

D_MODEL = 2048
N_HEADS = 16
HEAD_DIM = 128
D_ATTN = N_HEADS * HEAD_DIM
MOBA_BLOCK = 256
MOBA_TOPK = 3
ROPE_THETA = 500000.0
ROT_DIM = HEAD_DIM // 4
ROT_HALF = ROT_DIM // 2
D_RNN = 2048
N_RNN_BLOCKS = 16
RNN_BLOCK = D_RNN // N_RNN_BLOCKS
CONV_WIDTH = 4
LRU_C = 8.0
MEM_HEADS = 4
MEM_HEAD_DIM = 128
D_MEM = MEM_HEADS * MEM_HEAD_DIM
RMS_EPS = 1e-6
NEG_INF = -1e30
LOG2_E = 1.4426950408889634
SOFTMAX_LOG2_SCALE = (HEAD_DIM ** -0.5) * LOG2_E

LANES = 128
SUBLANES = 8
BF16_ROWS = 16
MXU_COLS = 2 * 256
ROT_PARTNER = LANES // 2
MIB = 1024 * 1024

BF16 = jnp.bfloat16
F32 = jnp.float32

_NT = (((1,), (1,)), ((), ()))
_TN = (((0,), (0,)), ((), ()))


def _params(semantics, vmem_mib):
    return pltpu.CompilerParams(dimension_semantics=semantics,
                                vmem_limit_bytes=vmem_mib * MIB)


def _rms(x, g):
    ms = jnp.mean(x * x, axis=-1, keepdims=True)
    return x * lax.rsqrt(ms + RMS_EPS) * g


def _norm_kernel(x_ref, g_ref, o_ref):
    o_ref[...] = _rms(x_ref[...], g_ref[...]).astype(o_ref.dtype)


def _norm_bf16(x, g, tm=512):
    m, d = x.shape
    return pl.pallas_call(
        _norm_kernel,
        grid=(m // tm,),
        in_specs=[pl.BlockSpec((tm, d), lambda i: (i, 0)),
                  pl.BlockSpec((1, d), lambda i: (0, 0))],
        out_specs=pl.BlockSpec((tm, d), lambda i: (i, 0)),
        out_shape=jax.ShapeDtypeStruct((m, d), BF16),
        name="norm_mix",
        compiler_params=_params(("parallel",), 32),
    )(x, g.reshape(1, d))


def _rope_table_kernel(pos_ref, invf_ref, cos_ref, sin_ref):
    ang = pos_ref[...] * invf_ref[...]
    lane = lax.broadcasted_iota(jnp.int32, ang.shape, 1)
    s = jnp.sin(ang)
    cos_ref[...] = jnp.cos(ang)
    sin_ref[...] = jnp.where(lane < ROT_PARTNER, -s, s)


def _rope_head_layout():
    split = ROT_DIM + ROT_PARTNER - ROT_HALF
    return [(0, ROT_HALF), (ROT_DIM, split), (ROT_HALF, ROT_DIM), (split, HEAD_DIM)]


def _cast_weights(w_ref, w_scr):
    @pl.when(pl.program_id(1) == 0)
    def _():
        w_scr[...] = w_ref[...].astype(w_scr.dtype)


def _cast_qk_weights(w_ref, w_scr):
    @pl.when(pl.program_id(1) == 0)
    def _():
        rows = 256
        lane = lax.broadcasted_iota(jnp.int32, (rows, HEAD_DIM), 1)
        for r0 in range(0, w_ref.shape[0], rows):
            for h0 in range(0, w_ref.shape[1], HEAD_DIM):
                x = w_ref[r0:r0 + rows, h0:h0 + HEAD_DIM]
                out, at = x, 0
                for start, stop in _rope_head_layout():
                    if start != at:
                        moved = pltpu.roll(x, (at - start) % HEAD_DIM, 1)
                        out = jnp.where((lane >= at) & (lane < at + stop - start), moved, out)
                    at += stop - start
                w_scr[r0:r0 + rows, h0:h0 + HEAD_DIM] = out.astype(w_scr.dtype)


def _rope_tables(positions, tr=1024):
    n = positions.size
    pos = positions.astype(F32).reshape(n, 1)
    inv_freq = jnp.power(ROPE_THETA, -jnp.arange(ROT_HALF, dtype=F32) / ROT_HALF)
    gap = jnp.zeros((ROT_PARTNER - ROT_HALF,), F32)
    invf = jnp.concatenate([inv_freq, gap, inv_freq, gap]).reshape(1, LANES)
    return pl.pallas_call(
        _rope_table_kernel,
        grid=(n // tr,),
        in_specs=[pl.BlockSpec((tr, 1), lambda i: (i, 0)),
                  pl.BlockSpec((1, LANES), lambda i: (0, 0))],
        out_specs=[pl.BlockSpec((tr, LANES), lambda i: (i, 0)),
                   pl.BlockSpec((tr, LANES), lambda i: (i, 0))],
        out_shape=[jax.ShapeDtypeStruct((n, LANES), F32),
                   jax.ShapeDtypeStruct((n, LANES), F32)],
        name="rope_tables",
        compiler_params=_params(("parallel",), 32),
    )(pos, invf)


def _mm_plain_kernel(a_ref, w_ref, o_ref, w_scr):
    _cast_weights(w_ref, w_scr)
    acc = jnp.dot(a_ref[...], w_scr[...], preferred_element_type=F32)
    o_ref[...] = acc.astype(o_ref.dtype)


def _mm_transposed_kernel(a_ref, w_ref, o_ref, w_scr):
    _cast_weights(w_ref, w_scr)
    for c0 in range(0, w_scr.shape[1], MXU_COLS):
        acc = jnp.dot(a_ref[...], w_scr[:, c0:c0 + MXU_COLS], preferred_element_type=F32)
        o_ref[c0:c0 + MXU_COLS, :] = acc.T.astype(o_ref.dtype)


def _mm_sigmoid_kernel(a_ref, w_ref, o_ref, w_scr):
    _cast_weights(w_ref, w_scr)
    for c0 in range(0, w_scr.shape[1], MXU_COLS):
        acc = jnp.dot(a_ref[...], w_scr[:, c0:c0 + MXU_COLS], preferred_element_type=F32)
        o_ref[:, c0:c0 + MXU_COLS] = (0.5 * jnp.tanh(0.5 * acc) + 0.5).astype(o_ref.dtype)


def _mm_rope_kernel(a_ref, w_ref, cos_ref, sin_ref, o_ref, mean_ref, w_scr):
    _cast_qk_weights(w_ref, w_scr)
    tm, tn = o_ref.shape
    q_scale = jnp.where(pl.program_id(0) < D_ATTN // tn, SOFTMAX_LOG2_SCALE, 1.0)
    cosf = cos_ref[...] * q_scale
    sinf = sin_ref[...] * q_scale
    for c0 in range(0, tn, MXU_COLS):
        acc = jnp.dot(a_ref[...], w_scr[:, c0:c0 + MXU_COLS], preferred_element_type=F32)
        for h0 in range(0, MXU_COLS, HEAD_DIM):
            a = acc[:, h0:h0 + HEAD_DIM]
            r = a * cosf + pltpu.roll(a, ROT_PARTNER, 1) * sinf
            cols = slice(c0 + h0, c0 + h0 + HEAD_DIM)
            o_ref[:, cols] = r.astype(o_ref.dtype)
            mean_ref[0, :, cols] = jnp.mean(r.reshape(tm // MOBA_BLOCK, MOBA_BLOCK, HEAD_DIM), axis=1)


def _matmul(kernel, name, a, w, *, col_off, n_cols, out_dtype, tm=1024, tn=512,
            extra=(), extra_blocks=(), extra_out_shape=(), extra_out_blocks=(), vmem_mib=48,
            transposed_out=False):
    m, k = a.shape
    off = col_off // tn
    spec = lambda shape, fn: pl.BlockSpec(shape, lambda j, i: fn(i, j))
    if transposed_out:
        main_shape, main_spec = (n_cols, m), spec((tn, tm), lambda i, j: (j, i))
    else:
        main_shape, main_spec = (m, n_cols), spec((tm, tn), lambda i, j: (i, j))
    out_shape = [jax.ShapeDtypeStruct(main_shape, out_dtype)] + list(extra_out_shape)
    out_specs = [main_spec] + [spec(*blk) for blk in extra_out_blocks]
    res = pl.pallas_call(
        kernel,
        grid=(n_cols // tn, m // tm),
        in_specs=[spec((tm, k), lambda i, j: (i, 0)),
                  spec((k, tn), lambda i, j: (0, j + off))] + [spec(*blk) for blk in extra_blocks],
        out_specs=out_specs,
        out_shape=out_shape,
        scratch_shapes=[pltpu.VMEM((k, tn), BF16)],
        name=name,
        compiler_params=_params(("parallel", "arbitrary"), vmem_mib),
    )(a, w, *extra)
    return res if extra_out_shape else res[0]


def _rglru_kernel(xr_ref, yr_ref, cw_ref, cb_ref, wa_ref, wi_ref, ba_ref, bi_ref, lam_ref,
                  o_ref, xbuf, a_scr, u_scr, h_scr, nat_scr, tail_scr, hc_scr):
    ts, tc = xr_ref.shape
    nv = ts // SUBLANES
    halo = (CONV_WIDTH - 1) * SUBLANES
    t = pl.program_id(2)

    def grp(g):
        return slice(halo + g * SUBLANES, halo + (g + 1) * SUBLANES)

    @pl.when(t == 0)
    def _():
        tail_scr[...] = jnp.zeros_like(tail_scr)
        hc_scr[...] = jnp.zeros_like(hc_scr)

    lane_blocks = [slice(cb * LANES, (cb + 1) * LANES) for cb in range(tc // LANES)]
    per_seg = nv // SUBLANES
    for cb, ls in enumerate(lane_blocks):
        for s in range(SUBLANES):
            for q in range(per_seg):
                t0 = s * nv + q * SUBLANES
                dst = pl.ds(halo + q * SUBLANES * SUBLANES + s, SUBLANES, stride=SUBLANES)
                xbuf[cb, dst, :] = xr_ref[t0:t0 + SUBLANES, ls]
    sub = lax.broadcasted_iota(jnp.int32, (SUBLANES, LANES), 0)
    for cb, ls in enumerate(lane_blocks):
        for d in range(1, CONV_WIDTH):
            keep = slice((CONV_WIDTH - 1 - d) * SUBLANES, (CONV_WIDTH - d) * SUBLANES)
            cur = xbuf[cb, grp(nv - d), :]
            xbuf[cb, grp(-d), :] = pltpu.roll(
                jnp.where(sub == SUBLANES - 1, tail_scr[keep, ls], cur), 1, 0)
            tail_scr[keep, ls] = cur

    half_a = (-0.5 * LRU_C) * jax.nn.softplus(-lam_ref[...])
    for nb, sl in enumerate(lane_blocks):
        xb = jnp.zeros((ts, LANES), F32) + cb_ref[:, sl]
        for kk in range(CONV_WIDTH):
            start = halo - (CONV_WIDTH - 1 - kk) * SUBLANES
            xb = xb + cw_ref[kk:kk + 1, sl] * xbuf[nb, start:start + ts, :]
        xb16 = xb.astype(BF16)
        zr_half = jnp.dot(xb16, wa_ref[nb], preferred_element_type=F32) + ba_ref[:, sl]
        zi_half = jnp.dot(xb16, wi_ref[nb], preferred_element_type=F32) + bi_ref[:, sl]
        log_a = half_a[:, sl] * jnp.tanh(zr_half) + half_a[:, sl]
        ig = 0.5 * jnp.tanh(zi_half) + 0.5
        a_scr[:, sl] = jnp.exp(log_a)
        th = jnp.tanh(log_a)
        y = -2.0 * th / (1.0 - th)
        u_scr[:, sl] = jnp.where(y > 0.0, y * lax.rsqrt(y), 0.0) * (ig * xb)

    h_end = jnp.zeros((SUBLANES, tc), F32)
    p_end = jnp.ones((SUBLANES, tc), F32)
    for v in range(nv):
        rows = slice(v * SUBLANES, (v + 1) * SUBLANES)
        a = a_scr[rows, :]
        h_end = a * h_end + u_scr[rows, :]
        p_end = a * p_end
        h_scr[rows, :] = h_end
        a_scr[rows, :] = p_end
    h_in = hc_scr[...]
    entering = []
    for s in range(SUBLANES):
        entering.append(h_in)
        h_in = h_end[s:s + 1, :] + p_end[s:s + 1, :] * h_in
    hc_scr[...] = h_in
    h_enter = jnp.concatenate(entering, axis=0)

    pitch = nat_scr.shape[1] // SUBLANES
    for v in range(nv):
        rows = slice(v * SUBLANES, (v + 1) * SUBLANES)
        h_v = h_scr[rows, :] + a_scr[rows, :] * h_enter
        for cb, ls in enumerate(lane_blocks):
            nat_scr[cb, pl.ds(v, SUBLANES, stride=pitch), :] = h_v[:, ls]
    for cb, ls in enumerate(lane_blocks):
        for s in range(SUBLANES):
            seg = slice(s * nv, (s + 1) * nv)
            o_ref[seg, ls] = (jax.nn.gelu(yr_ref[seg, ls])
                              * nat_scr[cb, s * pitch:s * pitch + nv, :]).astype(o_ref.dtype)


def _rglru(xy, conv_w, conv_b, w_a, b_a, w_i, b_i, lam, batch, seq, ts=512, tc=512):
    assert RNN_BLOCK == LANES, "the kernel walks gate blocks and 128-lane blocks together"
    n = batch * seq
    nt = seq // ts
    ncb = D_RNN // tc
    halo = (CONV_WIDTH - 1) * SUBLANES
    row = lambda v: v.reshape(1, D_RNN)
    vec_spec = pl.BlockSpec((1, tc), lambda b, c, t: (0, c))
    gate_spec = pl.BlockSpec((tc // RNN_BLOCK, RNN_BLOCK, RNN_BLOCK), lambda b, c, t: (c, 0, 0))
    return pl.pallas_call(
        _rglru_kernel,
        grid=(batch, ncb, nt),
        in_specs=[pl.BlockSpec((ts, tc), lambda b, c, t: (b * nt + t, c)),
                  pl.BlockSpec((ts, tc), lambda b, c, t: (b * nt + t, ncb + c)),
                  pl.BlockSpec((CONV_WIDTH, tc), lambda b, c, t: (0, c)),
                  vec_spec, gate_spec, gate_spec, vec_spec, vec_spec, vec_spec],
        out_specs=pl.BlockSpec((ts, tc), lambda b, c, t: (b * nt + t, c)),
        out_shape=jax.ShapeDtypeStruct((n, D_RNN), BF16),
        scratch_shapes=[pltpu.VMEM((tc // LANES, halo + ts, LANES), F32),
                        pltpu.VMEM((ts, tc), F32),
                        pltpu.VMEM((ts, tc), F32),
                        pltpu.VMEM((ts, tc), F32),
                        pltpu.VMEM((tc // LANES, ts + SUBLANES * SUBLANES, LANES), F32),
                        pltpu.VMEM((halo, tc), F32),
                        pltpu.VMEM((1, tc), F32)],
        name="rglru",
        compiler_params=_params(("parallel", "parallel", "arbitrary"), 32),
    )(xy, xy, conv_w, row(conv_b), (0.5 * w_a).astype(BF16), (0.5 * w_i).astype(BF16),
      row(0.5 * b_a), row(0.5 * b_i), row(lam))


def _moba_kernel(q_ref, k_ref, v_ref, km_ref, o_ref, bias_scr, s_scr, acc_scr, *, heads, tiles):
    first = pl.program_id(2) * tiles

    def one(u, carry):
        qrows = pl.ds(pl.multiple_of(u * MOBA_BLOCK, MOBA_BLOCK), MOBA_BLOCK)
        _moba_tile(first + u, qrows, q_ref, k_ref, v_ref, km_ref, o_ref, bias_scr, s_scr, acc_scr, heads)
        return carry

    lax.fori_loop(0, tiles, one, 0)


def _moba_tile(j, qrows, q_ref, k_ref, v_ref, km_ref, o_ref, bias_scr, s_scr, acc_scr, heads):
    nblk = km_ref.shape[0]
    blk = MOBA_BLOCK
    head_slices = [slice(h * HEAD_DIM, (h + 1) * HEAD_DIM) for h in range(heads)]

    def score_into(slot, block):
        start = pl.multiple_of(block * blk, blk)
        for h, hs in enumerate(head_slices):
            s_scr[slot, h] = lax.dot_general(k_ref[pl.ds(start, blk), hs], q_ref[qrows, hs], _NT,
                                             preferred_element_type=F32)

    ones_rows = jnp.ones((BF16_ROWS, blk), BF16)

    def attend(slot, mask, block, state):
        start = pl.multiple_of(block * blk, blk)
        half = blk // 2
        soft = []
        for h in range(heads):
            m = state[h]
            unselected = bias_scr[h, pl.ds(block, 1), :] < 0.0
            s_lo = mask(s_scr[slot, h, 0:half, :], 0)
            s_hi = mask(s_scr[slot, h, half:blk, :], half)
            m_blk = jnp.maximum(jnp.max(s_lo, axis=0, keepdims=True), jnp.max(s_hi, axis=0, keepdims=True))
            m_new = jnp.where(unselected, m, jnp.maximum(m, m_blk))
            alpha = jnp.exp2(m - m_new)
            p = jnp.exp2(mask(s_scr[slot, h], 0) - jnp.where(unselected, -NEG_INF, m_new))
            soft.append((m_new, alpha, p.astype(BF16)))
        out = []
        for h, hs in enumerate(head_slices):
            m_new, alpha, p = soft[h]
            vt = jnp.concatenate([v_ref[hs, pl.ds(start, blk)], ones_rows], axis=0)
            pv = jnp.dot(vt, p, preferred_element_type=F32)
            acc_scr[h] = alpha * acc_scr[h] + pv
            out.append(m_new)
        return tuple(out)

    no_mask = lambda s, row0: s

    gates = []
    for hs in head_slices:
        q = q_ref[qrows, hs]
        km = km_ref[:, hs]
        km_hi = km.astype(BF16)
        km_lo = (km - km_hi.astype(F32)).astype(BF16)
        gates.append(lax.dot_general(km_hi, q, _NT, preferred_element_type=F32)
                     + lax.dot_general(km_lo, q, _NT, preferred_element_type=F32))
    score_into(0, 0)
    for h, gate in enumerate(gates):
        bidx = lax.broadcasted_iota(jnp.int32, gate.shape, 0)
        past = bidx < j
        g = jnp.where(past, gate, -jnp.inf)
        rank = jnp.zeros(gate.shape, jnp.int32)
        for other in range(nblk):
            go = g[other:other + 1, :]
            beats = jnp.where(go > g, 1, jnp.where(go == g, jnp.where(bidx > other, 1, 0), 0))
            rank = rank + beats
        bias_scr[h] = jnp.where(past, jnp.where(rank < MOBA_TOPK, 0.0, NEG_INF),
                                jnp.where(bidx == j, 0.0, NEG_INF))

    def pair_body(t, state):
        first = 2 * t
        score_into(1, first + 1)
        state = attend(0, no_mask, first, state)
        score_into(0, first + 2)
        return attend(1, no_mask, first + 1, state)

    def odd_body(state):
        score_into(1, j)
        return attend(0, no_mask, j - 1, state)

    acc_scr[...] = jnp.zeros_like(acc_scr)
    init = tuple(jnp.full((1, blk), NEG_INF, F32) for _ in range(heads))
    state = lax.fori_loop(0, lax.shift_right_logical(j, 1), pair_body, init)
    odd = lax.bitwise_and(j, 1)
    state = lax.cond(odd == 1, odd_body, lambda st: st, state)

    def causal_mask(s, row0):
        kpos = row0 + lax.broadcasted_iota(jnp.int32, s.shape, 0)
        qpos = lax.broadcasted_iota(jnp.int32, s.shape, 1)
        return jnp.where(kpos <= qpos, s, NEG_INF)

    attend(odd, causal_mask, j, state)
    for h, hs in enumerate(head_slices):
        o_ref[qrows, hs] = (acc_scr[h, 0:HEAD_DIM, :] / acc_scr[h, HEAD_DIM:HEAD_DIM + 1, :]
                        ).T.astype(o_ref.dtype)


def _moba(qk, v_t, kmean, batch, seq, heads=8, tiles=4):
    n = batch * seq
    nblk = seq // MOBA_BLOCK
    groups = N_HEADS // heads
    width = heads * HEAD_DIM
    steps = nblk // tiles
    rows = tiles * MOBA_BLOCK
    return pl.pallas_call(
        functools.partial(_moba_kernel, heads=heads, tiles=tiles),
        grid=(batch, groups, steps),
        in_specs=[pl.BlockSpec((rows, width), lambda b, h, j: (b * steps + j, h)),
                  pl.BlockSpec((seq, width), lambda b, h, j: (b, groups + h)),
                  pl.BlockSpec((width, seq), lambda b, h, j: (h, b)),
                  pl.BlockSpec((nblk, width), lambda b, h, j: (b, groups + h))],
        out_specs=pl.BlockSpec((rows, width), lambda b, h, j: (b * steps + j, h)),
        out_shape=jax.ShapeDtypeStruct((n, D_ATTN), BF16),
        scratch_shapes=[pltpu.VMEM((heads, nblk, MOBA_BLOCK), F32),
                        pltpu.VMEM((2, heads, MOBA_BLOCK, MOBA_BLOCK), F32),
                        pltpu.VMEM((heads, HEAD_DIM + BF16_ROWS, MOBA_BLOCK), F32)],
        name="moba",
        compiler_params=_params(("parallel", "parallel", "arbitrary"), 56),
    )(qk, qk, v_t, kmean)


def _merge_kernel(gh_ref, o_ref, wr_ref, wa_ref, gr_ref, ga_ref, out_ref, wr_scr, wa_scr):
    _cast_weights(wr_ref, wr_scr)
    _cast_weights(wa_ref, wa_scr)
    rnn = jnp.dot(gh_ref[...], wr_scr[...], preferred_element_type=F32)
    att = jnp.dot(o_ref[...], wa_scr[...], preferred_element_type=F32)
    out_ref[...] = (gr_ref[...].astype(F32) * rnn + ga_ref[...].astype(F32) * att).astype(out_ref.dtype)


def _merge(gh, o, w_rnn, w_attn, gates, tm=1024, tn=512):
    m, k = gh.shape
    nj = D_MODEL // tn
    a_spec = pl.BlockSpec((tm, k), lambda j, i: (i, 0))
    w_spec = pl.BlockSpec((k, tn), lambda j, i: (0, j))
    return pl.pallas_call(
        _merge_kernel,
        grid=(nj, m // tm),
        in_specs=[a_spec, a_spec, w_spec, w_spec,
                  pl.BlockSpec((tm, tn), lambda j, i: (i, j)),
                  pl.BlockSpec((tm, tn), lambda j, i: (i, nj + j))],
        out_specs=pl.BlockSpec((tm, tn), lambda j, i: (i, j)),
        out_shape=jax.ShapeDtypeStruct((m, D_MODEL), BF16),
        scratch_shapes=[pltpu.VMEM((k, tn), BF16), pltpu.VMEM((k, tn), BF16)],
        name="merge",
        compiler_params=_params(("parallel", "arbitrary"), 48),
    )(gh, o, w_rnn, w_attn, gates, gates)


def _memkv_kernel(mem_ref, g_ref, w_ref, o_ref):
    hn = _rms(mem_ref[...], g_ref[...]).astype(BF16)
    o_ref[...] = jnp.dot(hn, w_ref[...], preferred_element_type=F32).astype(o_ref.dtype)


def _memkv(mem2d, g, w_kv, mem_len):
    m, d = mem2d.shape
    return pl.pallas_call(
        _memkv_kernel,
        grid=(m // mem_len,),
        in_specs=[pl.BlockSpec((mem_len, d), lambda i: (i, 0)),
                  pl.BlockSpec((1, d), lambda i: (0, 0)),
                  pl.BlockSpec((d, 2 * D_MEM), lambda i: (0, 0))],
        out_specs=pl.BlockSpec((mem_len, 2 * D_MEM), lambda i: (i, 0)),
        out_shape=jax.ShapeDtypeStruct((m, 2 * D_MEM), BF16),
        name="mem_kv",
        compiler_params=_params(("parallel",), 32),
    )(mem2d, g.reshape(1, d), w_kv)


def _xattn_kernel(mg_ref, wm_ref, x_ref, g_ref, wq_ref, kv_ref, wo_ref, o_ref):
    x = x_ref[...] + jnp.dot(mg_ref[...], wm_ref[...], preferred_element_type=F32)
    hn = _rms(x, g_ref[...]).astype(BF16)
    q = jnp.dot(hn, wq_ref[...], preferred_element_type=F32).astype(BF16)
    scale = MEM_HEAD_DIM ** -0.5
    heads = []
    for hd in range(MEM_HEADS):
        sl = slice(hd * MEM_HEAD_DIM, (hd + 1) * MEM_HEAD_DIM)
        kh = kv_ref[:, sl]
        vh = kv_ref[:, D_MEM + hd * MEM_HEAD_DIM:D_MEM + (hd + 1) * MEM_HEAD_DIM]
        s = lax.dot_general(q[:, sl], kh, _NT, preferred_element_type=F32) * scale
        m = jnp.max(s, axis=-1, keepdims=True)
        p = jnp.exp(s - m)
        l = jnp.sum(p, axis=-1, keepdims=True)
        oh = jnp.dot(p.astype(BF16), vh, preferred_element_type=F32) / l
        heads.append(oh.astype(BF16))
    o_all = jnp.concatenate(heads, axis=-1)
    o_ref[...] = x + jnp.dot(o_all, wo_ref[...], preferred_element_type=F32)


def _xattn(merged, w_mix, x, g, w_q, kv, w_o, seq, mem_len, tm=512):
    m, d = x.shape
    per_batch = seq // tm
    return pl.pallas_call(
        _xattn_kernel,
        grid=(m // tm,),
        in_specs=[pl.BlockSpec((tm, merged.shape[1]), lambda i: (i, 0)),
                  pl.BlockSpec(w_mix.shape, lambda i: (0, 0)),
                  pl.BlockSpec((tm, d), lambda i: (i, 0)),
                  pl.BlockSpec((1, d), lambda i: (0, 0)),
                  pl.BlockSpec((d, D_MEM), lambda i: (0, 0)),
                  pl.BlockSpec((mem_len, 2 * D_MEM), lambda i: (i // per_batch, 0)),
                  pl.BlockSpec((D_MEM, d), lambda i: (0, 0))],
        out_specs=pl.BlockSpec((tm, d), lambda i: (i, 0)),
        out_shape=jax.ShapeDtypeStruct((m, d), F32),
        name="mixout_xattn",
        compiler_params=_params(("parallel",), 56),
    )(merged, w_mix, x, g.reshape(1, d), w_q, kv, w_o)


def _ffn_kernel(x_ref, g_ref, wg_ref, wu_ref, wd_ref, gf_ref, o_ref, h_scr):
    f = pl.program_id(1)

    @pl.when(f == 0)
    def _():
        h_scr[...] = _rms(x_ref[...], g_ref[...]).astype(BF16)
        o_ref[...] = jnp.zeros_like(o_ref)

    hn = h_scr[...]
    a = jnp.dot(hn, wg_ref[...], preferred_element_type=F32)
    b = jnp.dot(hn, wu_ref[...], preferred_element_type=F32)
    act = (jax.nn.silu(a) * b).astype(BF16)
    o_ref[...] += jnp.dot(act, wd_ref[...], preferred_element_type=F32)

    @pl.when(f == pl.num_programs(1) - 1)
    def _():
        o_ref[...] = _rms(x_ref[...] + o_ref[...], gf_ref[...])


def _ffn(x, g, w_gate, w_up, w_down, g_final, tm=1024, tf=512):
    m, d = x.shape
    d_ff = w_gate.shape[1]
    return pl.pallas_call(
        _ffn_kernel,
        grid=(m // tm, d_ff // tf),
        in_specs=[pl.BlockSpec((tm, d), lambda i, f: (i, 0)),
                  pl.BlockSpec((1, d), lambda i, f: (0, 0)),
                  pl.BlockSpec((d, tf), lambda i, f: (0, f)),
                  pl.BlockSpec((d, tf), lambda i, f: (0, f)),
                  pl.BlockSpec((tf, d), lambda i, f: (f, 0)),
                  pl.BlockSpec((1, d), lambda i, f: (0, 0))],
        out_specs=pl.BlockSpec((tm, d), lambda i, f: (i, 0)),
        out_shape=jax.ShapeDtypeStruct((m, d), F32),
        scratch_shapes=[pltpu.VMEM((tm, d), BF16)],
        name="ffn",
        compiler_params=_params(("parallel", "arbitrary"), 63),
    )(x, g.reshape(1, d), w_gate, w_up, w_down, g_final.reshape(1, d))


def _layer(x2d, mem2d, cosf, sinf, batch, seq, mem_len, p):
    n = x2d.shape[0]
    w_in = p["w_in"]
    hn = _norm_bf16(x2d, p["norm_mix_g"])

    c0 = 0
    wide = 1024
    tall = 2048
    xy = _matmul(_mm_plain_kernel, "proj_xy", hn, w_in, col_off=c0, n_cols=2 * D_RNN, out_dtype=F32,
                 tn=wide)
    c0 += 2 * D_RNN
    tm, tn = 1024, wide
    rope_block = ((tm, LANES), lambda i, j: (i, 0))
    qk, means = _matmul(
        _mm_rope_kernel, "proj_qk", hn, w_in, col_off=c0, n_cols=2 * D_ATTN, out_dtype=BF16,
        tm=tm, tn=tn, extra=(cosf, sinf), extra_blocks=(rope_block, rope_block),
        extra_out_shape=(jax.ShapeDtypeStruct((n // tm, tm // MOBA_BLOCK, 2 * D_ATTN), F32),),
        extra_out_blocks=(((1, tm // MOBA_BLOCK, tn), lambda i, j: (i, 0, j)),))
    c0 += 2 * D_ATTN
    v_t = _matmul(_mm_transposed_kernel, "proj_v", hn, w_in, col_off=c0, n_cols=D_ATTN,
                  out_dtype=BF16, tm=tall, tn=wide, transposed_out=True, vmem_mib=60)
    c0 += D_ATTN
    gates = _matmul(_mm_sigmoid_kernel, "proj_gates", hn, w_in, col_off=c0, n_cols=2 * D_MODEL,
                    out_dtype=BF16, tm=tall, tn=wide, vmem_mib=60)

    gh = _rglru(xy, p["conv_w"], p["conv_b"], p["lru_w_a"], p["lru_b_a"], p["lru_w_i"],
                p["lru_b_i"], p["lru_lambda"], batch, seq)
    kmean = means.reshape(n // MOBA_BLOCK, 2 * D_ATTN)
    o = _moba(qk, v_t, kmean, batch, seq)

    merged = _merge(gh, o, p["w_rnn_proj"], p["w_attn_proj"], gates)
    kv = _memkv(mem2d, p["norm_mem_g"], p["w_xkv"].astype(BF16), mem_len)
    return _xattn(merged, p["w_mix_out"].astype(BF16), x2d, p["norm_xq_g"], p["w_xq"].astype(BF16),
                  kv, p["w_xo"].astype(BF16), seq, mem_len)


def kernel(x, mem, positions, norm_mix_g, w_in, conv_w, conv_b, lru_w_a, lru_b_a, lru_w_i, lru_b_i,
           lru_lambda, w_rnn_proj, w_attn_proj, w_mix_out, norm_xq_g, norm_mem_g, w_xq, w_xkv, w_xo,
           norm_ffn_g, w_ffn_gate, w_ffn_up, w_ffn_down, norm_final_g):
    batch, seq, d = x.shape
    mem_len = mem.shape[1]
    assert w_in.shape[0] == 1, "only DEPTH == 1 is supported"
    x2d = x.reshape(batch * seq, d)
    mem2d = mem.reshape(batch * mem_len, d)
    cosf, sinf = _rope_tables(positions)
    p = dict(norm_mix_g=norm_mix_g[0], w_in=w_in[0], conv_w=conv_w[0], conv_b=conv_b[0],
             lru_w_a=lru_w_a[0], lru_b_a=lru_b_a[0], lru_w_i=lru_w_i[0], lru_b_i=lru_b_i[0],
             lru_lambda=lru_lambda[0], w_rnn_proj=w_rnn_proj[0], w_attn_proj=w_attn_proj[0],
             w_mix_out=w_mix_out[0], norm_xq_g=norm_xq_g[0], norm_mem_g=norm_mem_g[0],
             w_xq=w_xq[0], w_xkv=w_xkv[0], w_xo=w_xo[0])
    x2 = _layer(x2d, mem2d, cosf, sinf, batch, seq, mem_len, p)
    out = _ffn(x2, norm_ffn_g[0], w_ffn_gate[0].astype(BF16), w_ffn_up[0].astype(BF16),
               w_ffn_down[0].astype(BF16), norm_final_g)
    return out.reshape(batch, seq, d)
```

```python
import functools

import jax
import jax.numpy as jnp
from jax import lax
from jax.experimental import pallas as pl
from jax.experimental.pallas import tpu as pltpu

D_MODEL = 2048
N_HEADS = 16
HEAD_DIM = 128
D_ATTN = N_HEADS * HEAD_DIM
MOBA_BLOCK = 256
MOBA_TOPK = 3
ROPE_THETA = 500000.0
ROT_DIM = HEAD_DIM // 4
ROT_HALF = ROT_DIM // 2
D_RNN = 2048
N_RNN_BLOCKS = 16
RNN_BLOCK = D_RNN // N_RNN_BLOCKS
CONV_WIDTH = 4
LRU_C = 8.0
MEM_HEADS = 4
MEM_HEAD_DIM = 128
D_MEM = MEM_HEADS * MEM_HEAD_DIM
RMS_EPS = 1e-6
NEG_INF = -1e30
LOG2_E = 1.4426950408889634
SOFTMAX_LOG2_SCALE = (HEAD_DIM ** -0.5) * LOG2_E

LANES = 128
SUBLANES = 8
BF16_ROWS = 16
MXU_COLS = 2 * 256
ROT_PARTNER = LANES // 2
MIB = 1024 * 1024

BF16 = jnp.bfloat16
F32 = jnp.float32

_NT = (((1,), (1,)), ((), ()))
_TN = (((0,), (0,)), ((), ()))


def _params(semantics, vmem_mib):
    return pltpu.CompilerParams(dimension_semantics=semantics,
                                vmem_limit_bytes=vmem_mib * MIB)


def _rms(x, g):
    ms = jnp.mean(x * x, axis=-1, keepdims=True)
    return x * lax.rsqrt(ms + RMS_EPS) * g


def _norm_kernel(x_ref, g_ref, o_ref):
    o_ref[...] = _rms(x_ref[...], g_ref[...]).astype(o_ref.dtype)


def _norm_bf16(x, g, tm=512):
    m, d = x.shape
    return pl.pallas_call(
        _norm_kernel,
        grid=(m // tm,),
        in_specs=[pl.BlockSpec((tm, d), lambda i: (i, 0)),
                  pl.BlockSpec((1, d), lambda i: (0, 0))],
        out_specs=pl.BlockSpec((tm, d), lambda i: (i, 0)),
        out_shape=jax.ShapeDtypeStruct((m, d), BF16),
        name="norm_mix",
        compiler_params=_params(("parallel",), 32),
    )(x, g.reshape(1, d))


def _rope_table_kernel(pos_ref, invf_ref, cos_ref, sin_ref):
    ang = pos_ref[...] * invf_ref[...]
    lane = lax.broadcasted_iota(jnp.int32, ang.shape, 1)
    s = jnp.sin(ang)
    cos_ref[...] = jnp.cos(ang)
    sin_ref[...] = jnp.where(lane < ROT_PARTNER, -s, s)


def _rope_head_layout():
    split = ROT_DIM + ROT_PARTNER - ROT_HALF
    return [(0, ROT_HALF), (ROT_DIM, split), (ROT_HALF, ROT_DIM), (split, HEAD_DIM)]


def _cast_weights(w_ref, w_scr):
    @pl.when(pl.program_id(1) == 0)
    def _():
        w_scr[...] = w_ref[...].astype(w_scr.dtype)


def _cast_qk_weights(w_ref, w_scr):
    @pl.when(pl.program_id(1) == 0)
    def _():
        rows = 256
        lane = lax.broadcasted_iota(jnp.int32, (rows, HEAD_DIM), 1)
        for r0 in range(0, w_ref.shape[0], rows):
            for h0 in range(0, w_ref.shape[1], HEAD_DIM):
                x = w_ref[r0:r0 + rows, h0:h0 + HEAD_DIM]
                out, at = x, 0
                for start, stop in _rope_head_layout():
                    if start != at:
                        moved = pltpu.roll(x, (at - start) % HEAD_DIM, 1)
                        out = jnp.where((lane >= at) & (lane < at + stop - start), moved, out)
                    at += stop - start
                w_scr[r0:r0 + rows, h0:h0 + HEAD_DIM] = out.astype(w_scr.dtype)


def _rope_tables(positions, tr=1024):
    n = positions.size
    pos = positions.astype(F32).reshape(n, 1)
    inv_freq = jnp.power(ROPE_THETA, -jnp.arange(ROT_HALF, dtype=F32) / ROT_HALF)
    gap = jnp.zeros((ROT_PARTNER - ROT_HALF,), F32)
    invf = jnp.concatenate([inv_freq, gap, inv_freq, gap]).reshape(1, LANES)
    return pl.pallas_call(
        _rope_table_kernel,
        grid=(n // tr,),
        in_specs=[pl.BlockSpec((tr, 1), lambda i: (i, 0)),
                  pl.BlockSpec((1, LANES), lambda i: (0, 0))],
        out_specs=[pl.BlockSpec((tr, LANES), lambda i: (i, 0)),
                   pl.BlockSpec((tr, LANES), lambda i: (i, 0))],
        out_shape=[jax.ShapeDtypeStruct((n, LANES), F32),
                   jax.ShapeDtypeStruct((n, LANES), F32)],
        name="rope_tables",
        compiler_params=_params(("parallel",), 32),
    )(pos, invf)


def _mm_plain_kernel(a_ref, w_ref, o_ref, w_scr):
    _cast_weights(w_ref, w_scr)
    acc = jnp.dot(a_ref[...], w_scr[...], preferred_element_type=F32)
    o_ref[...] = acc.astype(o_ref.dtype)


def _mm_transposed_kernel(a_ref, w_ref, o_ref, w_scr):
    _cast_weights(w_ref, w_scr)
    for c0 in range(0, w_scr.shape[1], MXU_COLS):
        acc = jnp.dot(a_ref[...], w_scr[:, c0:c0 + MXU_COLS], preferred_element_type=F32)
        o_ref[c0:c0 + MXU_COLS, :] = acc.T.astype(o_ref.dtype)


def _mm_sigmoid_kernel(a_ref, w_ref, o_ref, w_scr):
    _cast_weights(w_ref, w_scr)
    for c0 in range(0, w_scr.shape[1], MXU_COLS):
        acc = jnp.dot(a_ref[...], w_scr[:, c0:c0 + MXU_COLS], preferred_element_type=F32)
        o_ref[:, c0:c0 + MXU_COLS] = (0.5 * jnp.tanh(0.5 * acc) + 0.5).astype(o_ref.dtype)


def _mm_rope_kernel(a_ref, w_ref, cos_ref, sin_ref, o_ref, mean_ref, w_scr):
    _cast_qk_weights(w_ref, w_scr)
    tm, tn = o_ref.shape
    q_scale = jnp.where(pl.program_id(0) < D_ATTN // tn, SOFTMAX_LOG2_SCALE, 1.0)
    cosf = cos_ref[...] * q_scale
    sinf = sin_ref[...] * q_scale
    for c0 in range(0, tn, MXU_COLS):
        acc = jnp.dot(a_ref[...], w_scr[:, c0:c0 + MXU_COLS], preferred_element_type=F32)
        for h0 in range(0, MXU_COLS, HEAD_DIM):
            a = acc[:, h0:h0 + HEAD_DIM]
            r = a * cosf + pltpu.roll(a, ROT_PARTNER, 1) * sinf
            cols = slice(c0 + h0, c0 + h0 + HEAD_DIM)
            o_ref[:, cols] = r.astype(o_ref.dtype)
            mean_ref[0, :, cols] = jnp.mean(r.reshape(tm // MOBA_BLOCK, MOBA_BLOCK, HEAD_DIM), axis=1)


def _matmul(kernel, name, a, w, *, col_off, n_cols, out_dtype, tm=1024, tn=512,
            extra=(), extra_blocks=(), extra_out_shape=(), extra_out_blocks=(), vmem_mib=48,
            transposed_out=False):
    m, k = a.shape
    off = col_off // tn
    spec = lambda shape, fn: pl.BlockSpec(shape, lambda j, i: fn(i, j))
    if transposed_out:
        main_shape, main_spec = (n_cols, m), spec((tn, tm), lambda i, j: (j, i))
    else:
        main_shape, main_spec = (m, n_cols), spec((tm, tn), lambda i, j: (i, j))
    out_shape = [jax.ShapeDtypeStruct(main_shape, out_dtype)] + list(extra_out_shape)
    out_specs = [main_spec] + [spec(*blk) for blk in extra_out_blocks]
    res = pl.pallas_call(
        kernel,
        grid=(n_cols // tn, m // tm),
        in_specs=[spec((tm, k), lambda i, j: (i, 0)),
                  spec((k, tn), lambda i, j: (0, j + off))] + [spec(*blk) for blk in extra_blocks],
        out_specs=out_specs,
        out_shape=out_shape,
        scratch_shapes=[pltpu.VMEM((k, tn), BF16)],
        name=name,
        compiler_params=_params(("parallel", "arbitrary"), vmem_mib),
    )(a, w, *extra)
    return res if extra_out_shape else res[0]


def _rglru_kernel(xr_ref, yr_ref, cw_ref, cb_ref, wa_ref, wi_ref, ba_ref, bi_ref, lam_ref,
                  o_ref, xbuf, a_scr, u_scr, h_scr, nat_scr, tail_scr, hc_scr):
    ts, tc = xr_ref.shape
    nv = ts // SUBLANES
    halo = (CONV_WIDTH - 1) * SUBLANES
    t = pl.program_id(2)

    def grp(g):
        return slice(halo + g * SUBLANES, halo + (g + 1) * SUBLANES)

    @pl.when(t == 0)
    def _():
        tail_scr[...] = jnp.zeros_like(tail_scr)
        hc_scr[...] = jnp.zeros_like(hc_scr)

    lane_blocks = [slice(cb * LANES, (cb + 1) * LANES) for cb in range(tc // LANES)]
    per_seg = nv // SUBLANES
    for cb, ls in enumerate(lane_blocks):
        for s in range(SUBLANES):
            for q in range(per_seg):
                t0 = s * nv + q * SUBLANES
                dst = pl.ds(halo + q * SUBLANES * SUBLANES + s, SUBLANES, stride=SUBLANES)
                xbuf[cb, dst, :] = xr_ref[t0:t0 + SUBLANES, ls]
    sub = lax.broadcasted_iota(jnp.int32, (SUBLANES, LANES), 0)
    for cb, ls in enumerate(lane_blocks):
        for d in range(1, CONV_WIDTH):
            keep = slice((CONV_WIDTH - 1 - d) * SUBLANES, (CONV_WIDTH - d) * SUBLANES)
            cur = xbuf[cb, grp(nv - d), :]
            xbuf[cb, grp(-d), :] = pltpu.roll(
                jnp.where(sub == SUBLANES - 1, tail_scr[keep, ls], cur), 1, 0)
            tail_scr[keep, ls] = cur

    half_a = (-0.5 * LRU_C) * jax.nn.softplus(-lam_ref[...])
    for nb, sl in enumerate(lane_blocks):
        xb = jnp.zeros((ts, LANES), F32) + cb_ref[:, sl]
        for kk in range(CONV_WIDTH):
            start = halo - (CONV_WIDTH - 1 - kk) * SUBLANES
            xb = xb + cw_ref[kk:kk + 1, sl] * xbuf[nb, start:start + ts, :]
        xb16 = xb.astype(BF16)
        zr_half = jnp.dot(xb16, wa_ref[nb], preferred_element_type=F32) + ba_ref[:, sl]
        zi_half = jnp.dot(xb16, wi_ref[nb], preferred_element_type=F32) + bi_ref[:, sl]
        log_a = half_a[:, sl] * jnp.tanh(zr_half) + half_a[:, sl]
        ig = 0.5 * jnp.tanh(zi_half) + 0.5
        a_scr[:, sl] = jnp.exp(log_a)
        th = jnp.tanh(log_a)
        y = -2.0 * th / (1.0 - th)
        u_scr[:, sl] = jnp.where(y > 0.0, y * lax.rsqrt(y), 0.0) * (ig * xb)

    h_end = jnp.zeros((SUBLANES, tc), F32)
    p_end = jnp.ones((SUBLANES, tc), F32)
    for v in range(nv):
        rows = slice(v * SUBLANES, (v + 1) * SUBLANES)
        a = a_scr[rows, :]
        h_end = a * h_end + u_scr[rows, :]
        p_end = a * p_end
        h_scr[rows, :] = h_end
        a_scr[rows, :] = p_end
    h_in = hc_scr[...]
    entering = []
    for s in range(SUBLANES):
        entering.append(h_in)
        h_in = h_end[s:s + 1, :] + p_end[s:s + 1, :] * h_in
    hc_scr[...] = h_in
    h_enter = jnp.concatenate(entering, axis=0)

    pitch = nat_scr.shape[1] // SUBLANES
    for v in range(nv):
        rows = slice(v * SUBLANES, (v + 1) * SUBLANES)
        h_v = h_scr[rows, :] + a_scr[rows, :] * h_enter
        for cb, ls in enumerate(lane_blocks):
            nat_scr[cb, pl.ds(v, SUBLANES, stride=pitch), :] = h_v[:, ls]
    for cb, ls in enumerate(lane_blocks):
        for s in range(SUBLANES):
            seg = slice(s * nv, (s + 1) * nv)
            o_ref[seg, ls] = (jax.nn.gelu(yr_ref[seg, ls])
                              * nat_scr[cb, s * pitch:s * pitch + nv, :]).astype(o_ref.dtype)


def _rglru(xy, conv_w, conv_b, w_a, b_a, w_i, b_i, lam, batch, seq, ts=1024, tc=512):
    assert RNN_BLOCK == LANES, "the kernel walks gate blocks and 128-lane blocks together"
    n = batch * seq
    nt = seq // ts
    ncb = D_RNN // tc
    halo = (CONV_WIDTH - 1) * SUBLANES
    row = lambda v: v.reshape(1, D_RNN)
    vec_spec = pl.BlockSpec((1, tc), lambda b, c, t: (0, c))
    gate_spec = pl.BlockSpec((tc // RNN_BLOCK, RNN_BLOCK, RNN_BLOCK), lambda b, c, t: (c, 0, 0))
    return pl.pallas_call(
        _rglru_kernel,
        grid=(batch, ncb, nt),
        in_specs=[pl.BlockSpec((ts, tc), lambda b, c, t: (b * nt + t, c)),
                  pl.BlockSpec((ts, tc), lambda b, c, t: (b * nt + t, ncb + c)),
                  pl.BlockSpec((CONV_WIDTH, tc), lambda b, c, t: (0, c)),
                  vec_spec, gate_spec, gate_spec, vec_spec, vec_spec, vec_spec],
        out_specs=pl.BlockSpec((ts, tc), lambda b, c, t: (b * nt + t, c)),
        out_shape=jax.ShapeDtypeStruct((n, D_RNN), BF16),
        scratch_shapes=[pltpu.VMEM((tc // LANES, halo + ts, LANES), F32),
                        pltpu.VMEM((ts, tc), F32),
                        pltpu.VMEM((ts, tc), F32),
                        pltpu.VMEM((ts, tc), F32),
                        pltpu.VMEM((tc // LANES, ts + SUBLANES * SUBLANES, LANES), F32),
                        pltpu.VMEM((halo, tc), F32),
                        pltpu.VMEM((1, tc), F32)],
        name="rglru",
        compiler_params=_params(("parallel", "parallel", "arbitrary"), 32),
    )(xy, xy, conv_w, row(conv_b), (0.5 * w_a).astype(BF16), (0.5 * w_i).astype(BF16),
      row(0.5 * b_a), row(0.5 * b_i), row(lam))


def _moba_kernel(q_ref, k_ref, v_ref, km_ref, o_ref, bias_scr, s_scr, acc_scr, *, heads, tiles):
    first = pl.program_id(2) * tiles

    def one(u, carry):
        qrows = pl.ds(pl.multiple_of(u * MOBA_BLOCK, MOBA_BLOCK), MOBA_BLOCK)
        _moba_tile(first + u, qrows, q_ref, k_ref, v_ref, km_ref, o_ref, bias_scr, s_scr, acc_scr, heads)
        return carry

    lax.fori_loop(0, tiles, one, 0)


def _moba_tile(j, qrows, q_ref, k_ref, v_ref, km_ref, o_ref, bias_scr, s_scr, acc_scr, heads):
    nblk = km_ref.shape[0]
    blk = MOBA_BLOCK
    head_slices = [slice(h * HEAD_DIM, (h + 1) * HEAD_DIM) for h in range(heads)]

    def score_into(slot, block):
        start = pl.multiple_of(block * blk, blk)
        for h, hs in enumerate(head_slices):
            s_scr[slot, h] = lax.dot_general(k_ref[pl.ds(start, blk), hs], q_ref[qrows, hs], _NT,
                                             preferred_element_type=F32)

    ones_rows = jnp.ones((BF16_ROWS, blk), BF16)

    def attend(slot, mask, block, state):
        start = pl.multiple_of(block * blk, blk)
        half = blk // 2
        soft = []
        for h in range(heads):
            m = state[h]
            unselected = bias_scr[h, pl.ds(block, 1), :] < 0.0
            s_lo = mask(s_scr[slot, h, 0:half, :], 0)
            s_hi = mask(s_scr[slot, h, half:blk, :], half)
            m_blk = jnp.maximum(jnp.max(s_lo, axis=0, keepdims=True), jnp.max(s_hi, axis=0, keepdims=True))
            m_new = jnp.where(unselected, m, jnp.maximum(m, m_blk))
            alpha = jnp.exp2(m - m_new)
            p = jnp.exp2(mask(s_scr[slot, h], 0) - jnp.where(unselected, -NEG_INF, m_new))
            soft.append((m_new, alpha, p.astype(BF16)))
        out = []
        for h, hs in enumerate(head_slices):
            m_new, alpha, p = soft[h]
            vt = jnp.concatenate([v_ref[hs, pl.ds(start, blk)], ones_rows], axis=0)
            pv = jnp.dot(vt, p, preferred_element_type=F32)
            acc_scr[h] = alpha * acc_scr[h] + pv
            out.append(m_new)
        return tuple(out)

    no_mask = lambda s, row0: s

    gates = []
    for hs in head_slices:
        q = q_ref[qrows, hs]
        km = km_ref[:, hs]
        km_hi = km.astype(BF16)
        km_lo = (km - km_hi.astype(F32)).astype(BF16)
        gates.append(lax.dot_general(km_hi, q, _NT, preferred_element_type=F32)
                     + lax.dot_general(km_lo, q, _NT, preferred_element_type=F32))
    score_into(0, 0)
    for h, gate in enumerate(gates):
        bidx = lax.broadcasted_iota(jnp.int32, gate.shape, 0)
        past = bidx < j
        g = jnp.where(past, gate, -jnp.inf)
        rank = jnp.zeros(gate.shape, jnp.int32)
        for other in range(nblk):
            go = g[other:other + 1, :]
            beats = jnp.where(go > g, 1, jnp.where(go == g, jnp.where(bidx > other, 1, 0), 0))
            rank = rank + beats
        bias_scr[h] = jnp.where(past, jnp.where(rank < MOBA_TOPK, 0.0, NEG_INF),
                                jnp.where(bidx == j, 0.0, NEG_INF))

    def pair_body(t, state):
        first = 2 * t
        score_into(1, first + 1)
        state = attend(0, no_mask, first, state)
        score_into(0, first + 2)
        return attend(1, no_mask, first + 1, state)

    def odd_body(state):
        score_into(1, j)
        return attend(0, no_mask, j - 1, state)

    acc_scr[...] = jnp.zeros_like(acc_scr)
    init = tuple(jnp.full((1, blk), NEG_INF, F32) for _ in range(heads))
    state = lax.fori_loop(0, lax.shift_right_logical(j, 1), pair_body, init)
    odd = lax.bitwise_and(j, 1)
    state = lax.cond(odd == 1, odd_body, lambda st: st, state)

    def causal_mask(s, row0):
        kpos = row0 + lax.broadcasted_iota(jnp.int32, s.shape, 0)
        qpos = lax.broadcasted_iota(jnp.int32, s.shape, 1)
        return jnp.where(kpos <= qpos, s, NEG_INF)

    attend(odd, causal_mask, j, state)
    for h, hs in enumerate(head_slices):
        o_ref[qrows, hs] = (acc_scr[h, 0:HEAD_DIM, :] / acc_scr[h, HEAD_DIM:HEAD_DIM + 1, :]
                        ).T.astype(o_ref.dtype)


def _moba(qk, v_t, kmean, batch, seq, heads=8, tiles=4):
    n = batch * seq
    nblk = seq // MOBA_BLOCK
    groups = N_HEADS // heads
    width = heads * HEAD_DIM
    steps = nblk // tiles
    rows = tiles * MOBA_BLOCK
    return pl.pallas_call(
        functools.partial(_moba_kernel, heads=heads, tiles=tiles),
        grid=(batch, groups, steps),
        in_specs=[pl.BlockSpec((rows, width), lambda b, h, j: (b * steps + j, h)),
                  pl.BlockSpec((seq, width), lambda b, h, j: (b, groups + h)),
                  pl.BlockSpec((width, seq), lambda b, h, j: (h, b)),
                  pl.BlockSpec((nblk, width), lambda b, h, j: (b, groups + h))],
        out_specs=pl.BlockSpec((rows, width), lambda b, h, j: (b * steps + j, h)),
        out_shape=jax.ShapeDtypeStruct((n, D_ATTN), BF16),
        scratch_shapes=[pltpu.VMEM((heads, nblk, MOBA_BLOCK), F32),
                        pltpu.VMEM((2, heads, MOBA_BLOCK, MOBA_BLOCK), F32),
                        pltpu.VMEM((heads, HEAD_DIM + BF16_ROWS, MOBA_BLOCK), F32)],
        name="moba",
        compiler_params=_params(("parallel", "parallel", "arbitrary"), 56),
    )(qk, qk, v_t, kmean)


def _merge_kernel(gh_ref, o_ref, wr_ref, wa_ref, gr_ref, ga_ref, out_ref, wr_scr, wa_scr):
    _cast_weights(wr_ref, wr_scr)
    _cast_weights(wa_ref, wa_scr)
    rnn = jnp.dot(gh_ref[...], wr_scr[...], preferred_element_type=F32)
    att = jnp.dot(o_ref[...], wa_scr[...], preferred_element_type=F32)
    out_ref[...] = (gr_ref[...].astype(F32) * rnn + ga_ref[...].astype(F32) * att).astype(out_ref.dtype)


def _merge(gh, o, w_rnn, w_attn, gates, tm=1024, tn=512):
    m, k = gh.shape
    nj = D_MODEL // tn
    a_spec = pl.BlockSpec((tm, k), lambda j, i: (i, 0))
    w_spec = pl.BlockSpec((k, tn), lambda j, i: (0, j))
    return pl.pallas_call(
        _merge_kernel,
        grid=(nj, m // tm),
        in_specs=[a_spec, a_spec, w_spec, w_spec,
                  pl.BlockSpec((tm, tn), lambda j, i: (i, j)),
                  pl.BlockSpec((tm, tn), lambda j, i: (i, nj + j))],
        out_specs=pl.BlockSpec((tm, tn), lambda j, i: (i, j)),
        out_shape=jax.ShapeDtypeStruct((m, D_MODEL), BF16),
        scratch_shapes=[pltpu.VMEM((k, tn), BF16), pltpu.VMEM((k, tn), BF16)],
        name="merge",
        compiler_params=_params(("parallel", "arbitrary"), 48),
    )(gh, o, w_rnn, w_attn, gates, gates)


def _memkv_kernel(mem_ref, g_ref, w_ref, o_ref):
    hn = _rms(mem_ref[...], g_ref[...]).astype(BF16)
    o_ref[...] = jnp.dot(hn, w_ref[...], preferred_element_type=F32).astype(o_ref.dtype)


def _memkv(mem2d, g, w_kv, mem_len):
    m, d = mem2d.shape
    return pl.pallas_call(
        _memkv_kernel,
        grid=(m // mem_len,),
        in_specs=[pl.BlockSpec((mem_len, d), lambda i: (i, 0)),
                  pl.BlockSpec((1, d), lambda i: (0, 0)),
                  pl.BlockSpec((d, 2 * D_MEM), lambda i: (0, 0))],
        out_specs=pl.BlockSpec((mem_len, 2 * D_MEM), lambda i: (i, 0)),
        out_shape=jax.ShapeDtypeStruct((m, 2 * D_MEM), BF16),
        name="mem_kv",
        compiler_params=_params(("parallel",), 32),
    )(mem2d, g.reshape(1, d), w_kv)


def _xattn_kernel(mg_ref, wm_ref, x_ref, g_ref, wq_ref, kv_ref, wo_ref, o_ref):
    x = x_ref[...] + jnp.dot(mg_ref[...], wm_ref[...], preferred_element_type=F32)
    hn = _rms(x, g_ref[...]).astype(BF16)
    q = jnp.dot(hn, wq_ref[...], preferred_element_type=F32).astype(BF16)
    scale = MEM_HEAD_DIM ** -0.5
    heads = []
    for hd in range(MEM_HEADS):
        sl = slice(hd * MEM_HEAD_DIM, (hd + 1) * MEM_HEAD_DIM)
        kh = kv_ref[:, sl]
        vh = kv_ref[:, D_MEM + hd * MEM_HEAD_DIM:D_MEM + (hd + 1) * MEM_HEAD_DIM]
        s = lax.dot_general(q[:, sl], kh, _NT, preferred_element_type=F32) * scale
        m = jnp.max(s, axis=-1, keepdims=True)
        p = jnp.exp(s - m)
        l = jnp.sum(p, axis=-1, keepdims=True)
        oh = jnp.dot(p.astype(BF16), vh, preferred_element_type=F32) / l
        heads.append(oh.astype(BF16))
    o_all = jnp.concatenate(heads, axis=-1)
    o_ref[...] = x + jnp.dot(o_all, wo_ref[...], preferred_element_type=F32)


def _xattn(merged, w_mix, x, g, w_q, kv, w_o, seq, mem_len, tm=512):
    m, d = x.shape
    per_batch = seq // tm
    return pl.pallas_call(
        _xattn_kernel,
        grid=(m // tm,),
        in_specs=[pl.BlockSpec((tm, merged.shape[1]), lambda i: (i, 0)),
                  pl.BlockSpec(w_mix.shape, lambda i: (0, 0)),
                  pl.BlockSpec((tm, d), lambda i: (i, 0)),
                  pl.BlockSpec((1, d), lambda i: (0, 0)),
                  pl.BlockSpec((d, D_MEM), lambda i: (0, 0)),
                  pl.BlockSpec((mem_len, 2 * D_MEM), lambda i: (i // per_batch, 0)),
                  pl.BlockSpec((D_MEM, d), lambda i: (0, 0))],
        out_specs=pl.BlockSpec((tm, d), lambda i: (i, 0)),
        out_shape=jax.ShapeDtypeStruct((m, d), F32),
        name="mixout_xattn",
        compiler_params=_params(("parallel",), 56),
    )(merged, w_mix, x, g.reshape(1, d), w_q, kv, w_o)


def _ffn_kernel(x_ref, g_ref, wg_ref, wu_ref, wd_ref, gf_ref, o_ref, h_scr):
    f = pl.program_id(1)

    @pl.when(f == 0)
    def _():
        h_scr[...] = _rms(x_ref[...], g_ref[...]).astype(BF16)
        o_ref[...] = jnp.zeros_like(o_ref)

    hn = h_scr[...]
    a = jnp.dot(hn, wg_ref[...], preferred_element_type=F32)
    b = jnp.dot(hn, wu_ref[...], preferred_element_type=F32)
    act = (jax.nn.silu(a) * b).astype(BF16)
    o_ref[...] += jnp.dot(act, wd_ref[...], preferred_element_type=F32)

    @pl.when(f == pl.num_programs(1) - 1)
    def _():
        o_ref[...] = _rms(x_ref[...] + o_ref[...], gf_ref[...])


def _ffn(x, g, w_gate, w_up, w_down, g_final, tm=1024, tf=512):
    m, d = x.shape
    d_ff = w_gate.shape[1]
    return pl.pallas_call(
        _ffn_kernel,
        grid=(m // tm, d_ff // tf),
        in_specs=[pl.BlockSpec((tm, d), lambda i, f: (i, 0)),
                  pl.BlockSpec((1, d), lambda i, f: (0, 0)),
                  pl.BlockSpec((d, tf), lambda i, f: (0, f)),
                  pl.BlockSpec((d, tf), lambda i, f: (0, f)),
                  pl.BlockSpec((tf, d), lambda i, f: (f, 0)),
                  pl.BlockSpec((1, d), lambda i, f: (0, 0))],
        out_specs=pl.BlockSpec((tm, d), lambda i, f: (i, 0)),
        out_shape=jax.ShapeDtypeStruct((m, d), F32),
        scratch_shapes=[pltpu.VMEM((tm, d), BF16)],
        name="ffn",
        compiler_params=_params(("parallel", "arbitrary"), 63),
    )(x, g.reshape(1, d), w_gate, w_up, w_down, g_final.reshape(1, d))


def _layer(x2d, mem2d, cosf, sinf, batch, seq, mem_len, p):
    n = x2d.shape[0]
    w_in = p["w_in"]
    hn = _norm_bf16(x2d, p["norm_mix_g"])

    c0 = 0
    wide = 1024
    tall = 2048
    xy = _matmul(_mm_plain_kernel, "proj_xy", hn, w_in, col_off=c0, n_cols=2 * D_RNN, out_dtype=F32,
                 tn=wide)
    c0 += 2 * D_RNN
    tm, tn = 1024, wide
    rope_block = ((tm, LANES), lambda i, j: (i, 0))
    qk, means = _matmul(
        _mm_rope_kernel, "proj_qk", hn, w_in, col_off=c0, n_cols=2 * D_ATTN, out_dtype=BF16,
        tm=tm, tn=tn, extra=(cosf, sinf), extra_blocks=(rope_block, rope_block),
        extra_out_shape=(jax.ShapeDtypeStruct((n // tm, tm // MOBA_BLOCK, 2 * D_ATTN), F32),),
        extra_out_blocks=(((1, tm // MOBA_BLOCK, tn), lambda i, j: (i, 0, j)),))
    c0 += 2 * D_ATTN
    v_t = _matmul(_mm_transposed_kernel, "proj_v", hn, w_in, col_off=c0, n_cols=D_ATTN,
                  out_dtype=BF16, tm=tall, tn=wide, transposed_out=True, vmem_mib=60)
    c0 += D_ATTN
    gates = _matmul(_mm_sigmoid_kernel, "proj_gates", hn, w_in, col_off=c0, n_cols=2 * D_MODEL,
                    out_dtype=BF16, tm=tall, tn=wide, vmem_mib=60)

    gh = _rglru(xy, p["conv_w"], p["conv_b"], p["lru_w_a"], p["lru_b_a"], p["lru_w_i"],
                p["lru_b_i"], p["lru_lambda"], batch, seq)
    kmean = means.reshape(n // MOBA_BLOCK, 2 * D_ATTN)
    o = _moba(qk, v_t, kmean, batch, seq)

    merged = _merge(gh, o, p["w_rnn_proj"], p["w_attn_proj"], gates)
    kv = _memkv(mem2d, p["norm_mem_g"], p["w_xkv"].astype(BF16), mem_len)
    return _xattn(merged, p["w_mix_out"].astype(BF16), x2d, p["norm_xq_g"], p["w_xq"].astype(BF16),
                  kv, p["w_xo"].astype(BF16), seq, mem_len)


def kernel(x, mem, positions, norm_mix_g, w_in, conv_w, conv_b, lru_w_a, lru_b_a, lru_w_i, lru_b_i,
           lru_lambda, w_rnn_proj, w_attn_proj, w_mix_out, norm_xq_g, norm_mem_g, w_xq, w_xkv, w_xo,
           norm_ffn_g, w_ffn_gate, w_ffn_up, w_ffn_down, norm_final_g):
    batch, seq, d = x.shape
    mem_len = mem.shape[1]
    assert w_in.shape[0] == 1, "only DEPTH == 1 is supported"
    x2d = x.reshape(batch * seq, d)
    mem2d = mem.reshape(batch * mem_len, d)
    cosf, sinf = _rope_tables(positions)
    p = dict(norm_mix_g=norm_mix_g[0], w_in=w_in[0], conv_w=conv_w[0], conv_b=conv_b[0],
             lru_w_a=lru_w_a[0], lru_b_a=lru_b_a[0], lru_w_i=lru_w_i[0], lru_b_i=lru_b_i[0],
             lru_lambda=lru_lambda[0], w_rnn_proj=w_rnn_proj[0], w_attn_proj=w_attn_proj[0],
             w_mix_out=w_mix_out[0], norm_xq_g=norm_xq_g[0], norm_mem_g=norm_mem_g[0],
             w_xq=w_xq[0], w_xkv=w_xkv[0], w_xo=w_xo[0])
    x2 = _layer(x2d, mem2d, cosf, sinf, batch, seq, mem_len, p)
    out = _ffn(x2, norm_ffn_g[0], w_ffn_gate[0].astype(BF16), w_ffn_up[0].astype(BF16),
               w_ffn_down[0].astype(BF16), norm_final_g)
    return out.reshape(batch, seq, d)
```

```python
import functools

import jax
import jax.numpy as jnp
from jax import lax
from jax.experimental import pallas as pl
from jax.experimental.pallas import tpu as pltpu

D_MODEL = 2048
N_HEADS = 16
HEAD_DIM = 128
D_ATTN = N_HEADS * HEAD_DIM
MOBA_BLOCK = 256
MOBA_TOPK = 3
ROPE_THETA = 500000.0
ROT_DIM = HEAD_DIM // 4
ROT_HALF = ROT_DIM // 2
D_RNN = 2048
N_RNN_BLOCKS = 16
RNN_BLOCK = D_RNN // N_RNN_BLOCKS
CONV_WIDTH = 4
LRU_C = 8.0
MEM_HEADS = 4
MEM_HEAD_DIM = 128
D_MEM = MEM_HEADS * MEM_HEAD_DIM
RMS_EPS = 1e-6
NEG_INF = -1e30
LOG2_E = 1.4426950408889634
SOFTMAX_LOG2_SCALE = (HEAD_DIM ** -0.5) * LOG2_E

LANES = 128
SUBLANES = 8
BF16_ROWS = 16
MXU_COLS = 2 * 256
ROT_PARTNER = LANES // 2
MIB = 1024 * 1024

BF16 = jnp.bfloat16
F32 = jnp.float32

_NT = (((1,), (1,)), ((), ()))
_TN = (((0,), (0,)), ((), ()))


def _params(semantics, vmem_mib):
    return pltpu.CompilerParams(dimension_semantics=semantics,
                                vmem_limit_bytes=vmem_mib * MIB)


def _rms(x, g):
    ms = jnp.mean(x * x, axis=-1, keepdims=True)
    return x * lax.rsqrt(ms + RMS_EPS) * g


def _norm_kernel(x_ref, g_ref, o_ref):
    o_ref[...] = _rms(x_ref[...], g_ref[...]).astype(o_ref.dtype)


def _norm_bf16(x, g, tm=512):
    m, d = x.shape
    return pl.pallas_call(
        _norm_kernel,
        grid=(m // tm,),
        in_specs=[pl.BlockSpec((tm, d), lambda i: (i, 0)),
                  pl.BlockSpec((1, d), lambda i: (0, 0))],
        out_specs=pl.BlockSpec((tm, d), lambda i: (i, 0)),
        out_shape=jax.ShapeDtypeStruct((m, d), BF16),
        name="norm_mix",
        compiler_params=_params(("parallel",), 32),
    )(x, g.reshape(1, d))


def _rope_table_kernel(pos_ref, invf_ref, cos_ref, sin_ref):
    ang = pos_ref[...] * invf_ref[...]
    lane = lax.broadcasted_iota(jnp.int32, ang.shape, 1)
    s = jnp.sin(ang)
    cos_ref[...] = jnp.cos(ang)
    sin_ref[...] = jnp.where(lane < ROT_PARTNER, -s, s)


def _rope_head_layout():
    split = ROT_DIM + ROT_PARTNER - ROT_HALF
    return [(0, ROT_HALF), (ROT_DIM, split), (ROT_HALF, ROT_DIM), (split, HEAD_DIM)]


def _cast_weights(w_ref, w_scr):
    @pl.when(pl.program_id(1) == 0)
    def _():
        w_scr[...] = w_ref[...].astype(w_scr.dtype)


def _cast_qk_weights(w_ref, w_scr):
    @pl.when(pl.program_id(1) == 0)
    def _():
        rows = 256
        lane = lax.broadcasted_iota(jnp.int32, (rows, HEAD_DIM), 1)
        for r0 in range(0, w_ref.shape[0], rows):
            for h0 in range(0, w_ref.shape[1], HEAD_DIM):
                x = w_ref[r0:r0 + rows, h0:h0 + HEAD_DIM]
                out, at = x, 0
                for start, stop in _rope_head_layout():
                    if start != at:
                        moved = pltpu.roll(x, (at - start) % HEAD_DIM, 1)
                        out = jnp.where((lane >= at) & (lane < at + stop - start), moved, out)
                    at += stop - start
                w_scr[r0:r0 + rows, h0:h0 + HEAD_DIM] = out.astype(w_scr.dtype)


def _rope_tables(positions, tr=1024):
    n = positions.size
    pos = positions.astype(F32).reshape(n, 1)
    inv_freq = jnp.power(ROPE_THETA, -jnp.arange(ROT_HALF, dtype=F32) / ROT_HALF)
    gap = jnp.zeros((ROT_PARTNER - ROT_HALF,), F32)
    invf = jnp.concatenate([inv_freq, gap, inv_freq, gap]).reshape(1, LANES)
    return pl.pallas_call(
        _rope_table_kernel,
        grid=(n // tr,),
        in_specs=[pl.BlockSpec((tr, 1), lambda i: (i, 0)),
                  pl.BlockSpec((1, LANES), lambda i: (0, 0))],
        out_specs=[pl.BlockSpec((tr, LANES), lambda i: (i, 0)),
                   pl.BlockSpec((tr, LANES), lambda i: (i, 0))],
        out_shape=[jax.ShapeDtypeStruct((n, LANES), F32),
                   jax.ShapeDtypeStruct((n, LANES), F32)],
        name="rope_tables",
        compiler_params=_params(("parallel",), 32),
    )(pos, invf)


def _mm_plain_kernel(a_ref, w_ref, o_ref, w_scr):
    _cast_weights(w_ref, w_scr)
    acc = jnp.dot(a_ref[...], w_scr[...], preferred_element_type=F32)
    o_ref[...] = acc.astype(o_ref.dtype)


def _mm_transposed_kernel(a_ref, w_ref, o_ref, w_scr):
    _cast_weights(w_ref, w_scr)
    for c0 in range(0, w_scr.shape[1], MXU_COLS):
        acc = jnp.dot(a_ref[...], w_scr[:, c0:c0 + MXU_COLS], preferred_element_type=F32)
        o_ref[c0:c0 + MXU_COLS, :] = acc.T.astype(o_ref.dtype)


def _mm_rope_kernel(a_ref, w_ref, cos_ref, sin_ref, o_ref, mean_ref, w_scr):
    _cast_qk_weights(w_ref, w_scr)
    tm, tn = o_ref.shape
    q_scale = jnp.where(pl.program_id(0) < D_ATTN // tn, SOFTMAX_LOG2_SCALE, 1.0)
    cosf = cos_ref[...] * q_scale
    sinf = sin_ref[...] * q_scale
    for c0 in range(0, tn, MXU_COLS):
        acc = jnp.dot(a_ref[...], w_scr[:, c0:c0 + MXU_COLS], preferred_element_type=F32)
        for h0 in range(0, MXU_COLS, HEAD_DIM):
            a = acc[:, h0:h0 + HEAD_DIM]
            r = a * cosf + pltpu.roll(a, ROT_PARTNER, 1) * sinf
            cols = slice(c0 + h0, c0 + h0 + HEAD_DIM)
            o_ref[:, cols] = r.astype(o_ref.dtype)
            mean_ref[0, :, cols] = jnp.mean(r.reshape(tm // MOBA_BLOCK, MOBA_BLOCK, HEAD_DIM), axis=1)


def _matmul(kernel, name, a, w, *, col_off, n_cols, out_dtype, tm=1024, tn=512,
            extra=(), extra_blocks=(), extra_out_shape=(), extra_out_blocks=(), vmem_mib=48,
            transposed_out=False):
    m, k = a.shape
    off = col_off // tn
    spec = lambda shape, fn: pl.BlockSpec(shape, lambda j, i: fn(i, j))
    if transposed_out:
        main_shape, main_spec = (n_cols, m), spec((tn, tm), lambda i, j: (j, i))
    else:
        main_shape, main_spec = (m, n_cols), spec((tm, tn), lambda i, j: (i, j))
    out_shape = [jax.ShapeDtypeStruct(main_shape, out_dtype)] + list(extra_out_shape)
    out_specs = [main_spec] + [spec(*blk) for blk in extra_out_blocks]
    res = pl.pallas_call(
        kernel,
        grid=(n_cols // tn, m // tm),
        in_specs=[spec((tm, k), lambda i, j: (i, 0)),
                  spec((k, tn), lambda i, j: (0, j + off))] + [spec(*blk) for blk in extra_blocks],
        out_specs=out_specs,
        out_shape=out_shape,
        scratch_shapes=[pltpu.VMEM((k, tn), BF16)],
        name=name,
        compiler_params=_params(("parallel", "arbitrary"), vmem_mib),
    )(a, w, *extra)
    return res if extra_out_shape else res[0]


def _rglru_kernel(xr_ref, yr_ref, cw_ref, cb_ref, wa_ref, wi_ref, ba_ref, bi_ref, lam_ref,
                  o_ref, xbuf, a_scr, u_scr, h_scr, nat_scr, tail_scr, hc_scr):
    ts, tc = xr_ref.shape
    nv = ts // SUBLANES
    halo = (CONV_WIDTH - 1) * SUBLANES
    t = pl.program_id(2)

    def grp(g):
        return slice(halo + g * SUBLANES, halo + (g + 1) * SUBLANES)

    @pl.when(t == 0)
    def _():
        tail_scr[...] = jnp.zeros_like(tail_scr)
        hc_scr[...] = jnp.zeros_like(hc_scr)

    lane_blocks = [slice(cb * LANES, (cb + 1) * LANES) for cb in range(tc // LANES)]
    per_seg = nv // SUBLANES
    for cb, ls in enumerate(lane_blocks):
        for s in range(SUBLANES):
            for q in range(per_seg):
                t0 = s * nv + q * SUBLANES
                dst = pl.ds(halo + q * SUBLANES * SUBLANES + s, SUBLANES, stride=SUBLANES)
                xbuf[cb, dst, :] = xr_ref[t0:t0 + SUBLANES, ls]
    sub = lax.broadcasted_iota(jnp.int32, (SUBLANES, LANES), 0)
    for cb, ls in enumerate(lane_blocks):
        for d in range(1, CONV_WIDTH):
            keep = slice((CONV_WIDTH - 1 - d) * SUBLANES, (CONV_WIDTH - d) * SUBLANES)
            cur = xbuf[cb, grp(nv - d), :]
            xbuf[cb, grp(-d), :] = pltpu.roll(
                jnp.where(sub == SUBLANES - 1, tail_scr[keep, ls], cur), 1, 0)
            tail_scr[keep, ls] = cur

    half_a = (-0.5 * LRU_C) * jax.nn.softplus(-lam_ref[...])
    for nb, sl in enumerate(lane_blocks):
        xb = jnp.zeros((ts, LANES), F32) + cb_ref[:, sl]
        for kk in range(CONV_WIDTH):
            start = halo - (CONV_WIDTH - 1 - kk) * SUBLANES
            xb = xb + cw_ref[kk:kk + 1, sl] * xbuf[nb, start:start + ts, :]
        xb16 = xb.astype(BF16)
        zr_half = jnp.dot(xb16, wa_ref[nb], preferred_element_type=F32) + ba_ref[:, sl]
        zi_half = jnp.dot(xb16, wi_ref[nb], preferred_element_type=F32) + bi_ref[:, sl]
        log_a = half_a[:, sl] * jnp.tanh(zr_half) + half_a[:, sl]
        ig = 0.5 * jnp.tanh(zi_half) + 0.5
        a_scr[:, sl] = jnp.exp(log_a)
        th = jnp.tanh(log_a)
        y = -2.0 * th / (1.0 - th)
        u_scr[:, sl] = jnp.where(y > 0.0, y * lax.rsqrt(y), 0.0) * (ig * xb)

    h_end = jnp.zeros((SUBLANES, tc), F32)
    p_end = jnp.ones((SUBLANES, tc), F32)
    for v in range(nv):
        rows = slice(v * SUBLANES, (v + 1) * SUBLANES)
        a = a_scr[rows, :]
        h_end = a * h_end + u_scr[rows, :]
        p_end = a * p_end
        h_scr[rows, :] = h_end
        a_scr[rows, :] = p_end
    h_in = hc_scr[...]
    entering = []
    for s in range(SUBLANES):
        entering.append(h_in)
        h_in = h_end[s:s + 1, :] + p_end[s:s + 1, :] * h_in
    hc_scr[...] = h_in
    h_enter = jnp.concatenate(entering, axis=0)

    pitch = nat_scr.shape[1] // SUBLANES
    for v in range(nv):
        rows = slice(v * SUBLANES, (v + 1) * SUBLANES)
        h_v = h_scr[rows, :] + a_scr[rows, :] * h_enter
        for cb, ls in enumerate(lane_blocks):
            nat_scr[cb, pl.ds(v, SUBLANES, stride=pitch), :] = h_v[:, ls]
    for cb, ls in enumerate(lane_blocks):
        for s in range(SUBLANES):
            seg = slice(s * nv, (s + 1) * nv)
            o_ref[seg, ls] = (jax.nn.gelu(yr_ref[seg, ls])
                              * nat_scr[cb, s * pitch:s * pitch + nv, :]).astype(o_ref.dtype)


def _rglru(xy, conv_w, conv_b, w_a, b_a, w_i, b_i, lam, batch, seq, ts=512, tc=512):
    assert RNN_BLOCK == LANES, "the kernel walks gate blocks and 128-lane blocks together"
    n = batch * seq
    nt = seq // ts
    ncb = D_RNN // tc
    halo = (CONV_WIDTH - 1) * SUBLANES
    row = lambda v: v.reshape(1, D_RNN)
    vec_spec = pl.BlockSpec((1, tc), lambda b, c, t: (0, c))
    gate_spec = pl.BlockSpec((tc // RNN_BLOCK, RNN_BLOCK, RNN_BLOCK), lambda b, c, t: (c, 0, 0))
    return pl.pallas_call(
        _rglru_kernel,
        grid=(batch, ncb, nt),
        in_specs=[pl.BlockSpec((ts, tc), lambda b, c, t: (b * nt + t, c)),
                  pl.BlockSpec((ts, tc), lambda b, c, t: (b * nt + t, ncb + c)),
                  pl.BlockSpec((CONV_WIDTH, tc), lambda b, c, t: (0, c)),
                  vec_spec, gate_spec, gate_spec, vec_spec, vec_spec, vec_spec],
        out_specs=pl.BlockSpec((ts, tc), lambda b, c, t: (b * nt + t, c)),
        out_shape=jax.ShapeDtypeStruct((n, D_RNN), BF16),
        scratch_shapes=[pltpu.VMEM((tc // LANES, halo + ts, LANES), F32),
                        pltpu.VMEM((ts, tc), F32),
                        pltpu.VMEM((ts, tc), F32),
                        pltpu.VMEM((ts, tc), F32),
                        pltpu.VMEM((tc // LANES, ts + SUBLANES * SUBLANES, LANES), F32),
                        pltpu.VMEM((halo, tc), F32),
                        pltpu.VMEM((1, tc), F32)],
        name="rglru",
        compiler_params=_params(("parallel", "parallel", "arbitrary"), 32),
    )(xy, xy, conv_w, row(conv_b), (0.5 * w_a).astype(BF16), (0.5 * w_i).astype(BF16),
      row(0.5 * b_a), row(0.5 * b_i), row(lam))


def _moba_kernel(q_ref, k_ref, v_ref, km_ref, o_ref, bias_scr, s_scr, acc_scr, *, heads, tiles):
    first = pl.program_id(2) * tiles

    def one(u, carry):
        qrows = pl.ds(pl.multiple_of(u * MOBA_BLOCK, MOBA_BLOCK), MOBA_BLOCK)
        _moba_tile(first + u, qrows, q_ref, k_ref, v_ref, km_ref, o_ref, bias_scr, s_scr, acc_scr, heads)
        return carry

    lax.fori_loop(0, tiles, one, 0)


def _moba_tile(j, qrows, q_ref, k_ref, v_ref, km_ref, o_ref, bias_scr, s_scr, acc_scr, heads):
    nblk = km_ref.shape[0]
    blk = MOBA_BLOCK
    head_slices = [slice(h * HEAD_DIM, (h + 1) * HEAD_DIM) for h in range(heads)]

    def score_into(slot, block):
        start = pl.multiple_of(block * blk, blk)
        for h, hs in enumerate(head_slices):
            s_scr[slot, h] = lax.dot_general(k_ref[pl.ds(start, blk), hs], q_ref[qrows, hs], _NT,
                                             preferred_element_type=F32)

    ones_rows = jnp.ones((BF16_ROWS, blk), BF16)

    def attend(slot, mask, block, state):
        start = pl.multiple_of(block * blk, blk)
        half = blk // 2
        soft = []
        for h in range(heads):
            m = state[h]
            unselected = bias_scr[h, pl.ds(block, 1), :] < 0.0
            s_lo = mask(s_scr[slot, h, 0:half, :], 0)
            s_hi = mask(s_scr[slot, h, half:blk, :], half)
            m_blk = jnp.maximum(jnp.max(s_lo, axis=0, keepdims=True), jnp.max(s_hi, axis=0, keepdims=True))
            m_new = jnp.where(unselected, m, jnp.maximum(m, m_blk))
            alpha = jnp.exp2(m - m_new)
            p = jnp.exp2(mask(s_scr[slot, h], 0) - jnp.where(unselected, -NEG_INF, m_new))
            soft.append((m_new, alpha, p.astype(BF16)))
        out = []
        for h, hs in enumerate(head_slices):
            m_new, alpha, p = soft[h]
            vt = jnp.concatenate([v_ref[hs, pl.ds(start, blk)], ones_rows], axis=0)
            pv = jnp.dot(vt, p, preferred_element_type=F32)
            acc_scr[h] = alpha * acc_scr[h] + pv
            out.append(m_new)
        return tuple(out)

    no_mask = lambda s, row0: s

    gates = []
    for hs in head_slices:
        q = q_ref[qrows, hs]
        km = km_ref[:, hs]
        km_hi = km.astype(BF16)
        km_lo = (km - km_hi.astype(F32)).astype(BF16)
        gates.append(lax.dot_general(km_hi, q, _NT, preferred_element_type=F32)
                     + lax.dot_general(km_lo, q, _NT, preferred_element_type=F32))
    score_into(0, 0)
    for h, gate in enumerate(gates):
        bidx = lax.broadcasted_iota(jnp.int32, gate.shape, 0)
        past = bidx < j
        g = jnp.where(past, gate, -jnp.inf)
        rank = jnp.zeros(gate.shape, jnp.int32)
        for other in range(nblk):
            go = g[other:other + 1, :]
            beats = jnp.where(go > g, 1, jnp.where(go == g, jnp.where(bidx > other, 1, 0), 0))
            rank = rank + beats
        bias_scr[h] = jnp.where(past, jnp.where(rank < MOBA_TOPK, 0.0, NEG_INF),
                                jnp.where(bidx == j, 0.0, NEG_INF))

    def pair_body(t, state):
        first = 2 * t
        score_into(1, first + 1)
        state = attend(0, no_mask, first, state)
        score_into(0, first + 2)
        return attend(1, no_mask, first + 1, state)

    def odd_body(state):
        score_into(1, j)
        return attend(0, no_mask, j - 1, state)

    acc_scr[...] = jnp.zeros_like(acc_scr)
    init = tuple(jnp.full((1, blk), NEG_INF, F32) for _ in range(heads))
    state = lax.fori_loop(0, lax.shift_right_logical(j, 1), pair_body, init)
    odd = lax.bitwise_and(j, 1)
    state = lax.cond(odd == 1, odd_body, lambda st: st, state)

    def causal_mask(s, row0):
        kpos = row0 + lax.broadcasted_iota(jnp.int32, s.shape, 0)
        qpos = lax.broadcasted_iota(jnp.int32, s.shape, 1)
        return jnp.where(kpos <= qpos, s, NEG_INF)

    attend(odd, causal_mask, j, state)
    for h, hs in enumerate(head_slices):
        o_ref[qrows, hs] = (acc_scr[h, 0:HEAD_DIM, :] / acc_scr[h, HEAD_DIM:HEAD_DIM + 1, :]
                        ).T.astype(o_ref.dtype)


def _moba(qk, v_t, kmean, batch, seq, heads=8, tiles=4):
    n = batch * seq
    nblk = seq // MOBA_BLOCK
    groups = N_HEADS // heads
    width = heads * HEAD_DIM
    steps = nblk // tiles
    rows = tiles * MOBA_BLOCK
    return pl.pallas_call(
        functools.partial(_moba_kernel, heads=heads, tiles=tiles),
        grid=(batch, groups, steps),
        in_specs=[pl.BlockSpec((rows, width), lambda b, h, j: (b * steps + j, h)),
                  pl.BlockSpec((seq, width), lambda b, h, j: (b, groups + h)),
                  pl.BlockSpec((width, seq), lambda b, h, j: (h, b)),
                  pl.BlockSpec((nblk, width), lambda b, h, j: (b, groups + h))],
        out_specs=pl.BlockSpec((rows, width), lambda b, h, j: (b * steps + j, h)),
        out_shape=jax.ShapeDtypeStruct((n, D_ATTN), BF16),
        scratch_shapes=[pltpu.VMEM((heads, nblk, MOBA_BLOCK), F32),
                        pltpu.VMEM((2, heads, MOBA_BLOCK, MOBA_BLOCK), F32),
                        pltpu.VMEM((heads, HEAD_DIM + BF16_ROWS, MOBA_BLOCK), F32)],
        name="moba",
        compiler_params=_params(("parallel", "parallel", "arbitrary"), 56),
    )(qk, qk, v_t, kmean)


def _merge_kernel(gh_ref, o_ref, wr_ref, wa_ref, gr_ref, ga_ref, out_ref, wr_scr, wa_scr):
    _cast_weights(wr_ref, wr_scr)
    _cast_weights(wa_ref, wa_scr)
    sig_r = 0.5 * jnp.tanh(0.5 * gr_ref[...].astype(F32)) + 0.5
    sig_a = 0.5 * jnp.tanh(0.5 * ga_ref[...].astype(F32)) + 0.5
    rnn = jnp.dot(gh_ref[...], wr_scr[...], preferred_element_type=F32)
    att = jnp.dot(o_ref[...], wa_scr[...], preferred_element_type=F32)
    out_ref[...] = (sig_r * rnn + sig_a * att).astype(out_ref.dtype)


def _merge(gh, o, w_rnn, w_attn, gates, tm=1024, tn=512):
    m, k = gh.shape
    nj = D_MODEL // tn
    a_spec = pl.BlockSpec((tm, k), lambda j, i: (i, 0))
    w_spec = pl.BlockSpec((k, tn), lambda j, i: (0, j))
    return pl.pallas_call(
        _merge_kernel,
        grid=(nj, m // tm),
        in_specs=[a_spec, a_spec, w_spec, w_spec,
                  pl.BlockSpec((tm, tn), lambda j, i: (i, j)),
                  pl.BlockSpec((tm, tn), lambda j, i: (i, nj + j))],
        out_specs=pl.BlockSpec((tm, tn), lambda j, i: (i, j)),
        out_shape=jax.ShapeDtypeStruct((m, D_MODEL), BF16),
        scratch_shapes=[pltpu.VMEM((k, tn), BF16), pltpu.VMEM((k, tn), BF16)],
        name="merge",
        compiler_params=_params(("parallel", "arbitrary"), 48),
    )(gh, o, w_rnn, w_attn, gates, gates)


def _memkv_kernel(mem_ref, g_ref, w_ref, o_ref):
    hn = _rms(mem_ref[...], g_ref[...]).astype(BF16)
    o_ref[...] = jnp.dot(hn, w_ref[...], preferred_element_type=F32).astype(o_ref.dtype)


def _memkv(mem2d, g, w_kv, mem_len):
    m, d = mem2d.shape
    return pl.pallas_call(
        _memkv_kernel,
        grid=(m // mem_len,),
        in_specs=[pl.BlockSpec((mem_len, d), lambda i: (i, 0)),
                  pl.BlockSpec((1, d), lambda i: (0, 0)),
                  pl.BlockSpec((d, 2 * D_MEM), lambda i: (0, 0))],
        out_specs=pl.BlockSpec((mem_len, 2 * D_MEM), lambda i: (i, 0)),
        out_shape=jax.ShapeDtypeStruct((m, 2 * D_MEM), BF16),
        name="mem_kv",
        compiler_params=_params(("parallel",), 32),
    )(mem2d, g.reshape(1, d), w_kv)


def _xattn_kernel(mg_ref, wm_ref, x_ref, g_ref, wq_ref, kv_ref, wo_ref, o_ref):
    x = x_ref[...] + jnp.dot(mg_ref[...], wm_ref[...], preferred_element_type=F32)
    hn = _rms(x, g_ref[...]).astype(BF16)
    q = jnp.dot(hn, wq_ref[...], preferred_element_type=F32).astype(BF16)
    scale = MEM_HEAD_DIM ** -0.5
    heads = []
    for hd in range(MEM_HEADS):
        sl = slice(hd * MEM_HEAD_DIM, (hd + 1) * MEM_HEAD_DIM)
        kh = kv_ref[:, sl]
        vh = kv_ref[:, D_MEM + hd * MEM_HEAD_DIM:D_MEM + (hd + 1) * MEM_HEAD_DIM]
        s = lax.dot_general(q[:, sl], kh, _NT, preferred_element_type=F32) * scale
        m = jnp.max(s, axis=-1, keepdims=True)
        p = jnp.exp(s - m)
        l = jnp.sum(p, axis=-1, keepdims=True)
        oh = jnp.dot(p.astype(BF16), vh, preferred_element_type=F32) / l
        heads.append(oh.astype(BF16))
    o_all = jnp.concatenate(heads, axis=-1)
    o_ref[...] = x + jnp.dot(o_all, wo_ref[...], preferred_element_type=F32)


def _xattn(merged, w_mix, x, g, w_q, kv, w_o, seq, mem_len, tm=512):
    m, d = x.shape
    per_batch = seq // tm
    return pl.pallas_call(
        _xattn_kernel,
        grid=(m // tm,),
        in_specs=[pl.BlockSpec((tm, merged.shape[1]), lambda i: (i, 0)),
                  pl.BlockSpec(w_mix.shape, lambda i: (0, 0)),
                  pl.BlockSpec((tm, d), lambda i: (i, 0)),
                  pl.BlockSpec((1, d), lambda i: (0, 0)),
                  pl.BlockSpec((d, D_MEM), lambda i: (0, 0)),
                  pl.BlockSpec((mem_len, 2 * D_MEM), lambda i: (i // per_batch, 0)),
                  pl.BlockSpec((D_MEM, d), lambda i: (0, 0))],
        out_specs=pl.BlockSpec((tm, d), lambda i: (i, 0)),
        out_shape=jax.ShapeDtypeStruct((m, d), F32),
        name="mixout_xattn",
        compiler_params=_params(("parallel",), 56),
    )(merged, w_mix, x, g.reshape(1, d), w_q, kv, w_o)


def _ffn_kernel(x_ref, g_ref, wg_ref, wu_ref, wd_ref, gf_ref, o_ref, h_scr):
    f = pl.program_id(1)

    @pl.when(f == 0)
    def _():
        h_scr[...] = _rms(x_ref[...], g_ref[...]).astype(BF16)
        o_ref[...] = jnp.zeros_like(o_ref)

    hn = h_scr[...]
    a = jnp.dot(hn, wg_ref[...], preferred_element_type=F32)
    b = jnp.dot(hn, wu_ref[...], preferred_element_type=F32)
    act = (jax.nn.silu(a) * b).astype(BF16)
    o_ref[...] += jnp.dot(act, wd_ref[...], preferred_element_type=F32)

    @pl.when(f == pl.num_programs(1) - 1)
    def _():
        o_ref[...] = _rms(x_ref[...] + o_ref[...], gf_ref[...])


def _ffn(x, g, w_gate, w_up, w_down, g_final, tm=1024, tf=512):
    m, d = x.shape
    d_ff = w_gate.shape[1]
    return pl.pallas_call(
        _ffn_kernel,
        grid=(m // tm, d_ff // tf),
        in_specs=[pl.BlockSpec((tm, d), lambda i, f: (i, 0)),
                  pl.BlockSpec((1, d), lambda i, f: (0, 0)),
                  pl.BlockSpec((d, tf), lambda i, f: (0, f)),
                  pl.BlockSpec((d, tf), lambda i, f: (0, f)),
                  pl.BlockSpec((tf, d), lambda i, f: (f, 0)),
                  pl.BlockSpec((1, d), lambda i, f: (0, 0))],
        out_specs=pl.BlockSpec((tm, d), lambda i, f: (i, 0)),
        out_shape=jax.ShapeDtypeStruct((m, d), F32),
        scratch_shapes=[pltpu.VMEM((tm, d), BF16)],
        name="ffn",
        compiler_params=_params(("parallel", "arbitrary"), 63),
    )(x, g.reshape(1, d), w_gate, w_up, w_down, g_final.reshape(1, d))


def _layer(x2d, mem2d, cosf, sinf, batch, seq, mem_len, p):
    n = x2d.shape[0]
    w_in = p["w_in"]
    hn = _norm_bf16(x2d, p["norm_mix_g"])

    c0 = 0
    wide = 1024
    tall = 2048
    xy = _matmul(_mm_plain_kernel, "proj_xy", hn, w_in, col_off=c0, n_cols=2 * D_RNN, out_dtype=F32,
                 tn=wide)
    c0 += 2 * D_RNN
    tm, tn = 1024, wide
    rope_block = ((tm, LANES), lambda i, j: (i, 0))
    qk, means = _matmul(
        _mm_rope_kernel, "proj_qk", hn, w_in, col_off=c0, n_cols=2 * D_ATTN, out_dtype=BF16,
        tm=tm, tn=tn, extra=(cosf, sinf), extra_blocks=(rope_block, rope_block),
        extra_out_shape=(jax.ShapeDtypeStruct((n // tm, tm // MOBA_BLOCK, 2 * D_ATTN), F32),),
        extra_out_blocks=(((1, tm // MOBA_BLOCK, tn), lambda i, j: (i, 0, j)),))
    c0 += 2 * D_ATTN
    v_t = _matmul(_mm_transposed_kernel, "proj_v", hn, w_in, col_off=c0, n_cols=D_ATTN,
                  out_dtype=BF16, tm=tall, tn=wide, transposed_out=True, vmem_mib=60)
    c0 += D_ATTN
    gates = _matmul(_mm_plain_kernel, "proj_gates", hn, w_in, col_off=c0, n_cols=2 * D_MODEL,
                    out_dtype=BF16, tm=tall, tn=wide, vmem_mib=60)

    gh = _rglru(xy, p["conv_w"], p["conv_b"], p["lru_w_a"], p["lru_b_a"], p["lru_w_i"],
                p["lru_b_i"], p["lru_lambda"], batch, seq)
    kmean = means.reshape(n // MOBA_BLOCK, 2 * D_ATTN)
    o = _moba(qk, v_t, kmean, batch, seq)

    merged = _merge(gh, o, p["w_rnn_proj"], p["w_attn_proj"], gates)
    kv = _memkv(mem2d, p["norm_mem_g"], p["w_xkv"].astype(BF16), mem_len)
    return _xattn(merged, p["w_mix_out"].astype(BF16), x2d, p["norm_xq_g"], p["w_xq"].astype(BF16),
                  kv, p["w_xo"].astype(BF16), seq, mem_len)


def kernel(x, mem, positions, norm_mix_g, w_in, conv_w, conv_b, lru_w_a, lru_b_a, lru_w_i, lru_b_i,
           lru_lambda, w_rnn_proj, w_attn_proj, w_mix_out, norm_xq_g, norm_mem_g, w_xq, w_xkv, w_xo,
           norm_ffn_g, w_ffn_gate, w_ffn_up, w_ffn_down, norm_final_g):
    batch, seq, d = x.shape
    mem_len = mem.shape[1]
    assert w_in.shape[0] == 1, "only DEPTH == 1 is supported"
    x2d = x.reshape(batch * seq, d)
    mem2d = mem.reshape(batch * mem_len, d)
    cosf, sinf = _rope_tables(positions)
    p = dict(norm_mix_g=norm_mix_g[0], w_in=w_in[0], conv_w=conv_w[0], conv_b=conv_b[0],
             lru_w_a=lru_w_a[0], lru_b_a=lru_b_a[0], lru_w_i=lru_w_i[0], lru_b_i=lru_b_i[0],
             lru_lambda=lru_lambda[0], w_rnn_proj=w_rnn_proj[0], w_attn_proj=w_attn_proj[0],
             w_mix_out=w_mix_out[0], norm_xq_g=norm_xq_g[0], norm_mem_g=norm_mem_g[0],
             w_xq=w_xq[0], w_xkv=w_xkv[0], w_xo=w_xo[0])
    x2 = _layer(x2d, mem2d, cosf, sinf, batch, seq, mem_len, p)
    out = _ffn(x2, norm_ffn_g[0], w_ffn_gate[0].astype(BF16), w_ffn_up[0].astype(BF16),
               w_ffn_down[0].astype(BF16), norm_final_g)
    return out.reshape(batch, seq, d)
```
